```python
import jax
import jax.numpy as jnp
from jax import lax
import numpy as np

D_MODEL = 2048
BATCH = 2
SEQ = 4096
DEPTH = 4
DEC_BATCH = 8
DEC_SEQ = 4
PAST_LEN = 16384
PAGE_SIZE = 128

HEAD_DIM = 128
MIX_WIDTH = D_MODEL
MEM_LEN = 256
MEM_HEADS = 4
MEM_WIDTH = MEM_HEADS * HEAD_DIM
TOK_WIDTH = MIX_WIDTH - MEM_WIDTH
NSA_HEADS = TOK_WIDTH // HEAD_DIM
NSA_KV_GROUPS = 2
NSA_HPG = NSA_HEADS // NSA_KV_GROUPS
NSA_KV_SLOTS = 6
NSA_KV_WIDTH = NSA_KV_SLOTS * NSA_KV_GROUPS * HEAD_DIM
NSA_IN_WIDTH = TOK_WIDTH + NSA_KV_WIDTH + 3 * NSA_HEADS + MEM_WIDTH
CMP_STRIDE = 16
CMP_BLOCK = 2 * CMP_STRIDE
CMP_HIDDEN = HEAD_DIM
SEL_BLOCK = 64
N_SEL = 16
WINDOW = 512
QBLOCK = 128
POOL_WINDOWS = (2, 4, 8, 16)
POOL_GROUPS = len(POOL_WINDOWS)
POOL_GROUP_WIDTH = TOK_WIDTH // POOL_GROUPS
POOL_HIST = max(POOL_WINDOWS) - 1
POOL_IN_WIDTH = TOK_WIDTH + MEM_WIDTH
D_FF = 4 * D_MODEL
N_POOL_LAYERS = (DEPTH + 1) // 2
N_NSA_LAYERS = DEPTH // 2
ATTN_SCALE = HEAD_DIM ** -0.5
EPS = 1e-6
NEG = -1e30
BIG = 1e30

kernel_name = 'nsa_pool_hybrid_step'


def rms_norm(x, g):
    xf = x.astype(jnp.float32)
    y = xf * lax.rsqrt(jnp.mean(xf * xf, axis=-1, keepdims=True) + EPS)
    return (y * g.astype(jnp.float32)).astype(x.dtype)


def masked_softmax(s, mask):
    p = jax.nn.softmax(jnp.where(mask, s.astype(jnp.float32), NEG), axis=-1)
    return jnp.where(mask, p, 0.0)


def sq_relu_mlp(h, w_up, w_down):
    a = jax.nn.relu(h @ w_up)
    return (a * a) @ w_down


def memory_kv(mem, g_mem, w_kv, g_k):
    b, m, _ = mem.shape
    kv = (rms_norm(mem, g_mem) @ w_kv).reshape(b, m, 2, MEM_HEADS, HEAD_DIM)
    return jnp.stack([rms_norm(kv[:, :, 0], g_k), kv[:, :, 1]], axis=2)


def memory_attend(qm, mkv, g_q):
    b, t = qm.shape[:2]
    q = rms_norm(qm, g_q)
    s = jnp.einsum('bthd,bmhd->bhtm', q, mkv[:, :, 0]).astype(jnp.float32) * ATTN_SCALE
    p = jax.nn.softmax(s, axis=-1).astype(qm.dtype)
    return jnp.einsum('bhtm,bmhd->bthd', p, mkv[:, :, 1]).reshape(b, t, MEM_WIDTH)


def pool_token_mixer(u, pos0, w_group, scale):
    b, l, c = u.shape
    uf = u.astype(jnp.float32)
    cs = jnp.concatenate([jnp.zeros((b, 1, c), jnp.float32), jnp.cumsum(uf, axis=1)], axis=1)
    t = jnp.arange(l)
    groups = []
    for gi, w in enumerate(POOL_WINDOWS):
        c0, c1 = gi * POOL_GROUP_WIDTH, (gi + 1) * POOL_GROUP_WIDTH
        lo = jnp.maximum(t + 1 - w, 0)
        cnt = jnp.minimum(pos0 + t + 1, w).astype(jnp.float32)
        win_sum = cs[:, 1:, c0:c1] - cs[:, lo, c0:c1]
        groups.append(win_sum / cnt[None, :, None] - uf[:, :, c0:c1])
    pooled = jnp.stack(groups, axis=2).astype(u.dtype)
    mixed = jnp.einsum('blgc,gcd->blgd', pooled, w_group).reshape(b, l, c)
    return mixed * scale


def nsa_project(p, gate_bias, qk_gain):
    b, t, _ = p.shape
    q = rms_norm(p[..., :TOK_WIDTH].reshape(b, t, NSA_KV_GROUPS, NSA_HPG, HEAD_DIM), qk_gain[0])
    kv = p[..., TOK_WIDTH:TOK_WIDTH + NSA_KV_WIDTH].reshape(b, t, NSA_KV_SLOTS, NSA_KV_GROUPS, HEAD_DIM)
    kv = jnp.stack([kv[:, :, 0], kv[:, :, 1], rms_norm(kv[:, :, 2], qk_gain[2]),
                    kv[:, :, 3], rms_norm(kv[:, :, 4], qk_gain[3]), kv[:, :, 5]], axis=2)
    g0 = TOK_WIDTH + NSA_KV_WIDTH
    gates = jax.nn.sigmoid(p[..., g0:g0 + 3 * NSA_HEADS] + gate_bias).reshape(b, t, NSA_KV_GROUPS, NSA_HPG, 3)
    qm = p[..., g0 + 3 * NSA_HEADS:].reshape(b, t, MEM_HEADS, HEAD_DIM)
    return q, kv, gates, qm


def nsa_compress(rows, pos_enc, w1, w2):
    b, l, g, dh = rows.shape
    n_cmp = l // CMP_STRIDE
    half = jnp.concatenate([rows, jnp.zeros((b, CMP_STRIDE, g, dh), rows.dtype)], axis=1)
    half = half.reshape(b, n_cmp + 1, CMP_STRIDE, g, dh)
    blocks = jnp.concatenate([half[:, :-1], half[:, 1:]], axis=2) + pos_enc[None, None, :, None, :]
    flat = blocks.transpose(0, 1, 3, 2, 4).reshape(b, n_cmp, g, CMP_BLOCK * dh)
    return jax.nn.gelu(flat @ w1) @ w2


def nsa_global_branches(q, q_pos, rows, g_kc, pos_enc, w1, w2):
    b, tq = q.shape[:2]
    n_rows = rows.shape[1]
    n_cmp = n_rows // CMP_STRIDE
    n_blk = n_rows // SEL_BLOCK
    kc = rms_norm(nsa_compress(rows[:, :, 0], pos_enc[0], w1[0], w2[0]), g_kc)
    vc = nsa_compress(rows[:, :, 1], pos_enc[1], w1[1], w2[1])
    c_last = jnp.arange(n_cmp) * CMP_STRIDE + (CMP_BLOCK - 1)
    mask_c = c_last[None, :] <= q_pos[:, None]
    s_c = jnp.einsum('btgid,bcgd->bgitc', q, kc).astype(jnp.float32) * ATTN_SCALE
    p_c = masked_softmax(s_c, mask_c)
    o_cmp = jnp.einsum('bgitc,bcgd->btgid', p_c.astype(q.dtype), vc)
    imp = jnp.sum(p_c, axis=2)
    imp = imp + jnp.pad(imp[..., :-1], ((0, 0), (0, 0), (0, 0), (1, 0)))
    p_s = imp.reshape(b, NSA_KV_GROUPS, tq, n_blk, SEL_BLOCK // CMP_STRIDE).sum(-1)
    j = jnp.arange(n_blk)[None, :]
    cur = (q_pos // SEL_BLOCK)[:, None]
    forced = (j == 0) | (j == cur) | (j == cur - 1)
    score = jnp.where(forced, BIG, jnp.where(j <= cur, p_s, NEG))
    n_sel = min(N_SEL, n_blk)
    top_val, top_idx = lax.top_k(score, n_sel)
    top_ok = top_val > 0.5 * NEG
    ks = rows[:, :, 2].reshape(b, n_blk, SEL_BLOCK, NSA_KV_GROUPS, HEAD_DIM).transpose(0, 3, 1, 2, 4)
    vs = rows[:, :, 3].reshape(b, n_blk, SEL_BLOCK, NSA_KV_GROUPS, HEAD_DIM).transpose(0, 3, 1, 2, 4)
    qb = min(QBLOCK, tq)
    nq = tq // qb
    bi = jnp.arange(b)[:, None, None, None]
    gi = jnp.arange(NSA_KV_GROUPS)[None, :, None, None]
    offs = jnp.arange(SEL_BLOCK)

    def one_block(args):
        q_b, pos_b, idx_b, ok_b = args
        kg = ks[bi, gi, idx_b]
        vg = vs[bi, gi, idx_b]
        kpos = idx_b[..., None] * SEL_BLOCK + offs
        mask = ok_b[..., None] & (kpos <= pos_b[None, None, :, None, None])
        s = jnp.einsum('bqgid,bgqnkd->bgiqnk', q_b, kg).astype(jnp.float32) * ATTN_SCALE
        s = s.reshape(b, NSA_KV_GROUPS, NSA_HPG, qb, n_sel * SEL_BLOCK)
        p = masked_softmax(s, mask.reshape(b, NSA_KV_GROUPS, 1, qb, n_sel * SEL_BLOCK))
        p = p.reshape(b, NSA_KV_GROUPS, NSA_HPG, qb, n_sel, SEL_BLOCK).astype(q_b.dtype)
        return jnp.einsum('bgiqnk,bgqnkd->bqgid', p, vg)

    xs = (q.reshape(b, nq, qb, NSA_KV_GROUPS, NSA_HPG, HEAD_DIM).swapaxes(0, 1),
          q_pos.reshape(nq, qb),
          top_idx.reshape(b, NSA_KV_GROUPS, nq, qb, n_sel).transpose(2, 0, 1, 3, 4),
          top_ok.reshape(b, NSA_KV_GROUPS, nq, qb, n_sel).transpose(2, 0, 1, 3, 4))
    o_slc = lax.map(one_block, xs).swapaxes(0, 1).reshape(q.shape)
    return o_cmp, o_slc


def nsa_window(q, q_pos, k_ext, v_ext, k_pos0):
    b, tq = q.shape[:2]
    n_hist = k_ext.shape[1] - tq
    qb = min(QBLOCK, tq)
    nq = tq // qb
    span = n_hist + qb
    offs = jnp.arange(span)

    def one_block(args):
        blk, q_b, pos_b = args
        start = blk * qb
        kb = lax.dynamic_slice_in_dim(k_ext, start, span, axis=1)
        vb = lax.dynamic_slice_in_dim(v_ext, start, span, axis=1)
        kpos = k_pos0 + start + offs
        d = pos_b[:, None] - kpos[None, :]
        mask = (kpos[None, :] >= 0) & (d >= 0) & (d < WINDOW)
        s = jnp.einsum('bqgid,bkgd->bgiqk', q_b, kb).astype(jnp.float32) * ATTN_SCALE
        p = masked_softmax(s, mask).astype(q_b.dtype)
        return jnp.einsum('bgiqk,bkgd->bqgid', p, vb)

    xs = (jnp.arange(nq),
          q.reshape(b, nq, qb, NSA_KV_GROUPS, NSA_HPG, HEAD_DIM).swapaxes(0, 1),
          q_pos.reshape(nq, qb))
    return lax.map(one_block, xs).swapaxes(0, 1).reshape(q.shape)


def nsa_combine(gates, o_cmp, o_slc, o_win):
    b, t = o_cmp.shape[:2]
    o = gates[..., 0:1] * o_cmp + gates[..., 1:2] * o_slc + gates[..., 2:3] * o_win
    return o.reshape(b, t, TOK_WIDTH)


def setup_inputs(seed: int = 0) -> dict:
    key = jax.random.key(seed)
    ks = jax.random.split(key, 32)
    f32 = jnp.float32
    n_pages = PAST_LEN // PAGE_SIZE
    n_phys = (DEC_BATCH * n_pages * 5) // 4
    win_buf = min(WINDOW, PAST_LEN)

    def nrm(k, shape, scale=1.0):
        return jax.random.normal(k, shape, f32) * scale

    def gain(k, shape):
        return 1.0 + 0.05 * nrm(k, shape)

    page_table = jax.random.permutation(ks[7], n_phys)[:DEC_BATCH * n_pages]
    page_table = page_table.reshape(DEC_BATCH, n_pages).astype(jnp.int32)
    return {
        'x_prompt': nrm(ks[0], (BATCH, SEQ, D_MODEL)),
        'x_sample': nrm(ks[1], (DEC_BATCH, DEC_SEQ, D_MODEL)),
        'mem_prompt': nrm(ks[2], (BATCH, MEM_LEN, D_MODEL)),
        'cache_mem_kv': nrm(ks[3], (DEPTH, DEC_BATCH, MEM_LEN, 2, MEM_HEADS, HEAD_DIM)),
        'cache_nsa_kv': nrm(ks[4], (N_NSA_LAYERS, n_phys, PAGE_SIZE, 4, NSA_KV_GROUPS, HEAD_DIM)),
        'cache_nsa_win': nrm(ks[5], (N_NSA_LAYERS, DEC_BATCH, win_buf, 2, NSA_KV_GROUPS, HEAD_DIM)),
        'state_pool': nrm(ks[6], (N_POOL_LAYERS, DEC_BATCH, POOL_HIST, TOK_WIDTH)),
        'page_table': page_table,
        'g_norm_mix': gain(ks[8], (DEPTH, D_MODEL)),
        'g_norm_mlp': gain(ks[9], (DEPTH, D_MODEL)),
        'g_norm_mem': gain(ks[10], (DEPTH, D_MODEL)),
        'w_mem_kv': nrm(ks[11], (DEPTH, D_MODEL, 2 * MEM_WIDTH), D_MODEL ** -0.5),
        'mem_qk_gain': gain(ks[12], (DEPTH, 2, HEAD_DIM)),
        'w_out': nrm(ks[13], (DEPTH, MIX_WIDTH, D_MODEL), MIX_WIDTH ** -0.5),
        'w_mlp_up': nrm(ks[14], (DEPTH, D_MODEL, D_FF), D_MODEL ** -0.5),
        'w_mlp_down': nrm(ks[15], (DEPTH, D_FF, D_MODEL), D_FF ** -0.5),
        'w_in_pool': nrm(ks[16], (N_POOL_LAYERS, D_MODEL, POOL_IN_WIDTH), D_MODEL ** -0.5),
        'w_pool_group': nrm(ks[17], (N_POOL_LAYERS, POOL_GROUPS, POOL_GROUP_WIDTH, POOL_GROUP_WIDTH),
                            POOL_GROUP_WIDTH ** -0.5),
        'pool_scale': gain(ks[18], (N_POOL_LAYERS, TOK_WIDTH)),
        'w_in_nsa': nrm(ks[19], (N_NSA_LAYERS, D_MODEL, NSA_IN_WIDTH), D_MODEL ** -0.5),
        'nsa_gate_bias': nrm(ks[20], (N_NSA_LAYERS, 3 * NSA_HEADS), 0.1),
        'nsa_qk_gain': gain(ks[21], (N_NSA_LAYERS, 4, HEAD_DIM)),
        'cmp_pos': nrm(ks[22], (N_NSA_LAYERS, 2, CMP_BLOCK, HEAD_DIM), 0.1),
        'cmp_w1': nrm(ks[23], (N_NSA_LAYERS, 2, CMP_BLOCK * HEAD_DIM, CMP_HIDDEN), (CMP_BLOCK * HEAD_DIM) ** -0.5),
        'cmp_w2': nrm(ks[24], (N_NSA_LAYERS, 2, CMP_HIDDEN, HEAD_DIM), CMP_HIDDEN ** -0.5),
    }


def reference(x_prompt, x_sample, mem_prompt, cache_mem_kv, cache_nsa_kv, cache_nsa_win, state_pool,
              page_table, g_norm_mix, g_norm_mlp, g_norm_mem, w_mem_kv, mem_qk_gain, w_out, w_mlp_up,
              w_mlp_down, w_in_pool, w_pool_group, pool_scale, w_in_nsa, nsa_gate_bias, nsa_qk_gain,
              cmp_pos, cmp_w1, cmp_w2):
    past_len = page_table.shape[1] * cache_nsa_kv.shape[2]
    win_buf = cache_nsa_win.shape[2]
    bp, tp = x_prompt.shape[:2]
    db, ts = x_sample.shape[:2]
    pos_p = jnp.arange(tp)
    pos_s = past_len + jnp.arange(ts)
    xp, xs = x_prompt, x_sample
    mem_kv_p, nsa_kv_p, nsa_kv_s, win_p, win_s, pool_p, pool_s = [], [], [], [], [], [], []
    for i in range(DEPTH):
        li = i // 2
        mkv_p = memory_kv(mem_prompt, g_norm_mem[i], w_mem_kv[i], mem_qk_gain[i, 1])
        mem_kv_p.append(mkv_p)
        hp = rms_norm(xp, g_norm_mix[i])
        hs = rms_norm(xs, g_norm_mix[i])
        if i % 2 == 0:
            pp = hp @ w_in_pool[li]
            ps = hs @ w_in_pool[li]
            up, qmp = pp[..., :TOK_WIDTH], pp[..., TOK_WIDTH:].reshape(bp, tp, MEM_HEADS, HEAD_DIM)
            us, qms = ps[..., :TOK_WIDTH], ps[..., TOK_WIDTH:].reshape(db, ts, MEM_HEADS, HEAD_DIM)
            tok_p = pool_token_mixer(up, 0, w_pool_group[li], pool_scale[li])
            ext = jnp.concatenate([state_pool[li], us], axis=1)
            tok_s = pool_token_mixer(ext, past_len - POOL_HIST, w_pool_group[li], pool_scale[li])[:, POOL_HIST:]
            pool_p.append(up[:, -POOL_HIST:])
            pool_s.append(ext[:, -POOL_HIST:])
        else:
            q_p, kv_p, gt_p, qmp = nsa_project(hp @ w_in_nsa[li], nsa_gate_bias[li], nsa_qk_gain[li])
            q_s, kv_s, gt_s, qms = nsa_project(hs @ w_in_nsa[li], nsa_gate_bias[li], nsa_qk_gain[li])
            oc_p, os_p = nsa_global_branches(q_p, pos_p, kv_p[:, :, :4], nsa_qk_gain[li, 1],
                                             cmp_pos[li], cmp_w1[li], cmp_w2[li])
            wext_p = jnp.concatenate([jnp.zeros((bp, WINDOW, 2, NSA_KV_GROUPS, HEAD_DIM), kv_p.dtype),
                                      kv_p[:, :, 4:]], axis=1)
            ow_p = nsa_window(q_p, pos_p, wext_p[:, :, 0], wext_p[:, :, 1], -WINDOW)
            tok_p = nsa_combine(gt_p, oc_p, os_p, ow_p)
            past = cache_nsa_kv[li, page_table].reshape(db, past_len, 4, NSA_KV_GROUPS, HEAD_DIM)
            n_pad = (-(past_len + ts)) % SEL_BLOCK
            rows = jnp.concatenate([past, kv_s[:, :, :4],
                                    jnp.zeros((db, n_pad, 4, NSA_KV_GROUPS, HEAD_DIM), kv_s.dtype)], axis=1)
            oc_s, os_s = nsa_global_branches(q_s, pos_s, rows, nsa_qk_gain[li, 1],
                                             cmp_pos[li], cmp_w1[li], cmp_w2[li])
            wext_s = jnp.concatenate([cache_nsa_win[li], kv_s[:, :, 4:]], axis=1)
            ow_s = nsa_window(q_s, pos_s, wext_s[:, :, 0], wext_s[:, :, 1], past_len - win_buf)
            tok_s = nsa_combine(gt_s, oc_s, os_s, ow_s)
            nsa_kv_p.append(kv_p[:, :, :4])
            nsa_kv_s.append(kv_s[:, :, :4])
            win_p.append(kv_p[:, -min(WINDOW, tp):, 4:])
            win_s.append(wext_s[:, -win_buf:])
        mem_p = memory_attend(qmp, mkv_p, mem_qk_gain[i, 0])
        mem_s = memory_attend(qms, cache_mem_kv[i], mem_qk_gain[i, 0])
        xp = xp + jnp.concatenate([tok_p, mem_p], axis=-1) @ w_out[i]
        xs = xs + jnp.concatenate([tok_s, mem_s], axis=-1) @ w_out[i]
        xp = xp + sq_relu_mlp(rms_norm(xp, g_norm_mlp[i]), w_mlp_up[i], w_mlp_down[i])
        xs = xs + sq_relu_mlp(rms_norm(xs, g_norm_mlp[i]), w_mlp_up[i], w_mlp_down[i])
    return (xp, xs, jnp.stack(mem_kv_p), jnp.stack(nsa_kv_p), jnp.stack(nsa_kv_s),
            jnp.stack(win_p), jnp.stack(win_s), jnp.stack(pool_p), jnp.stack(pool_s))
```

```python
import functools

import jax
import jax.numpy as jnp
from jax import lax
from jax.experimental import pallas as pl
from jax.experimental.pallas import tpu as pltpu

F32 = jnp.float32
BF16 = jnp.bfloat16
I32 = jnp.int32

HEAD_DIM = 128
MEM_HEADS = 4
MEM_WIDTH = MEM_HEADS * HEAD_DIM
NSA_KV_GROUPS = 2
CMP_STRIDE = 16
CMP_BLOCK = 2 * CMP_STRIDE
SEL_BLOCK = 64
N_SEL = 16
WINDOW = 512
QBLOCK = 128
POOL_WINDOWS = (2, 4, 8, 16)
POOL_HIST = max(POOL_WINDOWS) - 1
POOL_HALO = 16
ATTN_SCALE = HEAD_DIM ** -0.5
EPS = 1e-6
NEG = -1e30
BIG = 1e30
LOWEST = -3.0e38
SEL_SHIFT = SEL_BLOCK.bit_length() - 1
RATIO_SHIFT = (SEL_BLOCK // CMP_STRIDE).bit_length() - 1
assert 1 << SEL_SHIFT == SEL_BLOCK and 1 << RATIO_SHIFT == SEL_BLOCK // CMP_STRIDE

LANE = 128
SUBLANE = 8
VMEM_LIMIT_BYTES = 56 * 1024 * 1024


def _cparams(*sem):
    return pltpu.CompilerParams(dimension_semantics=sem, vmem_limit_bytes=VMEM_LIMIT_BYTES)


def _round_up(x, m):
    return (x + m - 1) // m * m


def _rms(x, gain):
    return x * lax.rsqrt(jnp.mean(x * x, axis=-1, keepdims=True) + EPS) * gain


def _dot(a, b):
    return jnp.dot(a, b, preferred_element_type=F32)


def _dot_nt(a, b):
    return lax.dot_general(a, b, (((1,), (1,)), ((), ())), preferred_element_type=F32)


def _in_proj_body(x_ref, g_ref, w_ref, gains_ref, *out_refs, segs):
    xb = _rms(x_ref[...], g_ref[...]).astype(BF16)
    col = 0
    for o_ref, (width, norms) in zip(out_refs, segs):
        y = _dot(xb, w_ref[:, col:col + width])
        if norms is None:
            o_ref[...] = y
        else:
            for c, gi in enumerate(norms):
                yc = y[:, c * LANE:(c + 1) * LANE]
                if gi is not None:
                    yc = _rms(yc, gains_ref[gi:gi + 1, :])
                o_ref[:, c * LANE:(c + 1) * LANE] = yc
        col += width


def in_proj(x2d, g, w, gains, segs, tm):
    n, d = x2d.shape
    wtot = w.shape[1]
    assert n % tm == 0 and wtot == sum(s[0] for s in segs)
    return pl.pallas_call(
        functools.partial(_in_proj_body, segs=segs),
        grid=(n // tm,),
        in_specs=[
            pl.BlockSpec((tm, d), lambda i: (i, 0)),
            pl.BlockSpec((1, d), lambda i: (0, 0)),
            pl.BlockSpec((d, wtot), lambda i: (0, 0)),
            pl.BlockSpec(gains.shape, lambda i: (0, 0)),
        ],
        out_specs=[pl.BlockSpec((tm, s[0]), lambda i: (i, 0)) for s in segs],
        out_shape=[jax.ShapeDtypeStruct((n, s[0]), F32) for s in segs],
        compiler_params=_cparams("parallel"),
        name="in_proj",
    )(x2d, g.reshape(1, d), w, gains)


def _out_proj_body(tok_ref, mem_ref, x_ref, w_ref, o_ref):
    tw = tok_ref.shape[1]
    acc = _dot(tok_ref[...].astype(BF16), w_ref[0:tw, :])
    acc = acc + _dot(mem_ref[...].astype(BF16), w_ref[tw:, :])
    o_ref[...] = x_ref[...] + acc


def out_proj(tok, mem, x, w, tm):
    n, d = x.shape
    tw, mw = tok.shape[1], mem.shape[1]
    return pl.pallas_call(
        _out_proj_body,
        grid=(n // tm,),
        in_specs=[
            pl.BlockSpec((tm, tw), lambda i: (i, 0)),
            pl.BlockSpec((tm, mw), lambda i: (i, 0)),
            pl.BlockSpec((tm, d), lambda i: (i, 0)),
            pl.BlockSpec((tw + mw, d), lambda i: (0, 0)),
        ],
        out_specs=pl.BlockSpec((tm, d), lambda i: (i, 0)),
        out_shape=jax.ShapeDtypeStruct((n, d), F32),
        compiler_params=_cparams("parallel"),
        name="out_proj",
    )(tok, mem, x, w)


def _mlp_body(h_ref, g_ref, wu_ref, wd_ref, o_ref, xn_ref, acc_ref):
    k = pl.program_id(1)

    @pl.when(k == 0)
    def _():
        xn_ref[...] = _rms(h_ref[...], g_ref[...]).astype(BF16)
        acc_ref[...] = jnp.zeros_like(acc_ref)

    a = jnp.maximum(_dot(xn_ref[...], wu_ref[...]), 0.0)
    acc_ref[...] += _dot((a * a).astype(BF16), wd_ref[...])

    @pl.when(k == pl.num_programs(1) - 1)
    def _():
        o_ref[...] = h_ref[...] + acc_ref[...]


def mlp(h, g, w_up, w_down, tm, tf):
    n, d = h.shape
    ff = w_up.shape[1]
    return pl.pallas_call(
        _mlp_body,
        grid=(n // tm, ff // tf),
        in_specs=[
            pl.BlockSpec((tm, d), lambda i, k: (i, 0)),
            pl.BlockSpec((1, d), lambda i, k: (0, 0)),
            pl.BlockSpec((d, tf), lambda i, k: (0, k)),
            pl.BlockSpec((tf, d), lambda i, k: (k, 0)),
        ],
        out_specs=pl.BlockSpec((tm, d), lambda i, k: (i, 0)),
        out_shape=jax.ShapeDtypeStruct((n, d), F32),
        scratch_shapes=[pltpu.VMEM((tm, d), BF16), pltpu.VMEM((tm, d), F32)],
        compiler_params=_cparams("parallel", "arbitrary"),
        name="mlp",
    )(h, g.reshape(1, d), w_up, w_down)


def _mem_attend_body(q_ref, kv_ref, gq_ref, o_ref):
    for h in range(MEM_HEADS):
        sl = slice(h * HEAD_DIM, (h + 1) * HEAD_DIM)
        q = _rms(q_ref[:, sl], gq_ref[...]).astype(BF16)
        k = kv_ref[:, sl].astype(BF16)
        v = kv_ref[:, MEM_WIDTH + h * HEAD_DIM:MEM_WIDTH + (h + 1) * HEAD_DIM].astype(BF16)
        s = _dot_nt(q, k) * ATTN_SCALE
        e = jnp.exp(s - jnp.max(s, axis=-1, keepdims=True))
        o_ref[:, sl] = _dot(e.astype(BF16), v) / jnp.sum(e, axis=-1, keepdims=True)


def mem_attend(qm, mkv, gq, tq):
    b, t, _ = qm.shape
    m = mkv.shape[1]
    return pl.pallas_call(
        _mem_attend_body,
        grid=(b, t // tq),
        in_specs=[
            pl.BlockSpec((None, tq, MEM_WIDTH), lambda i, j: (i, j, 0)),
            pl.BlockSpec((None, m, 2 * MEM_WIDTH), lambda i, j: (i, 0, 0)),
            pl.BlockSpec((1, HEAD_DIM), lambda i, j: (0, 0)),
        ],
        out_specs=pl.BlockSpec((None, tq, MEM_WIDTH), lambda i, j: (i, j, 0)),
        out_shape=jax.ShapeDtypeStruct((b, t, MEM_WIDTH), F32),
        compiler_params=_cparams("parallel", "parallel"),
        name="mem_attend",
    )(qm, mkv, gq.reshape(1, HEAD_DIM))


def _pool_body(cur_ref, halo_ref, wg_ref, sc_ref, o_ref, ext_ref, *, pos0, tt, gw):
    t = pl.program_id(1)
    ext_ref[0:POOL_HALO, :] = jnp.where(t == 0, 0.0, halo_ref[...])
    ext_ref[POOL_HALO:POOL_HALO + tt, :] = cur_ref[...]
    row = t * tt + lax.broadcasted_iota(I32, (tt, 1), 0)
    for gi, w in enumerate(POOL_WINDOWS):
        c0, c1 = gi * gw, (gi + 1) * gw
        x = ext_ref[POOL_HALO:POOL_HALO + tt, c0:c1]
        s = x
        for dd in range(1, w):
            s = s + ext_ref[POOL_HALO - dd:POOL_HALO - dd + tt, c0:c1]
        cnt = jnp.minimum(pos0 + row + 1, w).astype(F32)
        pooled = s / cnt - x
        o_ref[:, c0:c1] = _dot(pooled.astype(BF16), wg_ref[gi]) * sc_ref[:, c0:c1]


def pool_mix(u, w_group, scale, pos0, tt):
    b, l, c = u.shape
    ng, gw, _ = w_group.shape
    assert l % tt == 0 and tt % POOL_HALO == 0 or l == tt
    halo_per_tile = tt // POOL_HALO if tt % POOL_HALO == 0 else 0
    return pl.pallas_call(
        functools.partial(_pool_body, pos0=pos0, tt=tt, gw=gw),
        grid=(b, l // tt),
        in_specs=[
            pl.BlockSpec((None, tt, c), lambda i, j: (i, j, 0)),
            pl.BlockSpec((None, POOL_HALO, c), lambda i, j: (i, jnp.maximum(j * halo_per_tile - 1, 0), 0)),
            pl.BlockSpec((ng, gw, gw), lambda i, j: (0, 0, 0)),
            pl.BlockSpec((1, c), lambda i, j: (0, 0)),
        ],
        out_specs=pl.BlockSpec((None, tt, c), lambda i, j: (i, j, 0)),
        out_shape=jax.ShapeDtypeStruct((b, l, c), F32),
        scratch_shapes=[pltpu.VMEM((POOL_HALO + tt, c), F32)],
        compiler_params=_cparams("parallel", "parallel"),
        name="pool_mix",
    )(u, u, w_group, scale.reshape(1, c))


def _cmp1_compute(page_refs, wk_ref, wv_ref, o_ref, stage_ref, rows):
    m = len(page_refs) * rows // CMP_STRIDE
    for slot, w_ref in ((0, wk_ref), (1, wv_ref)):
        xs = []
        for g in range(NSA_KV_GROUPS):
            j = slot * NSA_KV_GROUPS + g
            for p, pr in enumerate(page_refs):
                stage_ref[p * rows:(p + 1) * rows, :] = pr[:, j * HEAD_DIM:(j + 1) * HEAD_DIM]
            pieces = [stage_ref[pl.ds(r, m, stride=CMP_STRIDE), :] for r in range(CMP_STRIDE)]
            xs.append(jnp.concatenate(pieces, axis=1).astype(BF16))
        y = _dot(jnp.concatenate(xs, axis=0), w_ref[...])
        for g in range(NSA_KV_GROUPS):
            j = slot * NSA_KV_GROUPS + g
            o_ref[:, j * 2 * HEAD_DIM:(j + 1) * 2 * HEAD_DIM] = y[g * m:(g + 1) * m]


def _cmp1_dense_body(x_ref, wk_ref, wv_ref, o_ref, stage_ref, *, rows):
    _cmp1_compute([x_ref], wk_ref, wv_ref, o_ref, stage_ref, rows)


def cmp_stage1_dense(rows4, w1k, w1v, lane_block=0):
    b, l, _ = rows4.shape
    nh = l // CMP_STRIDE
    cw = 2 * NSA_KV_GROUPS * HEAD_DIM
    return pl.pallas_call(
        functools.partial(_cmp1_dense_body, rows=l),
        grid=(b,),
        in_specs=[
            pl.BlockSpec((None, l, cw), lambda i: (i, 0, lane_block)),
            pl.BlockSpec(w1k.shape, lambda i: (0, 0)),
            pl.BlockSpec(w1v.shape, lambda i: (0, 0)),
        ],
        out_specs=pl.BlockSpec((None, nh, 2 * cw), lambda i: (i, 0, 0)),
        out_shape=jax.ShapeDtypeStruct((b, nh, 2 * cw), F32),
        scratch_shapes=[pltpu.VMEM((l, HEAD_DIM), F32)],
        compiler_params=_cparams("parallel"),
        name="cmp_stage1_dense",
    )(rows4, w1k, w1v)


def _cmp1_paged_body(pt_ref, *refs, npg, rows):
    del pt_ref
    _cmp1_compute(list(refs[:npg]), refs[npg], refs[npg + 1], refs[npg + 2], refs[npg + 3], rows)


def _page_spec(li, k, npg, lane_block, page_size, cw):
    return pl.BlockSpec((None, None, page_size, cw),
                        lambda i, s, pt: (li, pt[i, s * npg + k], 0, lane_block))


def cmp_stage1_paged(cache, li, page_table, w1k, w1v, npg):
    b, n_pages = page_table.shape
    page = cache.shape[2]
    nh = page // CMP_STRIDE
    cw = 2 * NSA_KV_GROUPS * HEAD_DIM
    assert n_pages % npg == 0
    grid_spec = pltpu.PrefetchScalarGridSpec(
        num_scalar_prefetch=1,
        grid=(b, n_pages // npg),
        in_specs=[_page_spec(li, k, npg, 0, page, cw) for k in range(npg)] + [
            pl.BlockSpec(w1k.shape, lambda i, s, pt: (0, 0)),
            pl.BlockSpec(w1v.shape, lambda i, s, pt: (0, 0)),
        ],
        out_specs=pl.BlockSpec((None, npg * nh, 2 * cw), lambda i, s, pt: (i, s, 0)),
        scratch_shapes=[pltpu.VMEM((npg * page, HEAD_DIM), F32)],
    )
    return pl.pallas_call(
        functools.partial(_cmp1_paged_body, npg=npg, rows=page),
        grid_spec=grid_spec,
        out_shape=jax.ShapeDtypeStruct((b, n_pages * nh, 2 * cw), F32),
        compiler_params=_cparams("parallel", "arbitrary"),
        name="cmp_stage1_paged",
    )(page_table, *([cache] * npg), w1k, w1v)


def _cmp2_body(ab_ref, pos_ref, w1_ref, w2_ref, gk_ref, o_ref, *, ncp):
    for slot in range(2):
        posflat = jnp.concatenate([pos_ref[slot, r:r + 1, :] for r in range(CMP_BLOCK)], axis=1)
        posb = jnp.broadcast_to(posflat, (SUBLANE, CMP_BLOCK * HEAD_DIM)).astype(BF16)
        posc = _dot(posb, w1_ref[slot])[0:1]
        for g in range(NSA_KV_GROUPS):
            j = slot * NSA_KV_GROUPS + g
            a = ab_ref[0:ncp, j * 2 * HEAD_DIM:j * 2 * HEAD_DIM + HEAD_DIM]
            bm = ab_ref[1:ncp + 1, j * 2 * HEAD_DIM + HEAD_DIM:(j + 1) * 2 * HEAD_DIM]
            hid = jax.nn.gelu(a + bm + posc)
            out = _dot(hid.astype(BF16), w2_ref[slot])
            if slot == 0:
                out = _rms(out, gk_ref[...])
            o_ref[:, j * HEAD_DIM:(j + 1) * HEAD_DIM] = out


def cmp_stage2(ab, pos, w1, w2, gk, ncp):
    b, nrow, w = ab.shape
    assert nrow >= ncp + 1
    cw = 2 * NSA_KV_GROUPS * HEAD_DIM
    return pl.pallas_call(
        functools.partial(_cmp2_body, ncp=ncp),
        grid=(b,),
        in_specs=[
            pl.BlockSpec((None, nrow, w), lambda i: (i, 0, 0)),
            pl.BlockSpec(pos.shape, lambda i: (0, 0, 0)),
            pl.BlockSpec(w1.shape, lambda i: (0, 0, 0)),
            pl.BlockSpec(w2.shape, lambda i: (0, 0, 0)),
            pl.BlockSpec((1, HEAD_DIM), lambda i: (0, 0)),
        ],
        out_specs=pl.BlockSpec((None, ncp, cw), lambda i: (i, 0, 0)),
        out_shape=jax.ShapeDtypeStruct((b, ncp, cw), F32),
        compiler_params=_cparams("parallel"),
        name="cmp_stage2",
    )(ab, pos, w1, w2, gk.reshape(1, HEAD_DIM))


def _cmp_attn_body(q_ref, kc_ref, vc_ref, o_ref, sel_ref, *, pos0, tq, ncp, n_blk, n_sel, hpg):
    t = pl.program_id(2)
    qpos = pos0 + t * tq + lax.broadcasted_iota(I32, (tq, 1), 0)
    cidx = lax.broadcasted_iota(I32, (1, ncp), 1)
    cmask = (cidx * CMP_STRIDE + (CMP_BLOCK - 1)) <= qpos
    kc = kc_ref[...].astype(BF16)
    vc = vc_ref[...].astype(BF16)
    imp = jnp.zeros((tq, ncp), F32)
    for i in range(hpg):
        sl = slice(i * HEAD_DIM, (i + 1) * HEAD_DIM)
        s = _dot_nt(q_ref[:, sl].astype(BF16), kc) * ATTN_SCALE
        sm = jnp.where(cmask, s, NEG)
        e = jnp.exp(sm - jnp.max(sm, axis=-1, keepdims=True))
        p = jnp.where(cmask, e / jnp.sum(e, axis=-1, keepdims=True), 0.0)
        o_ref[:, sl] = _dot(p.astype(BF16), vc)
        imp = imp + p
    ratio = SEL_BLOCK // CMP_STRIDE
    imp2 = imp + jnp.where(cidx == 0, 0.0, pltpu.roll(imp, 1, axis=1))
    t1 = imp2 + pltpu.roll(imp2, ncp - 1, axis=1)
    ps = t1 + pltpu.roll(t1, ncp - 2, axis=1)
    j = cidx >> RATIO_SHIFT
    cur = qpos >> SEL_SHIFT
    forced = (j == 0) | (j == cur) | (j == cur - 1)
    score = jnp.where(forced, BIG, jnp.where(j <= cur, ps, NEG))
    score = jnp.where(((cidx & (ratio - 1)) == 0) & (j < n_blk), score, LOWEST)
    cf = cidx.astype(F32)

    def pick_one(_, carry):
        score, sel = carry
        m = jnp.max(score, axis=-1, keepdims=True)
        first = jnp.min(jnp.where(score == m, cf, float(ncp)), axis=-1, keepdims=True)
        pick = cf == first
        sel = jnp.where(pick & (m > 0.5 * NEG), 1.0, sel)
        return jnp.where(pick, LOWEST, score), sel

    _, sel = lax.fori_loop(0, n_sel, pick_one, (score, jnp.zeros((tq, ncp), F32)))
    nbp = ncp // ratio
    compact = (lax.broadcasted_iota(I32, (ncp, nbp), 0) == ratio * lax.broadcasted_iota(I32, (ncp, nbp), 1))
    sel_ref[...] = _dot(sel.astype(BF16), compact.astype(BF16))


def cmp_attend_topk(q, ct, pos0, n_blk, tq):
    b, t, qw = q.shape
    ncp = ct.shape[1]
    gwid = qw // NSA_KV_GROUPS
    hpg = gwid // HEAD_DIM
    assert SEL_BLOCK // CMP_STRIDE == 4 and ncp % LANE == 0 and t % tq == 0
    nbp = ncp // 4
    n_sel = min(N_SEL, n_blk)
    return pl.pallas_call(
        functools.partial(_cmp_attn_body, pos0=pos0, tq=tq, ncp=ncp, n_blk=n_blk, n_sel=n_sel, hpg=hpg),
        grid=(b, NSA_KV_GROUPS, t // tq),
        in_specs=[
            pl.BlockSpec((None, tq, gwid), lambda i, g, j: (i, j, g)),
            pl.BlockSpec((None, ncp, HEAD_DIM), lambda i, g, j: (i, 0, g)),
            pl.BlockSpec((None, ncp, HEAD_DIM), lambda i, g, j: (i, 0, NSA_KV_GROUPS + g)),
        ],
        out_specs=[
            pl.BlockSpec((None, tq, gwid), lambda i, g, j: (i, j, g)),
            pl.BlockSpec((None, None, tq, nbp), lambda i, g, j: (i, g, j, 0)),
        ],
        out_shape=[
            jax.ShapeDtypeStruct((b, t, qw), F32),
            jax.ShapeDtypeStruct((b, NSA_KV_GROUPS, t, nbp), F32),
        ],
        compiler_params=_cparams("parallel", "parallel", "parallel"),
        name="cmp_attend_topk",
    )(q, ct, ct)


def _stack_heads(q_ref, lane0, hpg):
    return jnp.concatenate([q_ref[:, lane0 + i * HEAD_DIM:lane0 + (i + 1) * HEAD_DIM] for i in range(hpg)],
                           axis=0).astype(BF16)


def _flash_update(q6, k, v, allowed, m_ref, l_ref, acc_ref, hpg):
    tq, tk = allowed.shape
    s3 = (_dot_nt(q6, k) * ATTN_SCALE).reshape(hpg, tq, tk)
    sm = jnp.where(allowed[None], s3, NEG).reshape(hpg * tq, tk)
    m_old = m_ref[...]
    m_new = jnp.maximum(m_old, jnp.max(sm, axis=-1, keepdims=True))
    alpha = jnp.exp(m_old - m_new)
    e3 = jnp.where(allowed[None], jnp.exp(sm - m_new).reshape(hpg, tq, tk), 0.0)
    e = e3.reshape(hpg * tq, tk)
    l_ref[...] = alpha * l_ref[...] + jnp.sum(e, axis=-1, keepdims=True)
    acc_ref[...] = alpha * acc_ref[...] + _dot(e.astype(BF16), v)
    m_ref[...] = m_new


def _block_mask(selb, k0, tk):
    nbp = selb.shape[1]
    blk = (k0 + lax.broadcasted_iota(I32, (nbp, tk), 1)) >> SEL_SHIFT
    expand = (lax.broadcasted_iota(I32, (nbp, tk), 0) == blk).astype(BF16)
    return _dot(selb, expand) > 0.5


def _slc_prompt_body(q_ref, k_ref, v_ref, sel_ref, o_ref, m_ref, l_ref, acc_ref, *, tq, tk, hpg):
    t = pl.program_id(2)
    q6 = _stack_heads(q_ref, 0, hpg)
    selb = sel_ref[...].astype(BF16)
    qpos = t * tq + lax.broadcasted_iota(I32, (tq, 1), 0)
    m_ref[...] = jnp.full_like(m_ref, NEG)
    l_ref[...] = jnp.zeros_like(l_ref)
    acc_ref[...] = jnp.zeros_like(acc_ref)

    def body(kt, carry):
        k0 = pl.multiple_of(kt * tk, tk)
        k = k_ref[pl.ds(k0, tk), :].astype(BF16)
        v = v_ref[pl.ds(k0, tk), :].astype(BF16)
        kpos = k0 + lax.broadcasted_iota(I32, (1, tk), 1)
        allowed = _block_mask(selb, k0, tk) & (kpos <= qpos)
        _flash_update(q6, k, v, allowed, m_ref, l_ref, acc_ref, hpg)
        return carry

    lax.fori_loop(0, (t * tq + tq - 1) // tk + 1, body, 0)
    o = acc_ref[...] / l_ref[...]
    for i in range(hpg):
        o_ref[:, i * HEAD_DIM:(i + 1) * HEAD_DIM] = o[i * tq:(i + 1) * tq]


def slc_attend_prompt(q, kv4, sel, tq, tk):
    b, t, qw = q.shape
    gwid = qw // NSA_KV_GROUPS
    hpg = gwid // HEAD_DIM
    nbp = sel.shape[3]
    assert t % tq == 0 and t % tk == 0
    return pl.pallas_call(
        functools.partial(_slc_prompt_body, tq=tq, tk=tk, hpg=hpg),
        grid=(b, NSA_KV_GROUPS, t // tq),
        in_specs=[
            pl.BlockSpec((None, tq, gwid), lambda i, g, j: (i, j, g)),
            pl.BlockSpec((None, t, HEAD_DIM), lambda i, g, j: (i, 0, 2 * NSA_KV_GROUPS + g)),
            pl.BlockSpec((None, t, HEAD_DIM), lambda i, g, j: (i, 0, 3 * NSA_KV_GROUPS + g)),
            pl.BlockSpec((None, None, tq, nbp), lambda i, g, j: (i, g, j, 0)),
        ],
        out_specs=pl.BlockSpec((None, tq, gwid), lambda i, g, j: (i, j, g)),
        out_shape=jax.ShapeDtypeStruct((b, t, qw), F32),
        scratch_shapes=[pltpu.VMEM((hpg * tq, 1), F32), pltpu.VMEM((hpg * tq, 1), F32),
                        pltpu.VMEM((hpg * tq, HEAD_DIM), F32)],
        compiler_params=_cparams("parallel", "parallel", "arbitrary"),
        name="slc_attend_prompt",
    )(q, kv4, kv4, sel)


def _slc_paged_body(pt_ref, q_ref, sel_ref, new_ref, *refs, npg, tq, hpg, past_len, page):
    del pt_ref
    page_refs = refs[:npg]
    o_ref, m_ref, l_ref, acc_ref = refs[npg:]
    s_id = pl.program_id(1)
    gwid = hpg * HEAD_DIM
    qpos = past_len + lax.broadcasted_iota(I32, (tq, 1), 0)

    @pl.when(s_id == 0)
    def _():
        m_ref[...] = jnp.full_like(m_ref, NEG)
        l_ref[...] = jnp.zeros_like(l_ref)
        acc_ref[...] = jnp.zeros_like(acc_ref)

    tk = npg * page
    k0 = s_id * tk
    kpos = k0 + lax.broadcasted_iota(I32, (1, tk), 1)
    for g in range(NSA_KV_GROUPS):
        q6 = _stack_heads(q_ref, g * gwid, hpg)
        selb = sel_ref[g].astype(BF16)
        k = jnp.concatenate([pr[:, g * HEAD_DIM:(g + 1) * HEAD_DIM] for pr in page_refs], axis=0).astype(BF16)
        v = jnp.concatenate([pr[:, (NSA_KV_GROUPS + g) * HEAD_DIM:(NSA_KV_GROUPS + g + 1) * HEAD_DIM]
                             for pr in page_refs], axis=0).astype(BF16)
        allowed = _block_mask(selb, k0, tk) & (kpos <= qpos)
        _flash_update(q6, k, v, allowed, m_ref.at[g], l_ref.at[g], acc_ref.at[g], hpg)

    @pl.when(s_id == pl.num_programs(1) - 1)
    def _():
        cur = past_len // SEL_BLOCK
        nn = new_ref.shape[0]
        npos = past_len + lax.broadcasted_iota(I32, (1, nn), 1)
        for g in range(NSA_KV_GROUPS):
            q6 = _stack_heads(q_ref, g * gwid, hpg)
            k = new_ref[:, g * HEAD_DIM:(g + 1) * HEAD_DIM].astype(BF16)
            v = new_ref[:, (NSA_KV_GROUPS + g) * HEAD_DIM:(NSA_KV_GROUPS + g + 1) * HEAD_DIM].astype(BF16)
            allowed = (sel_ref[g][:, cur:cur + 1] > 0.5) & (npos <= qpos)
            _flash_update(q6, k, v, allowed, m_ref.at[g], l_ref.at[g], acc_ref.at[g], hpg)
            o = acc_ref[g] / l_ref[g]
            for i in range(hpg):
                o_ref[:, g * gwid + i * HEAD_DIM:g * gwid + (i + 1) * HEAD_DIM] = o[i * tq:(i + 1) * tq]


def slc_attend_paged(q, sel, kv4_new, cache, li, page_table, npg):
    b, tq, qw = q.shape
    n_pages = page_table.shape[1]
    page = cache.shape[2]
    past_len = n_pages * page
    gwid = qw // NSA_KV_GROUPS
    hpg = gwid // HEAD_DIM
    nbp = sel.shape[3]
    cw = 2 * NSA_KV_GROUPS * HEAD_DIM
    assert n_pages % npg == 0 and past_len % SEL_BLOCK == 0 and tq <= SEL_BLOCK
    grid_spec = pltpu.PrefetchScalarGridSpec(
        num_scalar_prefetch=1,
        grid=(b, n_pages // npg),
        in_specs=[
            pl.BlockSpec((None, tq, qw), lambda i, s, pt: (i, 0, 0)),
            pl.BlockSpec((None, NSA_KV_GROUPS, tq, nbp), lambda i, s, pt: (i, 0, 0, 0)),
            pl.BlockSpec((None, tq, cw), lambda i, s, pt: (i, 0, 1)),
        ] + [_page_spec(li, k, npg, 1, page, cw) for k in range(npg)],
        out_specs=pl.BlockSpec((None, tq, qw), lambda i, s, pt: (i, 0, 0)),
        scratch_shapes=[pltpu.VMEM((NSA_KV_GROUPS, hpg * tq, 1), F32),
                        pltpu.VMEM((NSA_KV_GROUPS, hpg * tq, 1), F32),
                        pltpu.VMEM((NSA_KV_GROUPS, hpg * tq, HEAD_DIM), F32)],
    )
    return pl.pallas_call(
        functools.partial(_slc_paged_body, npg=npg, tq=tq, hpg=hpg, past_len=past_len, page=page),
        grid_spec=grid_spec,
        out_shape=jax.ShapeDtypeStruct((b, tq, qw), F32),
        compiler_params=_cparams("parallel", "arbitrary"),
        name="slc_attend_paged",
    )(page_table, q, sel, kv4_new, *([cache] * npg))


def _win_body(q_ref, k_ref, v_ref, o_ref, *, qb, span, pos0, kpos0, hpg):
    blk = pl.program_id(2)
    q6 = _stack_heads(q_ref, 0, hpg)
    start = pl.multiple_of(blk * qb, qb)
    k = k_ref[pl.ds(start, span), :].astype(BF16)
    v = v_ref[pl.ds(start, span), :].astype(BF16)
    kpos = kpos0 + blk * qb + lax.broadcasted_iota(I32, (1, span), 1)
    qpos = pos0 + blk * qb + lax.broadcasted_iota(I32, (qb, 1), 0)
    dist = qpos - kpos
    allowed = (kpos >= 0) & (dist >= 0) & (dist < WINDOW)
    s3 = (_dot_nt(q6, k) * ATTN_SCALE).reshape(hpg, qb, span)
    sm = jnp.where(allowed[None], s3, NEG)
    e3 = jnp.where(allowed[None], jnp.exp(sm - jnp.max(sm, axis=-1, keepdims=True)), 0.0)
    e = e3.reshape(hpg * qb, span)
    o = _dot(e.astype(BF16), v) / jnp.sum(e, axis=-1, keepdims=True)
    for i in range(hpg):
        o_ref[:, i * HEAD_DIM:(i + 1) * HEAD_DIM] = o[i * qb:(i + 1) * qb]


def win_attend(q, kext, qb, span, pos0, kpos0):
    b, t, qw = q.shape
    lk = kext.shape[1]
    gwid = qw // NSA_KV_GROUPS
    hpg = gwid // HEAD_DIM
    assert t % qb == 0 and (t // qb - 1) * qb + span <= lk and qb % SUBLANE == 0 and span % SUBLANE == 0
    return pl.pallas_call(
        functools.partial(_win_body, qb=qb, span=span, pos0=pos0, kpos0=kpos0, hpg=hpg),
        grid=(b, NSA_KV_GROUPS, t // qb),
        in_specs=[
            pl.BlockSpec((None, qb, gwid), lambda i, g, j: (i, j, g)),
            pl.BlockSpec((None, lk, HEAD_DIM), lambda i, g, j: (i, 0, g)),
            pl.BlockSpec((None, lk, HEAD_DIM), lambda i, g, j: (i, 0, NSA_KV_GROUPS + g)),
        ],
        out_specs=pl.BlockSpec((None, qb, gwid), lambda i, g, j: (i, j, g)),
        out_shape=jax.ShapeDtypeStruct((b, t, qw), F32),
        compiler_params=_cparams("parallel", "parallel", "parallel"),
        name="win_attend",
    )(q, kext, kext)


def _combine_body(gt_ref, bias_ref, oc_ref, os_ref, ow_ref, o_ref, *, n_heads):
    gs = jax.nn.sigmoid(gt_ref[...] + bias_ref[...])
    for h in range(n_heads):
        sl = slice(h * HEAD_DIM, (h + 1) * HEAD_DIM)
        o_ref[:, sl] = (gs[:, 3 * h:3 * h + 1] * oc_ref[:, sl] + gs[:, 3 * h + 1:3 * h + 2] * os_ref[:, sl]
                        + gs[:, 3 * h + 2:3 * h + 3] * ow_ref[:, sl])


def combine(gates, bias, o_cmp, o_slc, o_win, tm):
    n, w = o_cmp.shape
    return pl.pallas_call(
        functools.partial(_combine_body, n_heads=w // HEAD_DIM),
        grid=(n // tm,),
        in_specs=[
            pl.BlockSpec((tm, LANE), lambda i: (i, 0)),
            pl.BlockSpec((1, LANE), lambda i: (0, 0)),
            pl.BlockSpec((tm, w), lambda i: (i, 0)),
            pl.BlockSpec((tm, w), lambda i: (i, 0)),
            pl.BlockSpec((tm, w), lambda i: (i, 0)),
        ],
        out_specs=pl.BlockSpec((tm, w), lambda i: (i, 0)),
        out_shape=jax.ShapeDtypeStruct((n, w), F32),
        compiler_params=_cparams("parallel"),
        name="combine",
    )(gates, bias, o_cmp, o_slc, o_win)


def _row_tile(n, pref):
    t = min(n, pref)
    assert n % t == 0
    return t


def kernel(x_prompt, x_sample, mem_prompt, cache_mem_kv, cache_nsa_kv, cache_nsa_win, state_pool, page_table, g_norm_mix, g_norm_mlp, g_norm_mem, w_mem_kv, mem_qk_gain, w_out, w_mlp_up, w_mlp_down, w_in_pool, w_pool_group, pool_scale, w_in_nsa, nsa_gate_bias, nsa_qk_gain, cmp_pos, cmp_w1, cmp_w2):
    depth = g_norm_mix.shape[0]
    bp, tp, d = x_prompt.shape
    db, ts, _ = x_sample.shape
    n_pages = page_table.shape[1]
    page = cache_nsa_kv.shape[2]
    past_len = n_pages * page
    win_buf = cache_nsa_win.shape[2]
    tok_w = w_pool_group.shape[1] * w_pool_group.shape[2]
    n_heads = tok_w // HEAD_DIM
    n_gate = 3 * n_heads
    kvw = 4 * NSA_KV_GROUPS * HEAD_DIM
    winw = 2 * NSA_KV_GROUPS * HEAD_DIM
    assert win_buf == WINDOW and past_len % SEL_BLOCK == 0 and ts <= SUBLANE and tp % QBLOCK == 0

    tsp = SUBLANE
    xs = jnp.pad(x_sample, ((0, 0), (0, tsp - ts), (0, 0))).reshape(db * tsp, d)
    xp = x_prompt.reshape(bp * tp, d)
    n_p, n_s = bp * tp, db * tsp
    tm_p, tm_s = _row_tile(n_p, 256), n_s
    mem2d = mem_prompt.reshape(bp * mem_prompt.shape[1], d)
    mem_len = mem_prompt.shape[1]
    cache_view = cache_nsa_kv.reshape(cache_nsa_kv.shape[0], cache_nsa_kv.shape[1], page, kvw)
    npg = min(16, n_pages)

    mem_kv_p, nsa_kv_p, nsa_kv_s, win_p, win_s, pool_p, pool_s = [], [], [], [], [], [], []
    mem_segs = ((2 * MEM_WIDTH, (1,) * MEM_HEADS + (None,) * MEM_HEADS),)
    pool_segs = ((tok_w, None), (MEM_WIDTH, None))
    nsa_segs = ((tok_w, (0,) * n_heads),
                (kvw, (None, None, None, None, 2, 2, None, None)),
                (winw, (3, 3, None, None)),
                (MEM_WIDTH, None),
                (LANE, None))

    for i in range(depth):
        li = i // 2
        wo = w_out[i].astype(BF16)
        wu = w_mlp_up[i].astype(BF16)
        wd = w_mlp_down[i].astype(BF16)
        (mkv,) = in_proj(mem2d, g_norm_mem[i], w_mem_kv[i].astype(BF16), mem_qk_gain[i], mem_segs,
                         _row_tile(mem2d.shape[0], 256))
        mkv_p = mkv.reshape(bp, mem_len, 2 * MEM_WIDTH)
        mem_kv_p.append(mkv_p.reshape(bp, mem_len, 2, MEM_HEADS, HEAD_DIM))
        mkv_s = cache_mem_kv[i].reshape(db, cache_mem_kv.shape[2], 2 * MEM_WIDTH)
        if i % 2 == 0:
            w_in = w_in_pool[li].astype(BF16)
            gains = jnp.ones((1, HEAD_DIM), F32)
            up, qmp = in_proj(xp, g_norm_mix[i], w_in, gains, pool_segs, tm_p)
            us, qms = in_proj(xs, g_norm_mix[i], w_in, gains, pool_segs, tm_s)
            wg = w_pool_group[li].astype(BF16)
            up3 = up.reshape(bp, tp, tok_w)
            tok_p = pool_mix(up3, wg, pool_scale[li], 0, _row_tile(tp, 512)).reshape(n_p, tok_w)
            us3 = us.reshape(db, tsp, tok_w)[:, :ts]
            lead = _round_up(POOL_HIST + ts, SUBLANE) - (POOL_HIST + ts)
            ext = jnp.concatenate([state_pool[li], us3], axis=1)
            ext_pad = jnp.pad(ext, ((0, 0), (lead, 0), (0, 0)))
            l_ext = ext_pad.shape[1]
            tok_e = pool_mix(ext_pad, wg, pool_scale[li], past_len - POOL_HIST - lead, l_ext)
            tok_s = jnp.pad(tok_e[:, l_ext - ts:], ((0, 0), (0, tsp - ts), (0, 0))).reshape(n_s, tok_w)
            pool_p.append(up3[:, tp - POOL_HIST:])
            pool_s.append(ext[:, -POOL_HIST:])
        else:
            w = w_in_nsa[li]
            kv_end = tok_w + kvw + winw
            w_in = jnp.concatenate([w[:, :kv_end], w[:, kv_end + n_gate:], w[:, kv_end:kv_end + n_gate],
                                    jnp.zeros((d, LANE - n_gate), w.dtype)], axis=1).astype(BF16)
            gains = nsa_qk_gain[li]
            bias = jnp.pad(nsa_gate_bias[li], (0, LANE - n_gate)).reshape(1, LANE)
            w1 = cmp_w1[li]
            half = CMP_STRIDE * HEAD_DIM
            w1k = jnp.concatenate([w1[0, :half], w1[0, half:]], axis=1).astype(BF16)
            w1v = jnp.concatenate([w1[1, :half], w1[1, half:]], axis=1).astype(BF16)
            w1b = w1.astype(BF16)
            w2b = cmp_w2[li].astype(BF16)

            q_p, kv4_p, wn_p, qmp, gt_p = in_proj(xp, g_norm_mix[i], w_in, gains, nsa_segs, tm_p)
            q3 = q_p.reshape(bp, tp, tok_w)
            kv43 = kv4_p.reshape(bp, tp, kvw)
            wn3 = wn_p.reshape(bp, tp, winw)
            n_cmp = tp // CMP_STRIDE
            ncp = _round_up(n_cmp, LANE)
            ab = cmp_stage1_dense(kv43, w1k, w1v)
            ab = jnp.pad(ab, ((0, 0), (0, ncp + SUBLANE - ab.shape[1]), (0, 0)))
            ct = cmp_stage2(ab, cmp_pos[li], w1b, w2b, gains[1], ncp)
            oc_p, sel_p = cmp_attend_topk(q3, ct, 0, tp // SEL_BLOCK, _row_tile(tp, 256))
            os_p = slc_attend_prompt(q3, kv43, sel_p, QBLOCK, _row_tile(tp, 512))
            kext = jnp.concatenate([jnp.zeros((bp, WINDOW, winw), F32), wn3], axis=1)
            ow_p = win_attend(q3, kext, QBLOCK, WINDOW + QBLOCK, 0, -WINDOW)
            tok_p = combine(gt_p, bias, oc_p.reshape(n_p, tok_w), os_p.reshape(n_p, tok_w),
                            ow_p.reshape(n_p, tok_w), tm_p)

            q_s, kv4_s, wn_s, qms, gt_s = in_proj(xs, g_norm_mix[i], w_in, gains, nsa_segs, tm_s)
            qs3 = q_s.reshape(db, tsp, tok_w)
            kv4s3 = kv4_s.reshape(db, tsp, kvw)
            wns3 = wn_s.reshape(db, tsp, winw)
            new_rows = jnp.pad(kv4s3[:, :ts], ((0, 0), (0, page - ts), (0, 0)))
            n_rows = _round_up(past_len + ts, SEL_BLOCK)
            n_cmp_s = n_rows // CMP_STRIDE
            ncp_s = _round_up(n_cmp_s, LANE)
            ab_past = cmp_stage1_paged(cache_view, li, page_table, w1k, w1v, npg)
            ab_new = cmp_stage1_dense(new_rows, w1k, w1v)
            ab_s = jnp.concatenate([ab_past, ab_new], axis=1)
            ab_s = jnp.pad(ab_s, ((0, 0), (0, ncp_s + SUBLANE - ab_s.shape[1]), (0, 0)))
            ct_s = cmp_stage2(ab_s, cmp_pos[li], w1b, w2b, gains[1], ncp_s)
            oc_s, sel_s = cmp_attend_topk(qs3, ct_s, past_len, n_rows // SEL_BLOCK, tsp)
            os_s = slc_attend_paged(qs3, sel_s, kv4s3, cache_view, li, page_table, npg)
            wext = jnp.concatenate([cache_nsa_win[li].reshape(db, win_buf, winw), wns3[:, :ts]], axis=1)
            kext_s = jnp.pad(wext, ((0, 0), (0, tsp - ts), (0, 0)))
            ow_s = win_attend(qs3, kext_s, tsp, win_buf + tsp, past_len, past_len - win_buf)
            tok_s = combine(gt_s, bias, oc_s.reshape(n_s, tok_w), os_s.reshape(n_s, tok_w),
                            ow_s.reshape(n_s, tok_w), tm_s)

            nsa_kv_p.append(kv43.reshape(bp, tp, 4, NSA_KV_GROUPS, HEAD_DIM))
            nsa_kv_s.append(kv4s3[:, :ts].reshape(db, ts, 4, NSA_KV_GROUPS, HEAD_DIM))
            wlen = min(WINDOW, tp)
            win_p.append(wn3[:, tp - wlen:].reshape(bp, wlen, 2, NSA_KV_GROUPS, HEAD_DIM))
            win_s.append(wext[:, -win_buf:].reshape(db, win_buf, 2, NSA_KV_GROUPS, HEAD_DIM))

        mem_p = mem_attend(qmp.reshape(bp, tp, MEM_WIDTH), mkv_p, mem_qk_gain[i, 0], _row_tile(tp, 512))
        mem_s = mem_attend(qms.reshape(db, tsp, MEM_WIDTH), mkv_s, mem_qk_gain[i, 0], tsp)
        hp = out_proj(tok_p, mem_p.reshape(n_p, MEM_WIDTH), xp, wo, tm_p)
        hs = out_proj(tok_s, mem_s.reshape(n_s, MEM_WIDTH), xs, wo, tm_s)
        xp = mlp(hp, g_norm_mlp[i], wu, wd, _row_tile(n_p, 512), 512)
        xs = mlp(hs, g_norm_mlp[i], wu, wd, tm_s, 512)

    y_p = xp.reshape(bp, tp, d)
    y_s = xs.reshape(db, tsp, d)[:, :ts]
    return (y_p, y_s, jnp.stack(mem_kv_p), jnp.stack(nsa_kv_p), jnp.stack(nsa_kv_s),
            jnp.stack(win_p), jnp.stack(win_s), jnp.stack(pool_p), jnp.stack(pool_s))
```

```python
import functools

import jax
import jax.numpy as jnp
from jax import lax
from jax.experimental import pallas as pl
from jax.experimental.pallas import tpu as pltpu

F32 = jnp.float32
BF16 = jnp.bfloat16
I32 = jnp.int32

HEAD_DIM = 128
MEM_HEADS = 4
MEM_WIDTH = MEM_HEADS * HEAD_DIM
NSA_KV_GROUPS = 2
CMP_STRIDE = 16
CMP_BLOCK = 2 * CMP_STRIDE
SEL_BLOCK = 64
N_SEL = 16
WINDOW = 512
QBLOCK = 128
POOL_WINDOWS = (2, 4, 8, 16)
POOL_HIST = max(POOL_WINDOWS) - 1
POOL_HALO = 16
ATTN_SCALE = HEAD_DIM ** -0.5
EPS = 1e-6
NEG = -1e30
BIG = 1e30
LOWEST = -3.0e38
SEL_SHIFT = SEL_BLOCK.bit_length() - 1
RATIO_SHIFT = (SEL_BLOCK // CMP_STRIDE).bit_length() - 1
assert 1 << SEL_SHIFT == SEL_BLOCK and 1 << RATIO_SHIFT == SEL_BLOCK // CMP_STRIDE

LANE = 128
SUBLANE = 8
VMEM_LIMIT_BYTES = 56 * 1024 * 1024


def _cparams(*sem):
    return pltpu.CompilerParams(dimension_semantics=sem, vmem_limit_bytes=VMEM_LIMIT_BYTES)


def _round_up(x, m):
    return (x + m - 1) // m * m


def _rms(x, gain):
    return x * lax.rsqrt(jnp.mean(x * x, axis=-1, keepdims=True) + EPS) * gain


def _dot(a, b):
    return jnp.dot(a, b, preferred_element_type=F32)


def _dot_nt(a, b):
    return lax.dot_general(a, b, (((1,), (1,)), ((), ())), preferred_element_type=F32)


def _in_proj_body(x_ref, g_ref, w_ref, gains_ref, *out_refs, segs):
    xb = _rms(x_ref[...], g_ref[...]).astype(BF16)
    col = 0
    for o_ref, (width, norms) in zip(out_refs, segs):
        y = _dot(xb, w_ref[:, col:col + width])
        if norms is None:
            o_ref[...] = y
        else:
            for c, gi in enumerate(norms):
                yc = y[:, c * LANE:(c + 1) * LANE]
                if gi is not None:
                    yc = _rms(yc, gains_ref[gi:gi + 1, :])
                o_ref[:, c * LANE:(c + 1) * LANE] = yc
        col += width


def in_proj(x2d, g, w, gains, segs, tm):
    n, d = x2d.shape
    wtot = w.shape[1]
    assert n % tm == 0 and wtot == sum(s[0] for s in segs)
    return pl.pallas_call(
        functools.partial(_in_proj_body, segs=segs),
        grid=(n // tm,),
        in_specs=[
            pl.BlockSpec((tm, d), lambda i: (i, 0)),
            pl.BlockSpec((1, d), lambda i: (0, 0)),
            pl.BlockSpec((d, wtot), lambda i: (0, 0)),
            pl.BlockSpec(gains.shape, lambda i: (0, 0)),
        ],
        out_specs=[pl.BlockSpec((tm, s[0]), lambda i: (i, 0)) for s in segs],
        out_shape=[jax.ShapeDtypeStruct((n, s[0]), F32) for s in segs],
        compiler_params=_cparams("parallel"),
        name="in_proj",
    )(x2d, g.reshape(1, d), w, gains)


def _out_proj_body(tok_ref, mem_ref, x_ref, w_ref, o_ref):
    tw = tok_ref.shape[1]
    acc = _dot(tok_ref[...].astype(BF16), w_ref[0:tw, :])
    acc = acc + _dot(mem_ref[...].astype(BF16), w_ref[tw:, :])
    o_ref[...] = x_ref[...] + acc


def out_proj(tok, mem, x, w, tm):
    n, d = x.shape
    tw, mw = tok.shape[1], mem.shape[1]
    return pl.pallas_call(
        _out_proj_body,
        grid=(n // tm,),
        in_specs=[
            pl.BlockSpec((tm, tw), lambda i: (i, 0)),
            pl.BlockSpec((tm, mw), lambda i: (i, 0)),
            pl.BlockSpec((tm, d), lambda i: (i, 0)),
            pl.BlockSpec((tw + mw, d), lambda i: (0, 0)),
        ],
        out_specs=pl.BlockSpec((tm, d), lambda i: (i, 0)),
        out_shape=jax.ShapeDtypeStruct((n, d), F32),
        compiler_params=_cparams("parallel"),
        name="out_proj",
    )(tok, mem, x, w)


def _mlp_body(h_ref, g_ref, wu_ref, wd_ref, o_ref, xn_ref, acc_ref):
    k = pl.program_id(1)

    @pl.when(k == 0)
    def _():
        xn_ref[...] = _rms(h_ref[...], g_ref[...]).astype(BF16)
        acc_ref[...] = jnp.zeros_like(acc_ref)

    a = jnp.maximum(_dot(xn_ref[...], wu_ref[...]), 0.0)
    acc_ref[...] += _dot((a * a).astype(BF16), wd_ref[...])

    @pl.when(k == pl.num_programs(1) - 1)
    def _():
        o_ref[...] = h_ref[...] + acc_ref[...]


def mlp(h, g, w_up, w_down, tm, tf):
    n, d = h.shape
    ff = w_up.shape[1]
    return pl.pallas_call(
        _mlp_body,
        grid=(n // tm, ff // tf),
        in_specs=[
            pl.BlockSpec((tm, d), lambda i, k: (i, 0)),
            pl.BlockSpec((1, d), lambda i, k: (0, 0)),
            pl.BlockSpec((d, tf), lambda i, k: (0, k)),
            pl.BlockSpec((tf, d), lambda i, k: (k, 0)),
        ],
        out_specs=pl.BlockSpec((tm, d), lambda i, k: (i, 0)),
        out_shape=jax.ShapeDtypeStruct((n, d), F32),
        scratch_shapes=[pltpu.VMEM((tm, d), BF16), pltpu.VMEM((tm, d), F32)],
        compiler_params=_cparams("parallel", "arbitrary"),
        name="mlp",
    )(h, g.reshape(1, d), w_up, w_down)


def _mem_attend_body(q_ref, kv_ref, gq_ref, o_ref):
    for h in range(MEM_HEADS):
        sl = slice(h * HEAD_DIM, (h + 1) * HEAD_DIM)
        q = _rms(q_ref[:, sl], gq_ref[...]).astype(BF16)
        k = kv_ref[:, sl].astype(BF16)
        v = kv_ref[:, MEM_WIDTH + h * HEAD_DIM:MEM_WIDTH + (h + 1) * HEAD_DIM].astype(BF16)
        s = _dot_nt(q, k) * ATTN_SCALE
        e = jnp.exp(s - jnp.max(s, axis=-1, keepdims=True))
        o_ref[:, sl] = _dot(e.astype(BF16), v) / jnp.sum(e, axis=-1, keepdims=True)


def mem_attend(qm, mkv, gq, tq):
    b, t, _ = qm.shape
    m = mkv.shape[1]
    return pl.pallas_call(
        _mem_attend_body,
        grid=(b, t // tq),
        in_specs=[
            pl.BlockSpec((None, tq, MEM_WIDTH), lambda i, j: (i, j, 0)),
            pl.BlockSpec((None, m, 2 * MEM_WIDTH), lambda i, j: (i, 0, 0)),
            pl.BlockSpec((1, HEAD_DIM), lambda i, j: (0, 0)),
        ],
        out_specs=pl.BlockSpec((None, tq, MEM_WIDTH), lambda i, j: (i, j, 0)),
        out_shape=jax.ShapeDtypeStruct((b, t, MEM_WIDTH), F32),
        compiler_params=_cparams("parallel", "parallel"),
        name="mem_attend",
    )(qm, mkv, gq.reshape(1, HEAD_DIM))


def _pool_body(cur_ref, halo_ref, wg_ref, sc_ref, o_ref, ext_ref, *, pos0, tt, gw):
    t = pl.program_id(1)
    ext_ref[0:POOL_HALO, :] = jnp.where(t == 0, 0.0, halo_ref[...])
    ext_ref[POOL_HALO:POOL_HALO + tt, :] = cur_ref[...]
    row = t * tt + lax.broadcasted_iota(I32, (tt, 1), 0)
    for gi, w in enumerate(POOL_WINDOWS):
        c0, c1 = gi * gw, (gi + 1) * gw
        x = ext_ref[POOL_HALO:POOL_HALO + tt, c0:c1]
        s = x
        for dd in range(1, w):
            s = s + ext_ref[POOL_HALO - dd:POOL_HALO - dd + tt, c0:c1]
        cnt = jnp.minimum(pos0 + row + 1, w).astype(F32)
        pooled = s / cnt - x
        o_ref[:, c0:c1] = _dot(pooled.astype(BF16), wg_ref[gi]) * sc_ref[:, c0:c1]


def pool_mix(u, w_group, scale, pos0, tt):
    b, l, c = u.shape
    ng, gw, _ = w_group.shape
    assert l % tt == 0 and tt % POOL_HALO == 0 or l == tt
    halo_per_tile = tt // POOL_HALO if tt % POOL_HALO == 0 else 0
    return pl.pallas_call(
        functools.partial(_pool_body, pos0=pos0, tt=tt, gw=gw),
        grid=(b, l // tt),
        in_specs=[
            pl.BlockSpec((None, tt, c), lambda i, j: (i, j, 0)),
            pl.BlockSpec((None, POOL_HALO, c), lambda i, j: (i, jnp.maximum(j * halo_per_tile - 1, 0), 0)),
            pl.BlockSpec((ng, gw, gw), lambda i, j: (0, 0, 0)),
            pl.BlockSpec((1, c), lambda i, j: (0, 0)),
        ],
        out_specs=pl.BlockSpec((None, tt, c), lambda i, j: (i, j, 0)),
        out_shape=jax.ShapeDtypeStruct((b, l, c), F32),
        scratch_shapes=[pltpu.VMEM((POOL_HALO + tt, c), F32)],
        compiler_params=_cparams("parallel", "parallel"),
        name="pool_mix",
    )(u, u, w_group, scale.reshape(1, c))


def _cmp1_compute(page_refs, wk_ref, wv_ref, o_ref, stage_ref, rows):
    m = len(page_refs) * rows // CMP_STRIDE
    for slot, w_ref in ((0, wk_ref), (1, wv_ref)):
        xs = []
        for g in range(NSA_KV_GROUPS):
            j = slot * NSA_KV_GROUPS + g
            for p, pr in enumerate(page_refs):
                stage_ref[p * rows:(p + 1) * rows, :] = pr[:, j * HEAD_DIM:(j + 1) * HEAD_DIM]
            pieces = [stage_ref[pl.ds(r, m, stride=CMP_STRIDE), :] for r in range(CMP_STRIDE)]
            xs.append(jnp.concatenate(pieces, axis=1).astype(BF16))
        y = _dot(jnp.concatenate(xs, axis=0), w_ref[...])
        for g in range(NSA_KV_GROUPS):
            j = slot * NSA_KV_GROUPS + g
            o_ref[:, j * 2 * HEAD_DIM:(j + 1) * 2 * HEAD_DIM] = y[g * m:(g + 1) * m]


def _cmp1_dense_body(x_ref, wk_ref, wv_ref, o_ref, stage_ref, *, rows):
    _cmp1_compute([x_ref], wk_ref, wv_ref, o_ref, stage_ref, rows)


def cmp_stage1_dense(rows4, w1k, w1v, lane_block=0):
    b, l, _ = rows4.shape
    nh = l // CMP_STRIDE
    cw = 2 * NSA_KV_GROUPS * HEAD_DIM
    return pl.pallas_call(
        functools.partial(_cmp1_dense_body, rows=l),
        grid=(b,),
        in_specs=[
            pl.BlockSpec((None, l, cw), lambda i: (i, 0, lane_block)),
            pl.BlockSpec(w1k.shape, lambda i: (0, 0)),
            pl.BlockSpec(w1v.shape, lambda i: (0, 0)),
        ],
        out_specs=pl.BlockSpec((None, nh, 2 * cw), lambda i: (i, 0, 0)),
        out_shape=jax.ShapeDtypeStruct((b, nh, 2 * cw), F32),
        scratch_shapes=[pltpu.VMEM((l, HEAD_DIM), F32)],
        compiler_params=_cparams("parallel"),
        name="cmp_stage1_dense",
    )(rows4, w1k, w1v)


PAGE_CHUNKS = 4 * NSA_KV_GROUPS


def _cmp1_paged_body(pt_ref, *refs, npg, page):
    del pt_ref
    page_refs = refs[:npg]
    wk_ref, wv_ref, o_ref = refs[npg:npg + 3]
    nh = page // CMP_STRIDE
    m = npg * nh
    for slot, w_ref in ((0, wk_ref), (1, wv_ref)):
        xs = []
        for g in range(NSA_KV_GROUPS):
            j = slot * NSA_KV_GROUPS + g
            per_page = []
            for pr in page_refs:
                pieces = [pr[pl.ds(r * PAGE_CHUNKS + j, nh, stride=CMP_STRIDE * PAGE_CHUNKS), :]
                          for r in range(CMP_STRIDE)]
                per_page.append(jnp.concatenate(pieces, axis=1))
            xs.append(jnp.concatenate(per_page, axis=0).astype(BF16))
        y = _dot(jnp.concatenate(xs, axis=0), w_ref[...])
        for g in range(NSA_KV_GROUPS):
            j = slot * NSA_KV_GROUPS + g
            o_ref[:, j * 2 * HEAD_DIM:(j + 1) * 2 * HEAD_DIM] = y[g * m:(g + 1) * m]


def _page_spec(li, k, npg, page):
    return pl.BlockSpec((None, None, page * PAGE_CHUNKS, HEAD_DIM),
                        lambda i, s, pt: (li, pt[i, s * npg + k], 0, 0))


def cmp_stage1_paged(cache, li, page_table, w1k, w1v, npg):
    b, n_pages = page_table.shape
    page = cache.shape[2] // PAGE_CHUNKS
    nh = page // CMP_STRIDE
    cw = 2 * NSA_KV_GROUPS * HEAD_DIM
    assert n_pages % npg == 0
    grid_spec = pltpu.PrefetchScalarGridSpec(
        num_scalar_prefetch=1,
        grid=(b, n_pages // npg),
        in_specs=[_page_spec(li, k, npg, page) for k in range(npg)] + [
            pl.BlockSpec(w1k.shape, lambda i, s, pt: (0, 0)),
            pl.BlockSpec(w1v.shape, lambda i, s, pt: (0, 0)),
        ],
        out_specs=pl.BlockSpec((None, npg * nh, 2 * cw), lambda i, s, pt: (i, s, 0)),
    )
    return pl.pallas_call(
        functools.partial(_cmp1_paged_body, npg=npg, page=page),
        grid_spec=grid_spec,
        out_shape=jax.ShapeDtypeStruct((b, n_pages * nh, 2 * cw), F32),
        compiler_params=_cparams("parallel", "arbitrary"),
        name="cmp_stage1_paged",
    )(page_table, *([cache] * npg), w1k, w1v)


def _cmp2_body(ab_ref, pos_ref, w1_ref, w2_ref, gk_ref, o_ref, *, ncp):
    for slot in range(2):
        posflat = jnp.concatenate([pos_ref[slot, r:r + 1, :] for r in range(CMP_BLOCK)], axis=1)
        posb = jnp.broadcast_to(posflat, (SUBLANE, CMP_BLOCK * HEAD_DIM)).astype(BF16)
        posc = _dot(posb, w1_ref[slot])[0:1]
        for g in range(NSA_KV_GROUPS):
            j = slot * NSA_KV_GROUPS + g
            a = ab_ref[0:ncp, j * 2 * HEAD_DIM:j * 2 * HEAD_DIM + HEAD_DIM]
            bm = ab_ref[1:ncp + 1, j * 2 * HEAD_DIM + HEAD_DIM:(j + 1) * 2 * HEAD_DIM]
            hid = jax.nn.gelu(a + bm + posc)
            out = _dot(hid.astype(BF16), w2_ref[slot])
            if slot == 0:
                out = _rms(out, gk_ref[...])
            o_ref[:, j * HEAD_DIM:(j + 1) * HEAD_DIM] = out


def cmp_stage2(ab, pos, w1, w2, gk, ncp):
    b, nrow, w = ab.shape
    assert nrow >= ncp + 1
    cw = 2 * NSA_KV_GROUPS * HEAD_DIM
    return pl.pallas_call(
        functools.partial(_cmp2_body, ncp=ncp),
        grid=(b,),
        in_specs=[
            pl.BlockSpec((None, nrow, w), lambda i: (i, 0, 0)),
            pl.BlockSpec(pos.shape, lambda i: (0, 0, 0)),
            pl.BlockSpec(w1.shape, lambda i: (0, 0, 0)),
            pl.BlockSpec(w2.shape, lambda i: (0, 0, 0)),
            pl.BlockSpec((1, HEAD_DIM), lambda i: (0, 0)),
        ],
        out_specs=pl.BlockSpec((None, ncp, cw), lambda i: (i, 0, 0)),
        out_shape=jax.ShapeDtypeStruct((b, ncp, cw), F32),
        compiler_params=_cparams("parallel"),
        name="cmp_stage2",
    )(ab, pos, w1, w2, gk.reshape(1, HEAD_DIM))


def _cmp_attn_body(q_ref, kc_ref, vc_ref, o_ref, sel_ref, *, pos0, tq, ncp, n_blk, n_sel, hpg):
    t = pl.program_id(2)
    qpos = pos0 + t * tq + lax.broadcasted_iota(I32, (tq, 1), 0)
    cidx = lax.broadcasted_iota(I32, (1, ncp), 1)
    cmask = (cidx * CMP_STRIDE + (CMP_BLOCK - 1)) <= qpos
    kc = kc_ref[...].astype(BF16)
    vc = vc_ref[...].astype(BF16)
    imp = jnp.zeros((tq, ncp), F32)
    for i in range(hpg):
        sl = slice(i * HEAD_DIM, (i + 1) * HEAD_DIM)
        s = _dot_nt(q_ref[:, sl].astype(BF16), kc) * ATTN_SCALE
        sm = jnp.where(cmask, s, NEG)
        e = jnp.exp(sm - jnp.max(sm, axis=-1, keepdims=True))
        p = jnp.where(cmask, e / jnp.sum(e, axis=-1, keepdims=True), 0.0)
        o_ref[:, sl] = _dot(p.astype(BF16), vc)
        imp = imp + p
    ratio = SEL_BLOCK // CMP_STRIDE
    imp2 = imp + jnp.where(cidx == 0, 0.0, pltpu.roll(imp, 1, axis=1))
    t1 = imp2 + pltpu.roll(imp2, ncp - 1, axis=1)
    ps = t1 + pltpu.roll(t1, ncp - 2, axis=1)
    j = cidx >> RATIO_SHIFT
    cur = qpos >> SEL_SHIFT
    forced = (j == 0) | (j == cur) | (j == cur - 1)
    score = jnp.where(forced, BIG, jnp.where(j <= cur, ps, NEG))
    score = jnp.where(((cidx & (ratio - 1)) == 0) & (j < n_blk), score, LOWEST)
    cf = cidx.astype(F32)

    def pick_one(_, carry):
        score, sel = carry
        m = jnp.max(score, axis=-1, keepdims=True)
        first = jnp.min(jnp.where(score == m, cf, float(ncp)), axis=-1, keepdims=True)
        pick = cf == first
        sel = jnp.where(pick & (m > 0.5 * NEG), 1.0, sel)
        return jnp.where(pick, LOWEST, score), sel

    _, sel = lax.fori_loop(0, n_sel, pick_one, (score, jnp.zeros((tq, ncp), F32)))
    nbp = ncp // ratio
    compact = (lax.broadcasted_iota(I32, (ncp, nbp), 0) == ratio * lax.broadcasted_iota(I32, (ncp, nbp), 1))
    sel_ref[...] = _dot(sel.astype(BF16), compact.astype(BF16))


def cmp_attend_topk(q, ct, pos0, n_blk, tq):
    b, t, qw = q.shape
    ncp = ct.shape[1]
    gwid = qw // NSA_KV_GROUPS
    hpg = gwid // HEAD_DIM
    assert SEL_BLOCK // CMP_STRIDE == 4 and ncp % LANE == 0 and t % tq == 0
    nbp = ncp // 4
    n_sel = min(N_SEL, n_blk)
    return pl.pallas_call(
        functools.partial(_cmp_attn_body, pos0=pos0, tq=tq, ncp=ncp, n_blk=n_blk, n_sel=n_sel, hpg=hpg),
        grid=(b, NSA_KV_GROUPS, t // tq),
        in_specs=[
            pl.BlockSpec((None, tq, gwid), lambda i, g, j: (i, j, g)),
            pl.BlockSpec((None, ncp, HEAD_DIM), lambda i, g, j: (i, 0, g)),
            pl.BlockSpec((None, ncp, HEAD_DIM), lambda i, g, j: (i, 0, NSA_KV_GROUPS + g)),
        ],
        out_specs=[
            pl.BlockSpec((None, tq, gwid), lambda i, g, j: (i, j, g)),
            pl.BlockSpec((None, None, tq, nbp), lambda i, g, j: (i, g, j, 0)),
        ],
        out_shape=[
            jax.ShapeDtypeStruct((b, t, qw), F32),
            jax.ShapeDtypeStruct((b, NSA_KV_GROUPS, t, nbp), F32),
        ],
        compiler_params=_cparams("parallel", "parallel", "parallel"),
        name="cmp_attend_topk",
    )(q, ct, ct)


def _stack_heads(q_ref, lane0, hpg):
    return jnp.concatenate([q_ref[:, lane0 + i * HEAD_DIM:lane0 + (i + 1) * HEAD_DIM] for i in range(hpg)],
                           axis=0).astype(BF16)


def _prob_dtype(rows_per_head):
    return BF16 if rows_per_head % (2 * SUBLANE) == 0 else F32


def _flash_init(m_ref, l_ref, acc_ref):
    m_ref[...] = jnp.full_like(m_ref, NEG)
    l_ref[...] = jnp.zeros_like(l_ref)
    acc_ref[...] = jnp.zeros_like(acc_ref)


def _flash_tile(q6, k, v, bias, m_ref, l_ref, acc_ref, e_ref, hpg):
    tq, tk = bias.shape
    assert tk % LANE == 0 and acc_ref.shape[1] == LANE
    s_all = _dot_nt(q6, k)
    for i in range(hpg):
        rows = slice(i * tq, (i + 1) * tq)
        s = s_all[rows] * ATTN_SCALE + bias
        m_old = m_ref[rows, :]
        m_new = jnp.maximum(m_old, jnp.max(s, axis=-1, keepdims=True))
        e = jnp.exp(s - jnp.concatenate([m_new] * (tk // LANE), axis=1))
        alpha = jnp.exp(m_old - m_new)
        l_ref[rows, :] = alpha * l_ref[rows, :] + jnp.sum(e, axis=-1, keepdims=True)
        acc_ref[rows, :] = alpha * acc_ref[rows, :]
        m_ref[rows, :] = m_new
        e_ref[rows, :] = e.astype(e_ref.dtype)
    acc_ref[...] += _dot(e_ref[...].astype(BF16), v)


def _block_mask(selb, k0, tk):
    nbp = selb.shape[1]
    blk = (k0 + lax.broadcasted_iota(I32, (nbp, tk), 1)) >> SEL_SHIFT
    expand = (lax.broadcasted_iota(I32, (nbp, tk), 0) == blk).astype(BF16)
    return _dot(selb, expand) > 0.5


def _slc_prompt_body(q_ref, k_ref, v_ref, sel_ref, o_ref, kb_ref, vb_ref, m_ref, l_ref, acc_ref, e_ref,
                     *, tq, tk, hpg):
    t = pl.program_id(2)

    @pl.when(t == 0)
    def _():
        kb_ref[...] = k_ref[...].astype(BF16)
        vb_ref[...] = v_ref[...].astype(BF16)

    q6 = _stack_heads(q_ref, 0, hpg)
    selb = sel_ref[...].astype(BF16)
    qpos = t * tq + lax.broadcasted_iota(I32, (tq, 1), 0)
    _flash_init(m_ref, l_ref, acc_ref)

    def body(kt, carry):
        k0 = pl.multiple_of(kt * tk, tk)
        kpos = k0 + lax.broadcasted_iota(I32, (1, tk), 1)
        allowed = _block_mask(selb, k0, tk) & (kpos <= qpos)
        _flash_tile(q6, kb_ref[pl.ds(k0, tk), :], vb_ref[pl.ds(k0, tk), :], jnp.where(allowed, 0.0, NEG),
                    m_ref, l_ref, acc_ref, e_ref, hpg)
        return carry

    lax.fori_loop(0, (t * tq + tq - 1) // tk + 1, body, 0)
    o = acc_ref[...] / l_ref[...]
    for i in range(hpg):
        o_ref[:, i * HEAD_DIM:(i + 1) * HEAD_DIM] = o[i * tq:(i + 1) * tq]


def slc_attend_prompt(q, kv4, sel, tq, tk):
    b, t, qw = q.shape
    gwid = qw // NSA_KV_GROUPS
    hpg = gwid // HEAD_DIM
    nbp = sel.shape[3]
    assert t % tq == 0 and t % tk == 0
    return pl.pallas_call(
        functools.partial(_slc_prompt_body, tq=tq, tk=tk, hpg=hpg),
        grid=(b, NSA_KV_GROUPS, t // tq),
        in_specs=[
            pl.BlockSpec((None, tq, gwid), lambda i, g, j: (i, j, g)),
            pl.BlockSpec((None, t, HEAD_DIM), lambda i, g, j: (i, 0, 2 * NSA_KV_GROUPS + g)),
            pl.BlockSpec((None, t, HEAD_DIM), lambda i, g, j: (i, 0, 3 * NSA_KV_GROUPS + g)),
            pl.BlockSpec((None, None, tq, nbp), lambda i, g, j: (i, g, j, 0)),
        ],
        out_specs=pl.BlockSpec((None, tq, gwid), lambda i, g, j: (i, j, g)),
        out_shape=jax.ShapeDtypeStruct((b, t, qw), F32),
        scratch_shapes=[pltpu.VMEM((t, HEAD_DIM), BF16), pltpu.VMEM((t, HEAD_DIM), BF16),
                        pltpu.VMEM((hpg * tq, LANE), F32), pltpu.VMEM((hpg * tq, LANE), F32),
                        pltpu.VMEM((hpg * tq, HEAD_DIM), F32), pltpu.VMEM((hpg * tq, tk), BF16)],
        compiler_params=_cparams("parallel", "parallel", "arbitrary"),
        name="slc_attend_prompt",
    )(q, kv4, kv4, sel)


def _slc_paged_body(pt_ref, q_ref, sel_ref, new_ref, *refs, npg, tq, hpg, past_len, page):
    del pt_ref
    page_refs = refs[:npg]
    o_ref, m_ref, l_ref, acc_ref, e_ref = refs[npg:]
    s_id = pl.program_id(1)
    gwid = hpg * HEAD_DIM
    qpos = past_len + lax.broadcasted_iota(I32, (tq, 1), 0)
    k_chunk, v_chunk = 2 * NSA_KV_GROUPS, 3 * NSA_KV_GROUPS

    @pl.when(s_id == 0)
    def _():
        _flash_init(m_ref, l_ref, acc_ref)

    tk = npg * page
    k0 = s_id * tk
    kpos = k0 + lax.broadcasted_iota(I32, (1, tk), 1)
    for g in range(NSA_KV_GROUPS):
        q6 = _stack_heads(q_ref, g * gwid, hpg)
        selb = sel_ref[g].astype(BF16)
        k = jnp.concatenate([pr[pl.ds(k_chunk + g, page, stride=PAGE_CHUNKS), :] for pr in page_refs],
                            axis=0).astype(BF16)
        v = jnp.concatenate([pr[pl.ds(v_chunk + g, page, stride=PAGE_CHUNKS), :] for pr in page_refs],
                            axis=0).astype(BF16)
        allowed = _block_mask(selb, k0, tk) & (kpos <= qpos)
        _flash_tile(q6, k, v, jnp.where(allowed, 0.0, NEG), m_ref.at[g], l_ref.at[g], acc_ref.at[g], e_ref, hpg)

    @pl.when(s_id == pl.num_programs(1) - 1)
    def _():
        cur = past_len // SEL_BLOCK
        nn = new_ref.shape[0]
        npos = past_len + lax.broadcasted_iota(I32, (1, LANE), 1)
        fill = jnp.zeros((LANE - nn, HEAD_DIM), F32)
        for g in range(NSA_KV_GROUPS):
            q6 = _stack_heads(q_ref, g * gwid, hpg)
            k = jnp.concatenate([new_ref[:, g * HEAD_DIM:(g + 1) * HEAD_DIM], fill], axis=0).astype(BF16)
            v = jnp.concatenate([new_ref[:, (NSA_KV_GROUPS + g) * HEAD_DIM:(NSA_KV_GROUPS + g + 1) * HEAD_DIM],
                                 fill], axis=0).astype(BF16)
            allowed = (sel_ref[g][:, cur:cur + 1] > 0.5) & (npos <= qpos) & (npos < past_len + nn)
            _flash_tile(q6, k, v, jnp.where(allowed, 0.0, NEG), m_ref.at[g], l_ref.at[g], acc_ref.at[g],
                        e_ref.at[:, 0:LANE], hpg)
            o = acc_ref[g] / l_ref[g]
            for i in range(hpg):
                o_ref[:, g * gwid + i * HEAD_DIM:g * gwid + (i + 1) * HEAD_DIM] = o[i * tq:(i + 1) * tq]


def slc_attend_paged(q, sel, kv4_new, cache, li, page_table, npg):
    b, tq, qw = q.shape
    n_pages = page_table.shape[1]
    page = cache.shape[2] // PAGE_CHUNKS
    past_len = n_pages * page
    gwid = qw // NSA_KV_GROUPS
    hpg = gwid // HEAD_DIM
    nbp = sel.shape[3]
    cw = 2 * NSA_KV_GROUPS * HEAD_DIM
    assert n_pages % npg == 0 and past_len % SEL_BLOCK == 0 and tq <= SEL_BLOCK
    grid_spec = pltpu.PrefetchScalarGridSpec(
        num_scalar_prefetch=1,
        grid=(b, n_pages // npg),
        in_specs=[
            pl.BlockSpec((None, tq, qw), lambda i, s, pt: (i, 0, 0)),
            pl.BlockSpec((None, NSA_KV_GROUPS, tq, nbp), lambda i, s, pt: (i, 0, 0, 0)),
            pl.BlockSpec((None, tq, cw), lambda i, s, pt: (i, 0, 1)),
        ] + [_page_spec(li, k, npg, page) for k in range(npg)],
        out_specs=pl.BlockSpec((None, tq, qw), lambda i, s, pt: (i, 0, 0)),
        scratch_shapes=[pltpu.VMEM((NSA_KV_GROUPS, hpg * tq, LANE), F32),
                        pltpu.VMEM((NSA_KV_GROUPS, hpg * tq, LANE), F32),
                        pltpu.VMEM((NSA_KV_GROUPS, hpg * tq, HEAD_DIM), F32),
                        pltpu.VMEM((hpg * tq, npg * page), _prob_dtype(tq))],
    )
    return pl.pallas_call(
        functools.partial(_slc_paged_body, npg=npg, tq=tq, hpg=hpg, past_len=past_len, page=page),
        grid_spec=grid_spec,
        out_shape=jax.ShapeDtypeStruct((b, tq, qw), F32),
        compiler_params=_cparams("parallel", "arbitrary"),
        name="slc_attend_paged",
    )(page_table, q, sel, kv4_new, *([cache] * npg))


def _win_body(q_ref, k_ref, v_ref, o_ref, l_ref, e_ref, *, qb, span, pos0, kpos0, hpg):
    blk = pl.program_id(2)
    q6 = _stack_heads(q_ref, 0, hpg)
    start = pl.multiple_of(blk * qb, qb)
    k = k_ref[pl.ds(start, span), :].astype(BF16)
    v = v_ref[pl.ds(start, span), :].astype(BF16)
    kpos = kpos0 + blk * qb + lax.broadcasted_iota(I32, (1, span), 1)
    qpos = pos0 + blk * qb + lax.broadcasted_iota(I32, (qb, 1), 0)
    dist = qpos - kpos
    allowed = (kpos >= 0) & (dist >= 0) & (dist < WINDOW)
    bias = jnp.where(allowed, 0.0, NEG)
    s_all = _dot_nt(q6, k)
    for i in range(hpg):
        rows = slice(i * qb, (i + 1) * qb)
        s = s_all[rows] * ATTN_SCALE + bias
        e = jnp.exp(s - jnp.max(s, axis=-1, keepdims=True))
        l_ref[rows, :] = jnp.broadcast_to(jnp.sum(e, axis=-1, keepdims=True), (qb, LANE))
        e_ref[rows, :] = e.astype(e_ref.dtype)
    o = _dot(e_ref[...].astype(BF16), v) / l_ref[...]
    for i in range(hpg):
        o_ref[:, i * HEAD_DIM:(i + 1) * HEAD_DIM] = o[i * qb:(i + 1) * qb]


def win_attend(q, kext, qb, span, pos0, kpos0):
    b, t, qw = q.shape
    lk = kext.shape[1]
    gwid = qw // NSA_KV_GROUPS
    hpg = gwid // HEAD_DIM
    assert t % qb == 0 and (t // qb - 1) * qb + span <= lk and qb % SUBLANE == 0 and span % SUBLANE == 0
    return pl.pallas_call(
        functools.partial(_win_body, qb=qb, span=span, pos0=pos0, kpos0=kpos0, hpg=hpg),
        grid=(b, NSA_KV_GROUPS, t // qb),
        in_specs=[
            pl.BlockSpec((None, qb, gwid), lambda i, g, j: (i, j, g)),
            pl.BlockSpec((None, lk, HEAD_DIM), lambda i, g, j: (i, 0, g)),
            pl.BlockSpec((None, lk, HEAD_DIM), lambda i, g, j: (i, 0, NSA_KV_GROUPS + g)),
        ],
        out_specs=pl.BlockSpec((None, qb, gwid), lambda i, g, j: (i, j, g)),
        out_shape=jax.ShapeDtypeStruct((b, t, qw), F32),
        scratch_shapes=[pltpu.VMEM((hpg * qb, LANE), F32), pltpu.VMEM((hpg * qb, span), _prob_dtype(qb))],
        compiler_params=_cparams("parallel", "parallel", "parallel"),
        name="win_attend",
    )(q, kext, kext)


def _combine_body(gt_ref, bias_ref, oc_ref, os_ref, ow_ref, o_ref, *, n_heads):
    gs = jax.nn.sigmoid(gt_ref[...] + bias_ref[...])
    for h in range(n_heads):
        sl = slice(h * HEAD_DIM, (h + 1) * HEAD_DIM)
        o_ref[:, sl] = (gs[:, 3 * h:3 * h + 1] * oc_ref[:, sl] + gs[:, 3 * h + 1:3 * h + 2] * os_ref[:, sl]
                        + gs[:, 3 * h + 2:3 * h + 3] * ow_ref[:, sl])


def combine(gates, bias, o_cmp, o_slc, o_win, tm):
    n, w = o_cmp.shape
    return pl.pallas_call(
        functools.partial(_combine_body, n_heads=w // HEAD_DIM),
        grid=(n // tm,),
        in_specs=[
            pl.BlockSpec((tm, LANE), lambda i: (i, 0)),
            pl.BlockSpec((1, LANE), lambda i: (0, 0)),
            pl.BlockSpec((tm, w), lambda i: (i, 0)),
            pl.BlockSpec((tm, w), lambda i: (i, 0)),
            pl.BlockSpec((tm, w), lambda i: (i, 0)),
        ],
        out_specs=pl.BlockSpec((tm, w), lambda i: (i, 0)),
        out_shape=jax.ShapeDtypeStruct((n, w), F32),
        compiler_params=_cparams("parallel"),
        name="combine",
    )(gates, bias, o_cmp, o_slc, o_win)


def _row_tile(n, pref):
    t = min(n, pref)
    assert n % t == 0
    return t


def kernel(x_prompt, x_sample, mem_prompt, cache_mem_kv, cache_nsa_kv, cache_nsa_win, state_pool, page_table, g_norm_mix, g_norm_mlp, g_norm_mem, w_mem_kv, mem_qk_gain, w_out, w_mlp_up, w_mlp_down, w_in_pool, w_pool_group, pool_scale, w_in_nsa, nsa_gate_bias, nsa_qk_gain, cmp_pos, cmp_w1, cmp_w2):
    depth = g_norm_mix.shape[0]
    bp, tp, d = x_prompt.shape
    db, ts, _ = x_sample.shape
    n_pages = page_table.shape[1]
    page = cache_nsa_kv.shape[2]
    past_len = n_pages * page
    win_buf = cache_nsa_win.shape[2]
    tok_w = w_pool_group.shape[1] * w_pool_group.shape[2]
    n_heads = tok_w // HEAD_DIM
    n_gate = 3 * n_heads
    kvw = 4 * NSA_KV_GROUPS * HEAD_DIM
    winw = 2 * NSA_KV_GROUPS * HEAD_DIM
    assert win_buf == WINDOW and past_len % SEL_BLOCK == 0 and ts <= SUBLANE and tp % QBLOCK == 0

    tsp = SUBLANE
    xs = jnp.pad(x_sample, ((0, 0), (0, tsp - ts), (0, 0))).reshape(db * tsp, d)
    xp = x_prompt.reshape(bp * tp, d)
    n_p, n_s = bp * tp, db * tsp
    tm_p, tm_s = _row_tile(n_p, 256), n_s
    mem2d = mem_prompt.reshape(bp * mem_prompt.shape[1], d)
    mem_len = mem_prompt.shape[1]
    cache_view = cache_nsa_kv.reshape(cache_nsa_kv.shape[0], cache_nsa_kv.shape[1], page * PAGE_CHUNKS, HEAD_DIM)
    npg = min(16, n_pages)

    mem_kv_p, nsa_kv_p, nsa_kv_s, win_p, win_s, pool_p, pool_s = [], [], [], [], [], [], []
    mem_segs = ((2 * MEM_WIDTH, (1,) * MEM_HEADS + (None,) * MEM_HEADS),)
    pool_segs = ((tok_w, None), (MEM_WIDTH, None))
    nsa_segs = ((tok_w, (0,) * n_heads),
                (kvw, (None, None, None, None, 2, 2, None, None)),
                (winw, (3, 3, None, None)),
                (MEM_WIDTH, None),
                (LANE, None))

    for i in range(depth):
        li = i // 2
        wo = w_out[i].astype(BF16)
        wu = w_mlp_up[i].astype(BF16)
        wd = w_mlp_down[i].astype(BF16)
        (mkv,) = in_proj(mem2d, g_norm_mem[i], w_mem_kv[i].astype(BF16), mem_qk_gain[i], mem_segs,
                         _row_tile(mem2d.shape[0], 256))
        mkv_p = mkv.reshape(bp, mem_len, 2 * MEM_WIDTH)
        mem_kv_p.append(mkv_p.reshape(bp, mem_len, 2, MEM_HEADS, HEAD_DIM))
        mkv_s = cache_mem_kv[i].reshape(db, cache_mem_kv.shape[2], 2 * MEM_WIDTH)
        if i % 2 == 0:
            w_in = w_in_pool[li].astype(BF16)
            gains = jnp.ones((1, HEAD_DIM), F32)
            up, qmp = in_proj(xp, g_norm_mix[i], w_in, gains, pool_segs, tm_p)
            us, qms = in_proj(xs, g_norm_mix[i], w_in, gains, pool_segs, tm_s)
            wg = w_pool_group[li].astype(BF16)
            up3 = up.reshape(bp, tp, tok_w)
            tok_p = pool_mix(up3, wg, pool_scale[li], 0, _row_tile(tp, 512)).reshape(n_p, tok_w)
            us3 = us.reshape(db, tsp, tok_w)[:, :ts]
            lead = _round_up(POOL_HIST + ts, SUBLANE) - (POOL_HIST + ts)
            ext = jnp.concatenate([state_pool[li], us3], axis=1)
            ext_pad = jnp.pad(ext, ((0, 0), (lead, 0), (0, 0)))
            l_ext = ext_pad.shape[1]
            tok_e = pool_mix(ext_pad, wg, pool_scale[li], past_len - POOL_HIST - lead, l_ext)
            tok_s = jnp.pad(tok_e[:, l_ext - ts:], ((0, 0), (0, tsp - ts), (0, 0))).reshape(n_s, tok_w)
            pool_p.append(up3[:, tp - POOL_HIST:])
            pool_s.append(ext[:, -POOL_HIST:])
        else:
            w = w_in_nsa[li]
            kv_end = tok_w + kvw + winw
            w_in = jnp.concatenate([w[:, :kv_end], w[:, kv_end + n_gate:], w[:, kv_end:kv_end + n_gate],
                                    jnp.zeros((d, LANE - n_gate), w.dtype)], axis=1).astype(BF16)
            gains = nsa_qk_gain[li]
            bias = jnp.pad(nsa_gate_bias[li], (0, LANE - n_gate)).reshape(1, LANE)
            w1 = cmp_w1[li]
            half = CMP_STRIDE * HEAD_DIM
            w1k = jnp.concatenate([w1[0, :half], w1[0, half:]], axis=1).astype(BF16)
            w1v = jnp.concatenate([w1[1, :half], w1[1, half:]], axis=1).astype(BF16)
            w1b = w1.astype(BF16)
            w2b = cmp_w2[li].astype(BF16)

            q_p, kv4_p, wn_p, qmp, gt_p = in_proj(xp, g_norm_mix[i], w_in, gains, nsa_segs, tm_p)
            q3 = q_p.reshape(bp, tp, tok_w)
            kv43 = kv4_p.reshape(bp, tp, kvw)
            wn3 = wn_p.reshape(bp, tp, winw)
            n_cmp = tp // CMP_STRIDE
            ncp = _round_up(n_cmp, LANE)
            ab = cmp_stage1_dense(kv43, w1k, w1v)
            ab = jnp.pad(ab, ((0, 0), (0, ncp + SUBLANE - ab.shape[1]), (0, 0)))
            ct = cmp_stage2(ab, cmp_pos[li], w1b, w2b, gains[1], ncp)
            oc_p, sel_p = cmp_attend_topk(q3, ct, 0, tp // SEL_BLOCK, _row_tile(tp, 256))
            os_p = slc_attend_prompt(q3, kv43, sel_p, QBLOCK, _row_tile(tp, 512))
            kext = jnp.concatenate([jnp.zeros((bp, WINDOW, winw), F32), wn3], axis=1)
            ow_p = win_attend(q3, kext, QBLOCK, WINDOW + QBLOCK, 0, -WINDOW)
            tok_p = combine(gt_p, bias, oc_p.reshape(n_p, tok_w), os_p.reshape(n_p, tok_w),
                            ow_p.reshape(n_p, tok_w), tm_p)

            q_s, kv4_s, wn_s, qms, gt_s = in_proj(xs, g_norm_mix[i], w_in, gains, nsa_segs, tm_s)
            qs3 = q_s.reshape(db, tsp, tok_w)
            kv4s3 = kv4_s.reshape(db, tsp, kvw)
            wns3 = wn_s.reshape(db, tsp, winw)
            new_rows = jnp.pad(kv4s3[:, :ts], ((0, 0), (0, page - ts), (0, 0)))
            n_rows = _round_up(past_len + ts, SEL_BLOCK)
            n_cmp_s = n_rows // CMP_STRIDE
            ncp_s = _round_up(n_cmp_s, LANE)
            ab_past = cmp_stage1_paged(cache_view, li, page_table, w1k, w1v, npg)
            ab_new = cmp_stage1_dense(new_rows, w1k, w1v)
            ab_s = jnp.concatenate([ab_past, ab_new], axis=1)
            ab_s = jnp.pad(ab_s, ((0, 0), (0, ncp_s + SUBLANE - ab_s.shape[1]), (0, 0)))
            ct_s = cmp_stage2(ab_s, cmp_pos[li], w1b, w2b, gains[1], ncp_s)
            oc_s, sel_s = cmp_attend_topk(qs3, ct_s, past_len, n_rows // SEL_BLOCK, tsp)
            os_s = slc_attend_paged(qs3, sel_s, kv4s3, cache_view, li, page_table, npg)
            wext = jnp.concatenate([cache_nsa_win[li].reshape(db, win_buf, winw), wns3[:, :ts]], axis=1)
            kext_s = jnp.pad(wext, ((0, 0), (0, tsp - ts), (0, 0)))
            ow_s = win_attend(qs3, kext_s, tsp, win_buf + tsp, past_len, past_len - win_buf)
            tok_s = combine(gt_s, bias, oc_s.reshape(n_s, tok_w), os_s.reshape(n_s, tok_w),
                            ow_s.reshape(n_s, tok_w), tm_s)

            nsa_kv_p.append(kv43.reshape(bp, tp, 4, NSA_KV_GROUPS, HEAD_DIM))
            nsa_kv_s.append(kv4s3[:, :ts].reshape(db, ts, 4, NSA_KV_GROUPS, HEAD_DIM))
            wlen = min(WINDOW, tp)
            win_p.append(wn3[:, tp - wlen:].reshape(bp, wlen, 2, NSA_KV_GROUPS, HEAD_DIM))
            win_s.append(wext[:, -win_buf:].reshape(db, win_buf, 2, NSA_KV_GROUPS, HEAD_DIM))

        mem_p = mem_attend(qmp.reshape(bp, tp, MEM_WIDTH), mkv_p, mem_qk_gain[i, 0], _row_tile(tp, 512))
        mem_s = mem_attend(qms.reshape(db, tsp, MEM_WIDTH), mkv_s, mem_qk_gain[i, 0], tsp)
        hp = out_proj(tok_p, mem_p.reshape(n_p, MEM_WIDTH), xp, wo, tm_p)
        hs = out_proj(tok_s, mem_s.reshape(n_s, MEM_WIDTH), xs, wo, tm_s)
        xp = mlp(hp, g_norm_mlp[i], wu, wd, _row_tile(n_p, 512), 512)
        xs = mlp(hs, g_norm_mlp[i], wu, wd, tm_s, 512)

    y_p = xp.reshape(bp, tp, d)
    y_s = xs.reshape(db, tsp, d)[:, :ts]
    return (y_p, y_s, jnp.stack(mem_kv_p), jnp.stack(nsa_kv_p), jnp.stack(nsa_kv_s),
            jnp.stack(win_p), jnp.stack(win_s), jnp.stack(pool_p), jnp.stack(pool_s))
```

```python
import functools

import jax
import jax.numpy as jnp
from jax import lax
from jax.experimental import pallas as pl
from jax.experimental.pallas import tpu as pltpu

F32 = jnp.float32
BF16 = jnp.bfloat16
I32 = jnp.int32

HEAD_DIM = 128
MEM_HEADS = 4
MEM_WIDTH = MEM_HEADS * HEAD_DIM
NSA_KV_GROUPS = 2
CMP_STRIDE = 16
CMP_BLOCK = 2 * CMP_STRIDE
SEL_BLOCK = 64
N_SEL = 16
WINDOW = 512
QBLOCK = 128
POOL_WINDOWS = (2, 4, 8, 16)
POOL_HIST = max(POOL_WINDOWS) - 1
POOL_HALO = 16
ATTN_SCALE = HEAD_DIM ** -0.5
SCALE_LOG2E = ATTN_SCALE * 1.4426950408889634
EPS = 1e-6
NEG = -1e30
BIG = 1e30
LOWEST = -3.0e38
SEL_SHIFT = SEL_BLOCK.bit_length() - 1
RATIO_SHIFT = (SEL_BLOCK // CMP_STRIDE).bit_length() - 1
assert 1 << SEL_SHIFT == SEL_BLOCK and 1 << RATIO_SHIFT == SEL_BLOCK // CMP_STRIDE

LANE = 128
SUBLANE = 8
VMEM_LIMIT_BYTES = 56 * 1024 * 1024


def _cparams(*sem):
    return pltpu.CompilerParams(dimension_semantics=sem, vmem_limit_bytes=VMEM_LIMIT_BYTES)


def _round_up(x, m):
    return (x + m - 1) // m * m


def _rms(x, gain):
    return x * lax.rsqrt(jnp.mean(x * x, axis=-1, keepdims=True) + EPS) * gain


def _dot(a, b):
    return jnp.dot(a, b, preferred_element_type=F32)


def _dot_nt(a, b):
    return lax.dot_general(a, b, (((1,), (1,)), ((), ())), preferred_element_type=F32)


def _in_proj_body(x_ref, g_ref, w_ref, gains_ref, *out_refs, segs):
    xb = _rms(x_ref[...], g_ref[...]).astype(BF16)
    col = 0
    for o_ref, (width, norms) in zip(out_refs, segs):
        y = _dot(xb, w_ref[:, col:col + width])
        if norms is None:
            o_ref[...] = y
        else:
            for c, gi in enumerate(norms):
                yc = y[:, c * LANE:(c + 1) * LANE]
                if gi is not None:
                    yc = _rms(yc, gains_ref[gi:gi + 1, :])
                o_ref[:, c * LANE:(c + 1) * LANE] = yc
        col += width


def in_proj(x2d, g, w, gains, segs, tm):
    n, d = x2d.shape
    wtot = w.shape[1]
    assert n % tm == 0 and wtot == sum(s[0] for s in segs)
    return pl.pallas_call(
        functools.partial(_in_proj_body, segs=segs),
        grid=(n // tm,),
        in_specs=[
            pl.BlockSpec((tm, d), lambda i: (i, 0)),
            pl.BlockSpec((1, d), lambda i: (0, 0)),
            pl.BlockSpec((d, wtot), lambda i: (0, 0)),
            pl.BlockSpec(gains.shape, lambda i: (0, 0)),
        ],
        out_specs=[pl.BlockSpec((tm, s[0]), lambda i: (i, 0)) for s in segs],
        out_shape=[jax.ShapeDtypeStruct((n, s[0]), F32) for s in segs],
        compiler_params=_cparams("parallel"),
        name="in_proj",
    )(x2d, g.reshape(1, d), w, gains)


def _out_proj_body(tok_ref, mem_ref, x_ref, w_ref, o_ref):
    tw = tok_ref.shape[1]
    acc = _dot(tok_ref[...].astype(BF16), w_ref[0:tw, :])
    acc = acc + _dot(mem_ref[...].astype(BF16), w_ref[tw:, :])
    o_ref[...] = x_ref[...] + acc


def out_proj(tok, mem, x, w, tm):
    n, d = x.shape
    tw, mw = tok.shape[1], mem.shape[1]
    return pl.pallas_call(
        _out_proj_body,
        grid=(n // tm,),
        in_specs=[
            pl.BlockSpec((tm, tw), lambda i: (i, 0)),
            pl.BlockSpec((tm, mw), lambda i: (i, 0)),
            pl.BlockSpec((tm, d), lambda i: (i, 0)),
            pl.BlockSpec((tw + mw, d), lambda i: (0, 0)),
        ],
        out_specs=pl.BlockSpec((tm, d), lambda i: (i, 0)),
        out_shape=jax.ShapeDtypeStruct((n, d), F32),
        compiler_params=_cparams("parallel"),
        name="out_proj",
    )(tok, mem, x, w)


def _mlp_body(h_ref, g_ref, wu_ref, wd_ref, o_ref, xn_ref, acc_ref):
    k = pl.program_id(1)

    @pl.when(k == 0)
    def _():
        xn_ref[...] = _rms(h_ref[...], g_ref[...]).astype(BF16)
        acc_ref[...] = jnp.zeros_like(acc_ref)

    a = jnp.maximum(_dot(xn_ref[...], wu_ref[...]), 0.0)
    acc_ref[...] += _dot((a * a).astype(BF16), wd_ref[...])

    @pl.when(k == pl.num_programs(1) - 1)
    def _():
        o_ref[...] = h_ref[...] + acc_ref[...]


def mlp(h, g, w_up, w_down, tm, tf):
    n, d = h.shape
    ff = w_up.shape[1]
    return pl.pallas_call(
        _mlp_body,
        grid=(n // tm, ff // tf),
        in_specs=[
            pl.BlockSpec((tm, d), lambda i, k: (i, 0)),
            pl.BlockSpec((1, d), lambda i, k: (0, 0)),
            pl.BlockSpec((d, tf), lambda i, k: (0, k)),
            pl.BlockSpec((tf, d), lambda i, k: (k, 0)),
        ],
        out_specs=pl.BlockSpec((tm, d), lambda i, k: (i, 0)),
        out_shape=jax.ShapeDtypeStruct((n, d), F32),
        scratch_shapes=[pltpu.VMEM((tm, d), BF16), pltpu.VMEM((tm, d), F32)],
        compiler_params=_cparams("parallel", "arbitrary"),
        name="mlp",
    )(h, g.reshape(1, d), w_up, w_down)


def _mem_attend_body(q_ref, kv_ref, gq_ref, o_ref):
    for h in range(MEM_HEADS):
        sl = slice(h * HEAD_DIM, (h + 1) * HEAD_DIM)
        q = _rms(q_ref[:, sl], gq_ref[...]).astype(BF16)
        k = kv_ref[:, sl].astype(BF16)
        v = kv_ref[:, MEM_WIDTH + h * HEAD_DIM:MEM_WIDTH + (h + 1) * HEAD_DIM].astype(BF16)
        s = _dot_nt(q, k) * ATTN_SCALE
        e = jnp.exp(s - jnp.max(s, axis=-1, keepdims=True))
        o_ref[:, sl] = _dot(e.astype(BF16), v) / jnp.sum(e, axis=-1, keepdims=True)


def mem_attend(qm, mkv, gq, tq):
    b, t, _ = qm.shape
    m = mkv.shape[1]
    return pl.pallas_call(
        _mem_attend_body,
        grid=(b, t // tq),
        in_specs=[
            pl.BlockSpec((None, tq, MEM_WIDTH), lambda i, j: (i, j, 0)),
            pl.BlockSpec((None, m, 2 * MEM_WIDTH), lambda i, j: (i, 0, 0)),
            pl.BlockSpec((1, HEAD_DIM), lambda i, j: (0, 0)),
        ],
        out_specs=pl.BlockSpec((None, tq, MEM_WIDTH), lambda i, j: (i, j, 0)),
        out_shape=jax.ShapeDtypeStruct((b, t, MEM_WIDTH), F32),
        compiler_params=_cparams("parallel", "parallel"),
        name="mem_attend",
    )(qm, mkv, gq.reshape(1, HEAD_DIM))


def _pool_body(cur_ref, halo_ref, wg_ref, sc_ref, o_ref, ext_ref, *, pos0, tt, gw):
    t = pl.program_id(1)
    ext_ref[0:POOL_HALO, :] = jnp.where(t == 0, 0.0, halo_ref[...])
    ext_ref[POOL_HALO:POOL_HALO + tt, :] = cur_ref[...]
    row = t * tt + lax.broadcasted_iota(I32, (tt, 1), 0)
    for gi, w in enumerate(POOL_WINDOWS):
        c0, c1 = gi * gw, (gi + 1) * gw
        x = ext_ref[POOL_HALO:POOL_HALO + tt, c0:c1]
        s = x
        for dd in range(1, w):
            s = s + ext_ref[POOL_HALO - dd:POOL_HALO - dd + tt, c0:c1]
        cnt = jnp.minimum(pos0 + row + 1, w).astype(F32)
        pooled = s / cnt - x
        o_ref[:, c0:c1] = _dot(pooled.astype(BF16), wg_ref[gi]) * sc_ref[:, c0:c1]


def pool_mix(u, w_group, scale, pos0, tt):
    b, l, c = u.shape
    ng, gw, _ = w_group.shape
    assert l % tt == 0 and tt % POOL_HALO == 0 or l == tt
    halo_per_tile = tt // POOL_HALO if tt % POOL_HALO == 0 else 0
    return pl.pallas_call(
        functools.partial(_pool_body, pos0=pos0, tt=tt, gw=gw),
        grid=(b, l // tt),
        in_specs=[
            pl.BlockSpec((None, tt, c), lambda i, j: (i, j, 0)),
            pl.BlockSpec((None, POOL_HALO, c), lambda i, j: (i, jnp.maximum(j * halo_per_tile - 1, 0), 0)),
            pl.BlockSpec((ng, gw, gw), lambda i, j: (0, 0, 0)),
            pl.BlockSpec((1, c), lambda i, j: (0, 0)),
        ],
        out_specs=pl.BlockSpec((None, tt, c), lambda i, j: (i, j, 0)),
        out_shape=jax.ShapeDtypeStruct((b, l, c), F32),
        scratch_shapes=[pltpu.VMEM((POOL_HALO + tt, c), F32)],
        compiler_params=_cparams("parallel", "parallel"),
        name="pool_mix",
    )(u, u, w_group, scale.reshape(1, c))


def _cmp1_compute(page_refs, wk_ref, wv_ref, o_ref, stage_ref, rows):
    m = len(page_refs) * rows // CMP_STRIDE
    for slot, w_ref in ((0, wk_ref), (1, wv_ref)):
        xs = []
        for g in range(NSA_KV_GROUPS):
            j = slot * NSA_KV_GROUPS + g
            for p, pr in enumerate(page_refs):
                stage_ref[p * rows:(p + 1) * rows, :] = pr[:, j * HEAD_DIM:(j + 1) * HEAD_DIM]
            pieces = [stage_ref[pl.ds(r, m, stride=CMP_STRIDE), :] for r in range(CMP_STRIDE)]
            xs.append(jnp.concatenate(pieces, axis=1).astype(BF16))
        y = _dot(jnp.concatenate(xs, axis=0), w_ref[...])
        for g in range(NSA_KV_GROUPS):
            j = slot * NSA_KV_GROUPS + g
            o_ref[:, j * 2 * HEAD_DIM:(j + 1) * 2 * HEAD_DIM] = y[g * m:(g + 1) * m]


def _cmp1_dense_body(x_ref, wk_ref, wv_ref, o_ref, stage_ref, *, rows):
    _cmp1_compute([x_ref], wk_ref, wv_ref, o_ref, stage_ref, rows)


def cmp_stage1_dense(rows4, w1k, w1v, lane_block=0):
    b, l, _ = rows4.shape
    nh = l // CMP_STRIDE
    cw = 2 * NSA_KV_GROUPS * HEAD_DIM
    return pl.pallas_call(
        functools.partial(_cmp1_dense_body, rows=l),
        grid=(b,),
        in_specs=[
            pl.BlockSpec((None, l, cw), lambda i: (i, 0, lane_block)),
            pl.BlockSpec(w1k.shape, lambda i: (0, 0)),
            pl.BlockSpec(w1v.shape, lambda i: (0, 0)),
        ],
        out_specs=pl.BlockSpec((None, nh, 2 * cw), lambda i: (i, 0, 0)),
        out_shape=jax.ShapeDtypeStruct((b, nh, 2 * cw), F32),
        scratch_shapes=[pltpu.VMEM((l, HEAD_DIM), F32)],
        compiler_params=_cparams("parallel"),
        name="cmp_stage1_dense",
    )(rows4, w1k, w1v)


PAGE_CHUNKS = 4 * NSA_KV_GROUPS


def _cmp1_paged_body(pt_ref, *refs, npg, page):
    del pt_ref
    page_refs = refs[:npg]
    wk_ref, wv_ref, o_ref = refs[npg:npg + 3]
    nh = page // CMP_STRIDE
    m = npg * nh
    for slot, w_ref in ((0, wk_ref), (1, wv_ref)):
        xs = []
        for g in range(NSA_KV_GROUPS):
            j = slot * NSA_KV_GROUPS + g
            per_page = []
            for pr in page_refs:
                pieces = [pr[pl.ds(r * PAGE_CHUNKS + j, nh, stride=CMP_STRIDE * PAGE_CHUNKS), :]
                          for r in range(CMP_STRIDE)]
                per_page.append(jnp.concatenate(pieces, axis=1))
            xs.append(jnp.concatenate(per_page, axis=0).astype(BF16))
        y = _dot(jnp.concatenate(xs, axis=0), w_ref[...])
        for g in range(NSA_KV_GROUPS):
            j = slot * NSA_KV_GROUPS + g
            o_ref[:, j * 2 * HEAD_DIM:(j + 1) * 2 * HEAD_DIM] = y[g * m:(g + 1) * m]


def _page_spec(li, k, npg, page):
    return pl.BlockSpec((None, None, page * PAGE_CHUNKS, HEAD_DIM),
                        lambda i, s, pt: (li, pt[i, s * npg + k], 0, 0))


def cmp_stage1_paged(cache, li, page_table, w1k, w1v, npg):
    b, n_pages = page_table.shape
    page = cache.shape[2] // PAGE_CHUNKS
    nh = page // CMP_STRIDE
    cw = 2 * NSA_KV_GROUPS * HEAD_DIM
    assert n_pages % npg == 0
    grid_spec = pltpu.PrefetchScalarGridSpec(
        num_scalar_prefetch=1,
        grid=(b, n_pages // npg),
        in_specs=[_page_spec(li, k, npg, page) for k in range(npg)] + [
            pl.BlockSpec(w1k.shape, lambda i, s, pt: (0, 0)),
            pl.BlockSpec(w1v.shape, lambda i, s, pt: (0, 0)),
        ],
        out_specs=pl.BlockSpec((None, npg * nh, 2 * cw), lambda i, s, pt: (i, s, 0)),
    )
    return pl.pallas_call(
        functools.partial(_cmp1_paged_body, npg=npg, page=page),
        grid_spec=grid_spec,
        out_shape=jax.ShapeDtypeStruct((b, n_pages * nh, 2 * cw), F32),
        compiler_params=_cparams("parallel", "arbitrary"),
        name="cmp_stage1_paged",
    )(page_table, *([cache] * npg), w1k, w1v)


def _cmp2_body(ab_ref, pos_ref, w1_ref, w2_ref, gk_ref, o_ref, *, ncp):
    for slot in range(2):
        posflat = jnp.concatenate([pos_ref[slot, r:r + 1, :] for r in range(CMP_BLOCK)], axis=1)
        posb = jnp.broadcast_to(posflat, (SUBLANE, CMP_BLOCK * HEAD_DIM)).astype(BF16)
        posc = _dot(posb, w1_ref[slot])[0:1]
        for g in range(NSA_KV_GROUPS):
            j = slot * NSA_KV_GROUPS + g
            a = ab_ref[0:ncp, j * 2 * HEAD_DIM:j * 2 * HEAD_DIM + HEAD_DIM]
            bm = ab_ref[1:ncp + 1, j * 2 * HEAD_DIM + HEAD_DIM:(j + 1) * 2 * HEAD_DIM]
            hid = jax.nn.gelu(a + bm + posc)
            out = _dot(hid.astype(BF16), w2_ref[slot])
            if slot == 0:
                out = _rms(out, gk_ref[...])
            o_ref[:, j * HEAD_DIM:(j + 1) * HEAD_DIM] = out


def cmp_stage2(ab, pos, w1, w2, gk, ncp):
    b, nrow, w = ab.shape
    assert nrow >= ncp + 1
    cw = 2 * NSA_KV_GROUPS * HEAD_DIM
    return pl.pallas_call(
        functools.partial(_cmp2_body, ncp=ncp),
        grid=(b,),
        in_specs=[
            pl.BlockSpec((None, nrow, w), lambda i: (i, 0, 0)),
            pl.BlockSpec(pos.shape, lambda i: (0, 0, 0)),
            pl.BlockSpec(w1.shape, lambda i: (0, 0, 0)),
            pl.BlockSpec(w2.shape, lambda i: (0, 0, 0)),
            pl.BlockSpec((1, HEAD_DIM), lambda i: (0, 0)),
        ],
        out_specs=pl.BlockSpec((None, ncp, cw), lambda i: (i, 0, 0)),
        out_shape=jax.ShapeDtypeStruct((b, ncp, cw), F32),
        compiler_params=_cparams("parallel"),
        name="cmp_stage2",
    )(ab, pos, w1, w2, gk.reshape(1, HEAD_DIM))


def _cmp_attn_body(q_ref, ct_ref, o_ref, sel_ref, st_ref, *, pos0, tq, ncp, n_blk, n_sel, hpg):
    t = pl.program_id(1)
    ng = NSA_KV_GROUPS
    gwid = hpg * HEAD_DIM
    ratio = SEL_BLOCK // CMP_STRIDE
    nbp = ncp // ratio
    q0 = pos0 + t * tq
    qpos = q0 + lax.broadcasted_iota(I32, (tq, 1), 0)
    cidx = lax.broadcasted_iota(I32, (1, ncp), 1)
    cmask = (cidx * CMP_STRIDE + (CMP_BLOCK - 1)) <= qpos
    bias = jnp.where(cmask, 0.0, NEG)
    ps_groups = []
    for g in range(ng):
        q6 = _stack_heads(q_ref, g * gwid, hpg)
        kc = ct_ref[:, g * HEAD_DIM:(g + 1) * HEAD_DIM].astype(BF16)
        vc = ct_ref[:, (ng + g) * HEAD_DIM:(ng + g + 1) * HEAD_DIM].astype(BF16)
        s_all = _dot_nt(q6, kc)
        imp = jnp.zeros((tq, ncp), F32)
        probs = []
        for i in range(hpg):
            s = s_all[i * tq:(i + 1) * tq] * ATTN_SCALE + bias
            e = jnp.exp(s - jnp.max(s, axis=-1, keepdims=True))
            p = jnp.where(cmask, e / jnp.sum(e, axis=-1, keepdims=True), 0.0)
            imp = imp + p
            probs.append(p)
        o = _dot(jnp.concatenate(probs, axis=0).astype(BF16), vc)
        for i in range(hpg):
            o_ref[:, g * gwid + i * HEAD_DIM:g * gwid + (i + 1) * HEAD_DIM] = o[i * tq:(i + 1) * tq]
        imp2 = imp + jnp.where(cidx == 0, 0.0, pltpu.roll(imp, 1, axis=1))
        t1 = imp2 + pltpu.roll(imp2, ncp - 1, axis=1)
        ps_groups.append(t1 + pltpu.roll(t1, ncp - 2, axis=1))

    rp = _round_up(ng * tq, LANE)
    if rp > ng * tq:
        ps_groups.append(jnp.zeros((rp - ng * tq, ncp), F32))
    ps_t = jnp.concatenate(ps_groups, axis=0).T
    tiles = []
    for h in range(rp // LANE):
        st_ref[...] = ps_t[:, h * LANE:(h + 1) * LANE]
        tiles.append(st_ref[pl.ds(0, nbp, stride=ratio), :])
    ps_blk = jnp.concatenate(tiles, axis=1)
    jb = lax.broadcasted_iota(I32, (nbp, 1), 0)
    lane = lax.broadcasted_iota(I32, (1, rp), 1)
    cur = (q0 + (lane & (tq - 1))) >> SEL_SHIFT
    forced = (jb == 0) | (jb == cur) | (jb == cur - 1)
    score = jnp.where(forced, BIG, jnp.where(jb <= cur, ps_blk, NEG))
    score = jnp.where(jb < n_blk, score, LOWEST)
    jf = jb.astype(F32)

    def pick_one(_, carry):
        score, sel = carry
        m = jnp.max(score, axis=0, keepdims=True)
        first = jnp.min(jnp.where(score == m, jf, float(nbp)), axis=0, keepdims=True)
        pick = jf == first
        sel = jnp.where(pick & (m > 0.5 * NEG), 1.0, sel)
        return jnp.where(pick, LOWEST, score), sel

    _, sel_t = lax.fori_loop(0, n_sel, pick_one, (score, jnp.zeros((nbp, rp), F32)))
    nbpp = sel_ref.shape[2]
    if nbpp > nbp:
        sel_t = jnp.concatenate([sel_t, jnp.zeros((nbpp - nbp, rp), F32)], axis=0)
    sel = sel_t.T
    for g in range(ng):
        sel_ref[g] = sel[g * tq:(g + 1) * tq]


def cmp_attend_topk(q, ct, pos0, n_blk, tq):
    b, t, qw = q.shape
    ncp = ct.shape[1]
    gwid = qw // NSA_KV_GROUPS
    hpg = gwid // HEAD_DIM
    ratio = SEL_BLOCK // CMP_STRIDE
    assert ncp % LANE == 0 and t % tq == 0 and tq & (tq - 1) == 0 and tq % SUBLANE == 0
    nbpp = _round_up(ncp // ratio, LANE)
    n_sel = min(N_SEL, n_blk)
    return pl.pallas_call(
        functools.partial(_cmp_attn_body, pos0=pos0, tq=tq, ncp=ncp, n_blk=n_blk, n_sel=n_sel, hpg=hpg),
        grid=(b, t // tq),
        in_specs=[
            pl.BlockSpec((None, tq, qw), lambda i, j: (i, j, 0)),
            pl.BlockSpec((None, ncp, ct.shape[2]), lambda i, j: (i, 0, 0)),
        ],
        out_specs=[
            pl.BlockSpec((None, tq, qw), lambda i, j: (i, j, 0)),
            pl.BlockSpec((None, NSA_KV_GROUPS, tq, nbpp), lambda i, j: (i, 0, j, 0)),
        ],
        out_shape=[
            jax.ShapeDtypeStruct((b, t, qw), F32),
            jax.ShapeDtypeStruct((b, NSA_KV_GROUPS, t, nbpp), F32),
        ],
        scratch_shapes=[pltpu.VMEM((ncp, LANE), F32)],
        compiler_params=_cparams("parallel", "parallel"),
        name="cmp_attend_topk",
    )(q, ct)


def _stack_heads(q_ref, lane0, hpg):
    return jnp.concatenate([q_ref[:, lane0 + i * HEAD_DIM:lane0 + (i + 1) * HEAD_DIM] for i in range(hpg)],
                           axis=0).astype(BF16)


def _prob_dtype(rows_per_head):
    return BF16 if rows_per_head % (2 * SUBLANE) == 0 else F32


def _flash_init(m_ref, l_ref, acc_ref):
    m_ref[...] = jnp.full_like(m_ref, NEG)
    l_ref[...] = jnp.zeros_like(l_ref)
    acc_ref[...] = jnp.zeros_like(acc_ref)


def _flash_tile(q6, k, v, bias, m_ref, l_ref, acc_ref, e_ref, hpg):
    tq, tk = bias.shape
    assert tk % LANE == 0 and acc_ref.shape[1] == LANE
    s_all = _dot_nt(q6, k)
    for i in range(hpg):
        rows = slice(i * tq, (i + 1) * tq)
        s = s_all[rows] * SCALE_LOG2E + bias
        m_old = m_ref[rows, :]
        m_new = jnp.maximum(m_old, jnp.max(s, axis=-1, keepdims=True))
        e = jnp.exp2(s - jnp.concatenate([m_new] * (tk // LANE), axis=1))
        alpha = jnp.exp2(m_old - m_new)
        l_ref[rows, :] = alpha * l_ref[rows, :] + jnp.sum(e, axis=-1, keepdims=True)
        acc_ref[rows, :] = alpha * acc_ref[rows, :]
        m_ref[rows, :] = m_new
        e_ref[rows, :] = e.astype(e_ref.dtype)
    acc_ref[...] += _dot(e_ref[...].astype(BF16), v)


def _block_mask(selb, k0, tk):
    nbp = selb.shape[1]
    blk = (k0 + lax.broadcasted_iota(I32, (nbp, tk), 1)) >> SEL_SHIFT
    expand = (lax.broadcasted_iota(I32, (nbp, tk), 0) == blk).astype(BF16)
    return _dot(selb, expand) > 0.5


def _slc_prompt_body(q_ref, k_ref, v_ref, sel_ref, o_ref, kb_ref, vb_ref, m_ref, l_ref, acc_ref, e_ref,
                     *, tq, tk, hpg):
    t = pl.program_id(2)

    @pl.when(t == 0)
    def _():
        kb_ref[...] = k_ref[...].astype(BF16)
        vb_ref[...] = v_ref[...].astype(BF16)

    q6 = _stack_heads(q_ref, 0, hpg)
    selb = sel_ref[...].astype(BF16)
    qpos = t * tq + lax.broadcasted_iota(I32, (tq, 1), 0)
    _flash_init(m_ref, l_ref, acc_ref)

    def body(kt, carry):
        k0 = pl.multiple_of(kt * tk, tk)
        kpos = k0 + lax.broadcasted_iota(I32, (1, tk), 1)
        allowed = _block_mask(selb, k0, tk) & (kpos <= qpos)
        _flash_tile(q6, kb_ref[pl.ds(k0, tk), :], vb_ref[pl.ds(k0, tk), :], jnp.where(allowed, 0.0, NEG),
                    m_ref, l_ref, acc_ref, e_ref, hpg)
        return carry

    lax.fori_loop(0, (t * tq + tq - 1) // tk + 1, body, 0)
    o = acc_ref[...] / l_ref[...]
    for i in range(hpg):
        o_ref[:, i * HEAD_DIM:(i + 1) * HEAD_DIM] = o[i * tq:(i + 1) * tq]


def slc_attend_prompt(q, kv4, sel, tq, tk):
    b, t, qw = q.shape
    gwid = qw // NSA_KV_GROUPS
    hpg = gwid // HEAD_DIM
    nbp = sel.shape[3]
    assert t % tq == 0 and t % tk == 0
    return pl.pallas_call(
        functools.partial(_slc_prompt_body, tq=tq, tk=tk, hpg=hpg),
        grid=(b, NSA_KV_GROUPS, t // tq),
        in_specs=[
            pl.BlockSpec((None, tq, gwid), lambda i, g, j: (i, j, g)),
            pl.BlockSpec((None, t, HEAD_DIM), lambda i, g, j: (i, 0, 2 * NSA_KV_GROUPS + g)),
            pl.BlockSpec((None, t, HEAD_DIM), lambda i, g, j: (i, 0, 3 * NSA_KV_GROUPS + g)),
            pl.BlockSpec((None, None, tq, nbp), lambda i, g, j: (i, g, j, 0)),
        ],
        out_specs=pl.BlockSpec((None, tq, gwid), lambda i, g, j: (i, j, g)),
        out_shape=jax.ShapeDtypeStruct((b, t, qw), F32),
        scratch_shapes=[pltpu.VMEM((t, HEAD_DIM), BF16), pltpu.VMEM((t, HEAD_DIM), BF16),
                        pltpu.VMEM((hpg * tq, LANE), F32), pltpu.VMEM((hpg * tq, LANE), F32),
                        pltpu.VMEM((hpg * tq, HEAD_DIM), F32), pltpu.VMEM((hpg * tq, tk), BF16)],
        compiler_params=_cparams("parallel", "parallel", "arbitrary"),
        name="slc_attend_prompt",
    )(q, kv4, kv4, sel)


def _slc_paged_body(pt_ref, q_ref, sel_ref, new_ref, *refs, npg, tq, hpg, past_len, page):
    del pt_ref
    page_refs = refs[:npg]
    o_ref, m_ref, l_ref, acc_ref, e_ref = refs[npg:]
    s_id = pl.program_id(1)
    gwid = hpg * HEAD_DIM
    qpos = past_len + lax.broadcasted_iota(I32, (tq, 1), 0)
    k_chunk, v_chunk = 2 * NSA_KV_GROUPS, 3 * NSA_KV_GROUPS

    @pl.when(s_id == 0)
    def _():
        _flash_init(m_ref, l_ref, acc_ref)

    tk = npg * page
    k0 = s_id * tk
    kpos = k0 + lax.broadcasted_iota(I32, (1, tk), 1)
    for g in range(NSA_KV_GROUPS):
        q6 = _stack_heads(q_ref, g * gwid, hpg)
        selb = sel_ref[g].astype(BF16)
        k = jnp.concatenate([pr[pl.ds(k_chunk + g, page, stride=PAGE_CHUNKS), :] for pr in page_refs],
                            axis=0).astype(BF16)
        v = jnp.concatenate([pr[pl.ds(v_chunk + g, page, stride=PAGE_CHUNKS), :] for pr in page_refs],
                            axis=0).astype(BF16)
        allowed = _block_mask(selb, k0, tk) & (kpos <= qpos)
        _flash_tile(q6, k, v, jnp.where(allowed, 0.0, NEG), m_ref.at[g], l_ref.at[g], acc_ref.at[g], e_ref, hpg)

    @pl.when(s_id == pl.num_programs(1) - 1)
    def _():
        cur = past_len // SEL_BLOCK
        nn = new_ref.shape[0]
        npos = past_len + lax.broadcasted_iota(I32, (1, LANE), 1)
        fill = jnp.zeros((LANE - nn, HEAD_DIM), F32)
        for g in range(NSA_KV_GROUPS):
            q6 = _stack_heads(q_ref, g * gwid, hpg)
            k = jnp.concatenate([new_ref[:, g * HEAD_DIM:(g + 1) * HEAD_DIM], fill], axis=0).astype(BF16)
            v = jnp.concatenate([new_ref[:, (NSA_KV_GROUPS + g) * HEAD_DIM:(NSA_KV_GROUPS + g + 1) * HEAD_DIM],
                                 fill], axis=0).astype(BF16)
            allowed = (sel_ref[g][:, cur:cur + 1] > 0.5) & (npos <= qpos) & (npos < past_len + nn)
            _flash_tile(q6, k, v, jnp.where(allowed, 0.0, NEG), m_ref.at[g], l_ref.at[g], acc_ref.at[g],
                        e_ref.at[:, 0:LANE], hpg)
            o = acc_ref[g] / l_ref[g]
            for i in range(hpg):
                o_ref[:, g * gwid + i * HEAD_DIM:g * gwid + (i + 1) * HEAD_DIM] = o[i * tq:(i + 1) * tq]


def slc_attend_paged(q, sel, kv4_new, cache, li, page_table, npg):
    b, tq, qw = q.shape
    n_pages = page_table.shape[1]
    page = cache.shape[2] // PAGE_CHUNKS
    past_len = n_pages * page
    gwid = qw // NSA_KV_GROUPS
    hpg = gwid // HEAD_DIM
    nbp = sel.shape[3]
    cw = 2 * NSA_KV_GROUPS * HEAD_DIM
    assert n_pages % npg == 0 and past_len % SEL_BLOCK == 0 and tq <= SEL_BLOCK
    grid_spec = pltpu.PrefetchScalarGridSpec(
        num_scalar_prefetch=1,
        grid=(b, n_pages // npg),
        in_specs=[
            pl.BlockSpec((None, tq, qw), lambda i, s, pt: (i, 0, 0)),
            pl.BlockSpec((None, NSA_KV_GROUPS, tq, nbp), lambda i, s, pt: (i, 0, 0, 0)),
            pl.BlockSpec((None, tq, cw), lambda i, s, pt: (i, 0, 1)),
        ] + [_page_spec(li, k, npg, page) for k in range(npg)],
        out_specs=pl.BlockSpec((None, tq, qw), lambda i, s, pt: (i, 0, 0)),
        scratch_shapes=[pltpu.VMEM((NSA_KV_GROUPS, hpg * tq, LANE), F32),
                        pltpu.VMEM((NSA_KV_GROUPS, hpg * tq, LANE), F32),
                        pltpu.VMEM((NSA_KV_GROUPS, hpg * tq, HEAD_DIM), F32),
                        pltpu.VMEM((hpg * tq, npg * page), _prob_dtype(tq))],
    )
    return pl.pallas_call(
        functools.partial(_slc_paged_body, npg=npg, tq=tq, hpg=hpg, past_len=past_len, page=page),
        grid_spec=grid_spec,
        out_shape=jax.ShapeDtypeStruct((b, tq, qw), F32),
        compiler_params=_cparams("parallel", "arbitrary"),
        name="slc_attend_paged",
    )(page_table, q, sel, kv4_new, *([cache] * npg))


def _win_body(q_ref, k_ref, v_ref, o_ref, l_ref, e_ref, *, qb, span, pos0, kpos0, hpg):
    blk = pl.program_id(2)
    q6 = _stack_heads(q_ref, 0, hpg)
    start = pl.multiple_of(blk * qb, qb)
    k = k_ref[pl.ds(start, span), :].astype(BF16)
    v = v_ref[pl.ds(start, span), :].astype(BF16)
    kpos = kpos0 + blk * qb + lax.broadcasted_iota(I32, (1, span), 1)
    qpos = pos0 + blk * qb + lax.broadcasted_iota(I32, (qb, 1), 0)
    dist = qpos - kpos
    allowed = (kpos >= 0) & (dist >= 0) & (dist < WINDOW)
    bias = jnp.where(allowed, 0.0, NEG)
    s_all = _dot_nt(q6, k)
    for i in range(hpg):
        rows = slice(i * qb, (i + 1) * qb)
        s = s_all[rows] * SCALE_LOG2E + bias
        e = jnp.exp2(s - jnp.max(s, axis=-1, keepdims=True))
        l_ref[rows, :] = jnp.broadcast_to(jnp.sum(e, axis=-1, keepdims=True), (qb, LANE))
        e_ref[rows, :] = e.astype(e_ref.dtype)
    o = _dot(e_ref[...].astype(BF16), v) / l_ref[...]
    for i in range(hpg):
        o_ref[:, i * HEAD_DIM:(i + 1) * HEAD_DIM] = o[i * qb:(i + 1) * qb]


def win_attend(q, kext, qb, span, pos0, kpos0):
    b, t, qw = q.shape
    lk = kext.shape[1]
    gwid = qw // NSA_KV_GROUPS
    hpg = gwid // HEAD_DIM
    assert t % qb == 0 and (t // qb - 1) * qb + span <= lk and qb % SUBLANE == 0 and span % SUBLANE == 0
    return pl.pallas_call(
        functools.partial(_win_body, qb=qb, span=span, pos0=pos0, kpos0=kpos0, hpg=hpg),
        grid=(b, NSA_KV_GROUPS, t // qb),
        in_specs=[
            pl.BlockSpec((None, qb, gwid), lambda i, g, j: (i, j, g)),
            pl.BlockSpec((None, lk, HEAD_DIM), lambda i, g, j: (i, 0, g)),
            pl.BlockSpec((None, lk, HEAD_DIM), lambda i, g, j: (i, 0, NSA_KV_GROUPS + g)),
        ],
        out_specs=pl.BlockSpec((None, qb, gwid), lambda i, g, j: (i, j, g)),
        out_shape=jax.ShapeDtypeStruct((b, t, qw), F32),
        scratch_shapes=[pltpu.VMEM((hpg * qb, LANE), F32), pltpu.VMEM((hpg * qb, span), _prob_dtype(qb))],
        compiler_params=_cparams("parallel", "parallel", "parallel"),
        name="win_attend",
    )(q, kext, kext)


def _combine_body(gt_ref, bias_ref, oc_ref, os_ref, ow_ref, o_ref, *, n_heads):
    gs = jax.nn.sigmoid(gt_ref[...] + bias_ref[...])
    for h in range(n_heads):
        sl = slice(h * HEAD_DIM, (h + 1) * HEAD_DIM)
        o_ref[:, sl] = (gs[:, 3 * h:3 * h + 1] * oc_ref[:, sl] + gs[:, 3 * h + 1:3 * h + 2] * os_ref[:, sl]
                        + gs[:, 3 * h + 2:3 * h + 3] * ow_ref[:, sl])


def combine(gates, bias, o_cmp, o_slc, o_win, tm):
    n, w = o_cmp.shape
    return pl.pallas_call(
        functools.partial(_combine_body, n_heads=w // HEAD_DIM),
        grid=(n // tm,),
        in_specs=[
            pl.BlockSpec((tm, LANE), lambda i: (i, 0)),
            pl.BlockSpec((1, LANE), lambda i: (0, 0)),
            pl.BlockSpec((tm, w), lambda i: (i, 0)),
            pl.BlockSpec((tm, w), lambda i: (i, 0)),
            pl.BlockSpec((tm, w), lambda i: (i, 0)),
        ],
        out_specs=pl.BlockSpec((tm, w), lambda i: (i, 0)),
        out_shape=jax.ShapeDtypeStruct((n, w), F32),
        compiler_params=_cparams("parallel"),
        name="combine",
    )(gates, bias, o_cmp, o_slc, o_win)


def _row_tile(n, pref):
    t = min(n, pref)
    assert n % t == 0
    return t


def kernel(x_prompt, x_sample, mem_prompt, cache_mem_kv, cache_nsa_kv, cache_nsa_win, state_pool, page_table, g_norm_mix, g_norm_mlp, g_norm_mem, w_mem_kv, mem_qk_gain, w_out, w_mlp_up, w_mlp_down, w_in_pool, w_pool_group, pool_scale, w_in_nsa, nsa_gate_bias, nsa_qk_gain, cmp_pos, cmp_w1, cmp_w2):
    depth = g_norm_mix.shape[0]
    bp, tp, d = x_prompt.shape
    db, ts, _ = x_sample.shape
    n_pages = page_table.shape[1]
    page = cache_nsa_kv.shape[2]
    past_len = n_pages * page
    win_buf = cache_nsa_win.shape[2]
    tok_w = w_pool_group.shape[1] * w_pool_group.shape[2]
    n_heads = tok_w // HEAD_DIM
    n_gate = 3 * n_heads
    kvw = 4 * NSA_KV_GROUPS * HEAD_DIM
    winw = 2 * NSA_KV_GROUPS * HEAD_DIM
    assert win_buf == WINDOW and past_len % SEL_BLOCK == 0 and ts <= SUBLANE and tp % QBLOCK == 0

    tsp = SUBLANE
    xs = jnp.pad(x_sample, ((0, 0), (0, tsp - ts), (0, 0))).reshape(db * tsp, d)
    xp = x_prompt.reshape(bp * tp, d)
    n_p, n_s = bp * tp, db * tsp
    tm_p, tm_s = _row_tile(n_p, 256), n_s
    mem2d = mem_prompt.reshape(bp * mem_prompt.shape[1], d)
    mem_len = mem_prompt.shape[1]
    cache_view = cache_nsa_kv.reshape(cache_nsa_kv.shape[0], cache_nsa_kv.shape[1], page * PAGE_CHUNKS, HEAD_DIM)
    npg = min(16, n_pages)

    mem_kv_p, nsa_kv_p, nsa_kv_s, win_p, win_s, pool_p, pool_s = [], [], [], [], [], [], []
    mem_segs = ((2 * MEM_WIDTH, (1,) * MEM_HEADS + (None,) * MEM_HEADS),)
    pool_segs = ((tok_w, None), (MEM_WIDTH, None))
    nsa_segs = ((tok_w, (0,) * n_heads),
                (kvw, (None, None, None, None, 2, 2, None, None)),
                (winw, (3, 3, None, None)),
                (MEM_WIDTH, None),
                (LANE, None))

    for i in range(depth):
        li = i // 2
        wo = w_out[i].astype(BF16)
        wu = w_mlp_up[i].astype(BF16)
        wd = w_mlp_down[i].astype(BF16)
        (mkv,) = in_proj(mem2d, g_norm_mem[i], w_mem_kv[i].astype(BF16), mem_qk_gain[i], mem_segs,
                         _row_tile(mem2d.shape[0], 256))
        mkv_p = mkv.reshape(bp, mem_len, 2 * MEM_WIDTH)
        mem_kv_p.append(mkv_p.reshape(bp, mem_len, 2, MEM_HEADS, HEAD_DIM))
        mkv_s = cache_mem_kv[i].reshape(db, cache_mem_kv.shape[2], 2 * MEM_WIDTH)
        if i % 2 == 0:
            w_in = w_in_pool[li].astype(BF16)
            gains = jnp.ones((1, HEAD_DIM), F32)
            up, qmp = in_proj(xp, g_norm_mix[i], w_in, gains, pool_segs, tm_p)
            us, qms = in_proj(xs, g_norm_mix[i], w_in, gains, pool_segs, tm_s)
            wg = w_pool_group[li].astype(BF16)
            up3 = up.reshape(bp, tp, tok_w)
            tok_p = pool_mix(up3, wg, pool_scale[li], 0, _row_tile(tp, 512)).reshape(n_p, tok_w)
            us3 = us.reshape(db, tsp, tok_w)[:, :ts]
            lead = _round_up(POOL_HIST + ts, SUBLANE) - (POOL_HIST + ts)
            ext = jnp.concatenate([state_pool[li], us3], axis=1)
            ext_pad = jnp.pad(ext, ((0, 0), (lead, 0), (0, 0)))
            l_ext = ext_pad.shape[1]
            tok_e = pool_mix(ext_pad, wg, pool_scale[li], past_len - POOL_HIST - lead, l_ext)
            tok_s = jnp.pad(tok_e[:, l_ext - ts:], ((0, 0), (0, tsp - ts), (0, 0))).reshape(n_s, tok_w)
            pool_p.append(up3[:, tp - POOL_HIST:])
            pool_s.append(ext[:, -POOL_HIST:])
        else:
            w = w_in_nsa[li]
            kv_end = tok_w + kvw + winw
            w_in = jnp.concatenate([w[:, :kv_end], w[:, kv_end + n_gate:], w[:, kv_end:kv_end + n_gate],
                                    jnp.zeros((d, LANE - n_gate), w.dtype)], axis=1).astype(BF16)
            gains = nsa_qk_gain[li]
            bias = jnp.pad(nsa_gate_bias[li], (0, LANE - n_gate)).reshape(1, LANE)
            w1 = cmp_w1[li]
            half = CMP_STRIDE * HEAD_DIM
            w1k = jnp.concatenate([w1[0, :half], w1[0, half:]], axis=1).astype(BF16)
            w1v = jnp.concatenate([w1[1, :half], w1[1, half:]], axis=1).astype(BF16)
            w1b = w1.astype(BF16)
            w2b = cmp_w2[li].astype(BF16)

            q_p, kv4_p, wn_p, qmp, gt_p = in_proj(xp, g_norm_mix[i], w_in, gains, nsa_segs, tm_p)
            q3 = q_p.reshape(bp, tp, tok_w)
            kv43 = kv4_p.reshape(bp, tp, kvw)
            wn3 = wn_p.reshape(bp, tp, winw)
            n_cmp = tp // CMP_STRIDE
            ncp = _round_up(n_cmp, LANE)
            ab = cmp_stage1_dense(kv43, w1k, w1v)
            ab = jnp.pad(ab, ((0, 0), (0, ncp + SUBLANE - ab.shape[1]), (0, 0)))
            ct = cmp_stage2(ab, cmp_pos[li], w1b, w2b, gains[1], ncp)
            oc_p, sel_p = cmp_attend_topk(q3, ct, 0, tp // SEL_BLOCK, _row_tile(tp, 256))
            os_p = slc_attend_prompt(q3, kv43, sel_p, QBLOCK, _row_tile(tp, 512))
            kext = jnp.concatenate([jnp.zeros((bp, WINDOW, winw), F32), wn3], axis=1)
            ow_p = win_attend(q3, kext, QBLOCK, WINDOW + QBLOCK, 0, -WINDOW)
            tok_p = combine(gt_p, bias, oc_p.reshape(n_p, tok_w), os_p.reshape(n_p, tok_w),
                            ow_p.reshape(n_p, tok_w), tm_p)

            q_s, kv4_s, wn_s, qms, gt_s = in_proj(xs, g_norm_mix[i], w_in, gains, nsa_segs, tm_s)
            qs3 = q_s.reshape(db, tsp, tok_w)
            kv4s3 = kv4_s.reshape(db, tsp, kvw)
            wns3 = wn_s.reshape(db, tsp, winw)
            new_rows = jnp.pad(kv4s3[:, :ts], ((0, 0), (0, page - ts), (0, 0)))
            n_rows = _round_up(past_len + ts, SEL_BLOCK)
            n_cmp_s = n_rows // CMP_STRIDE
            ncp_s = _round_up(n_cmp_s, LANE)
            ab_past = cmp_stage1_paged(cache_view, li, page_table, w1k, w1v, npg)
            ab_new = cmp_stage1_dense(new_rows, w1k, w1v)
            ab_s = jnp.concatenate([ab_past, ab_new], axis=1)
            ab_s = jnp.pad(ab_s, ((0, 0), (0, ncp_s + SUBLANE - ab_s.shape[1]), (0, 0)))
            ct_s = cmp_stage2(ab_s, cmp_pos[li], w1b, w2b, gains[1], ncp_s)
            oc_s, sel_s = cmp_attend_topk(qs3, ct_s, past_len, n_rows // SEL_BLOCK, tsp)
            os_s = slc_attend_paged(qs3, sel_s, kv4s3, cache_view, li, page_table, npg)
            wext = jnp.concatenate([cache_nsa_win[li].reshape(db, win_buf, winw), wns3[:, :ts]], axis=1)
            kext_s = jnp.pad(wext, ((0, 0), (0, tsp - ts), (0, 0)))
            ow_s = win_attend(qs3, kext_s, tsp, win_buf + tsp, past_len, past_len - win_buf)
            tok_s = combine(gt_s, bias, oc_s.reshape(n_s, tok_w), os_s.reshape(n_s, tok_w),
                            ow_s.reshape(n_s, tok_w), tm_s)

            nsa_kv_p.append(kv43.reshape(bp, tp, 4, NSA_KV_GROUPS, HEAD_DIM))
            nsa_kv_s.append(kv4s3[:, :ts].reshape(db, ts, 4, NSA_KV_GROUPS, HEAD_DIM))
            wlen = min(WINDOW, tp)
            win_p.append(wn3[:, tp - wlen:].reshape(bp, wlen, 2, NSA_KV_GROUPS, HEAD_DIM))
            win_s.append(wext[:, -win_buf:].reshape(db, win_buf, 2, NSA_KV_GROUPS, HEAD_DIM))

        mem_p = mem_attend(qmp.reshape(bp, tp, MEM_WIDTH), mkv_p, mem_qk_gain[i, 0], _row_tile(tp, 512))
        mem_s = mem_attend(qms.reshape(db, tsp, MEM_WIDTH), mkv_s, mem_qk_gain[i, 0], tsp)
        hp = out_proj(tok_p, mem_p.reshape(n_p, MEM_WIDTH), xp, wo, tm_p)
        hs = out_proj(tok_s, mem_s.reshape(n_s, MEM_WIDTH), xs, wo, tm_s)
        xp = mlp(hp, g_norm_mlp[i], wu, wd, _row_tile(n_p, 512), 1024)
        xs = mlp(hs, g_norm_mlp[i], wu, wd, tm_s, 1024)

    y_p = xp.reshape(bp, tp, d)
    y_s = xs.reshape(db, tsp, d)[:, :ts]
    return (y_p, y_s, jnp.stack(mem_kv_p), jnp.stack(nsa_kv_p), jnp.stack(nsa_kv_s),
            jnp.stack(win_p), jnp.stack(win_s), jnp.stack(pool_p), jnp.stack(pool_s))
```

```python
import functools

import jax
import jax.numpy as jnp
from jax import lax
from jax.experimental import pallas as pl
from jax.experimental.pallas import tpu as pltpu

F32 = jnp.float32
BF16 = jnp.bfloat16
I32 = jnp.int32

HEAD_DIM = 128
MEM_HEADS = 4
MEM_WIDTH = MEM_HEADS * HEAD_DIM
NSA_KV_GROUPS = 2
CMP_STRIDE = 16
CMP_BLOCK = 2 * CMP_STRIDE
SEL_BLOCK = 64
N_SEL = 16
WINDOW = 512
QBLOCK = 128
POOL_WINDOWS = (2, 4, 8, 16)
POOL_HIST = max(POOL_WINDOWS) - 1
POOL_HALO = 16
ATTN_SCALE = HEAD_DIM ** -0.5
SCALE_LOG2E = ATTN_SCALE * 1.4426950408889634
EPS = 1e-6
NEG = -1e30
BIG = 1e30
LOWEST = -3.0e38
SEL_SHIFT = SEL_BLOCK.bit_length() - 1
RATIO_SHIFT = (SEL_BLOCK // CMP_STRIDE).bit_length() - 1
assert 1 << SEL_SHIFT == SEL_BLOCK and 1 << RATIO_SHIFT == SEL_BLOCK // CMP_STRIDE

LANE = 128
SUBLANE = 8
VMEM_LIMIT_BYTES = 56 * 1024 * 1024


def _cparams(*sem):
    return pltpu.CompilerParams(dimension_semantics=sem, vmem_limit_bytes=VMEM_LIMIT_BYTES)


def _round_up(x, m):
    return (x + m - 1) // m * m


def _rms(x, gain):
    return x * lax.rsqrt(jnp.mean(x * x, axis=-1, keepdims=True) + EPS) * gain


def _dot(a, b):
    return jnp.dot(a, b, preferred_element_type=F32)


def _dot_nt(a, b):
    return lax.dot_general(a, b, (((1,), (1,)), ((), ())), preferred_element_type=F32)


def _in_proj_body(x_ref, g_ref, w_ref, gains_ref, *out_refs, segs):
    xb = _rms(x_ref[...], g_ref[...]).astype(BF16)
    col = 0
    for o_ref, (width, norms) in zip(out_refs, segs):
        y = _dot(xb, w_ref[:, col:col + width])
        if norms is None:
            o_ref[...] = y
        else:
            for c, gi in enumerate(norms):
                yc = y[:, c * LANE:(c + 1) * LANE]
                if gi is not None:
                    yc = _rms(yc, gains_ref[gi:gi + 1, :])
                o_ref[:, c * LANE:(c + 1) * LANE] = yc
        col += width


def in_proj(x2d, g, w, layer, gains, segs, tm):
    n, d = x2d.shape
    wtot = w.shape[2]
    assert n % tm == 0 and wtot == sum(s[0] for s in segs)
    return pl.pallas_call(
        functools.partial(_in_proj_body, segs=segs),
        grid=(n // tm,),
        in_specs=[
            pl.BlockSpec((tm, d), lambda i: (i, 0)),
            pl.BlockSpec((1, d), lambda i: (0, 0)),
            pl.BlockSpec((None, d, wtot), lambda i: (layer, 0, 0)),
            pl.BlockSpec(gains.shape, lambda i: (0, 0)),
        ],
        out_specs=[pl.BlockSpec((tm, s[0]), lambda i: (i, 0)) for s in segs],
        out_shape=[jax.ShapeDtypeStruct((n, s[0]), F32) for s in segs],
        compiler_params=_cparams("parallel"),
        name="in_proj",
    )(x2d, g.reshape(1, d), w, gains)


def _out_proj_body(tok_ref, mem_ref, x_ref, w_ref, o_ref):
    tw = tok_ref.shape[1]
    acc = _dot(tok_ref[...].astype(BF16), w_ref[0:tw, :])
    acc = acc + _dot(mem_ref[...].astype(BF16), w_ref[tw:, :])
    o_ref[...] = x_ref[...] + acc


def _out_proj_gated_body(gt_ref, bias_ref, oc_ref, os_ref, ow_ref, mem_ref, x_ref, w_ref, o_ref):
    tw = oc_ref.shape[1]
    gs = jax.nn.sigmoid(gt_ref[...] + bias_ref[...])
    heads = []
    for h in range(tw // HEAD_DIM):
        sl = slice(h * HEAD_DIM, (h + 1) * HEAD_DIM)
        heads.append(gs[:, 3 * h:3 * h + 1] * oc_ref[:, sl] + gs[:, 3 * h + 1:3 * h + 2] * os_ref[:, sl]
                     + gs[:, 3 * h + 2:3 * h + 3] * ow_ref[:, sl])
    acc = _dot(jnp.concatenate(heads, axis=1).astype(BF16), w_ref[0:tw, :])
    acc = acc + _dot(mem_ref[...].astype(BF16), w_ref[tw:, :])
    o_ref[...] = x_ref[...] + acc


def out_proj(tok, mem, x, w, layer, tm, gates=None):
    n, d = x.shape
    mw = mem.shape[1]
    toks = tok if gates is not None else (tok,)
    tw = toks[0].shape[1]
    row = lambda width: pl.BlockSpec((tm, width), lambda i: (i, 0))
    lead_specs, lead_args = [], []
    if gates is not None:
        lead_specs = [row(LANE), pl.BlockSpec((1, LANE), lambda i: (0, 0))]
        lead_args = list(gates)
    return pl.pallas_call(
        _out_proj_gated_body if gates is not None else _out_proj_body,
        grid=(n // tm,),
        in_specs=lead_specs + [row(tw) for _ in toks] + [
            row(mw), row(d),
            pl.BlockSpec((None, tw + mw, d), lambda i: (layer, 0, 0)),
        ],
        out_specs=row(d),
        out_shape=jax.ShapeDtypeStruct((n, d), F32),
        compiler_params=_cparams("parallel"),
        name="out_proj",
    )(*lead_args, *toks, mem, x, w)


def _mlp_body(h_ref, g_ref, wu_ref, wd_ref, o_ref, xn_ref, acc_ref):
    k = pl.program_id(1)

    @pl.when(k == 0)
    def _():
        xn_ref[...] = _rms(h_ref[...], g_ref[...]).astype(BF16)
        acc_ref[...] = jnp.zeros_like(acc_ref)

    a = jnp.maximum(_dot(xn_ref[...], wu_ref[...]), 0.0)
    acc_ref[...] += _dot((a * a).astype(BF16), wd_ref[...])

    @pl.when(k == pl.num_programs(1) - 1)
    def _():
        o_ref[...] = h_ref[...] + acc_ref[...]


def mlp(h, g, w_up, w_down, layer, tm, tf):
    n, d = h.shape
    ff = w_up.shape[2]
    return pl.pallas_call(
        _mlp_body,
        grid=(n // tm, ff // tf),
        in_specs=[
            pl.BlockSpec((tm, d), lambda i, k: (i, 0)),
            pl.BlockSpec((1, d), lambda i, k: (0, 0)),
            pl.BlockSpec((None, d, tf), lambda i, k: (layer, 0, k)),
            pl.BlockSpec((None, tf, d), lambda i, k: (layer, k, 0)),
        ],
        out_specs=pl.BlockSpec((tm, d), lambda i, k: (i, 0)),
        out_shape=jax.ShapeDtypeStruct((n, d), F32),
        scratch_shapes=[pltpu.VMEM((tm, d), BF16), pltpu.VMEM((tm, d), F32)],
        compiler_params=_cparams("parallel", "arbitrary"),
        name="mlp",
    )(h, g.reshape(1, d), w_up, w_down)


def _mem_attend_body(q_ref, kv_ref, gq_ref, o_ref, *, row_major):
    chunks = 2 * MEM_HEADS
    m = kv_ref.shape[0] // chunks if row_major else kv_ref.shape[0]
    for h in range(MEM_HEADS):
        sl = slice(h * HEAD_DIM, (h + 1) * HEAD_DIM)
        q = _rms(q_ref[:, sl], gq_ref[...]).astype(BF16)
        if row_major:
            k = kv_ref[pl.ds(h, m, stride=chunks), :].astype(BF16)
            v = kv_ref[pl.ds(MEM_HEADS + h, m, stride=chunks), :].astype(BF16)
        else:
            k = kv_ref[:, sl].astype(BF16)
            v = kv_ref[:, MEM_WIDTH + h * HEAD_DIM:MEM_WIDTH + (h + 1) * HEAD_DIM].astype(BF16)
        s = _dot_nt(q, k) * ATTN_SCALE
        e = jnp.exp(s - jnp.max(s, axis=-1, keepdims=True))
        o_ref[:, sl] = _dot(e.astype(BF16), v) / jnp.sum(e, axis=-1, keepdims=True)


def mem_attend(qm, mkv, gq, tq, layer=None):
    b, t, _ = qm.shape
    if layer is None:
        kv_spec = pl.BlockSpec((None,) + mkv.shape[1:], lambda i, j: (i, 0, 0))
    else:
        kv_spec = pl.BlockSpec((None, None) + mkv.shape[2:], lambda i, j: (layer, i, 0, 0))
    return pl.pallas_call(
        functools.partial(_mem_attend_body, row_major=layer is not None),
        grid=(b, t // tq),
        in_specs=[
            pl.BlockSpec((None, tq, MEM_WIDTH), lambda i, j: (i, j, 0)),
            kv_spec,
            pl.BlockSpec((1, HEAD_DIM), lambda i, j: (0, 0)),
        ],
        out_specs=pl.BlockSpec((None, tq, MEM_WIDTH), lambda i, j: (i, j, 0)),
        out_shape=jax.ShapeDtypeStruct((b, t, MEM_WIDTH), F32),
        compiler_params=_cparams("parallel", "parallel"),
        name="mem_attend",
    )(qm, mkv, gq.reshape(1, HEAD_DIM))


def _pool_body(cur_ref, halo_ref, wg_ref, sc_ref, o_ref, ext_ref, *, pos0, tt, gw):
    t = pl.program_id(1)
    ext_ref[0:POOL_HALO, :] = jnp.where(t == 0, 0.0, halo_ref[...])
    ext_ref[POOL_HALO:POOL_HALO + tt, :] = cur_ref[...]
    row = t * tt + lax.broadcasted_iota(I32, (tt, 1), 0)
    for gi, w in enumerate(POOL_WINDOWS):
        c0, c1 = gi * gw, (gi + 1) * gw
        x = ext_ref[POOL_HALO:POOL_HALO + tt, c0:c1]
        s = x
        for dd in range(1, w):
            s = s + ext_ref[POOL_HALO - dd:POOL_HALO - dd + tt, c0:c1]
        cnt = jnp.minimum(pos0 + row + 1, w).astype(F32)
        pooled = s / cnt - x
        o_ref[:, c0:c1] = _dot(pooled.astype(BF16), wg_ref[gi]) * sc_ref[:, c0:c1]


def pool_mix(u, w_group, layer, scale, pos0, tt):
    b, l, c = u.shape
    _, ng, gw, _ = w_group.shape
    assert l % tt == 0 and tt % POOL_HALO == 0 or l == tt
    halo_per_tile = tt // POOL_HALO if tt % POOL_HALO == 0 else 0
    return pl.pallas_call(
        functools.partial(_pool_body, pos0=pos0, tt=tt, gw=gw),
        grid=(b, l // tt),
        in_specs=[
            pl.BlockSpec((None, tt, c), lambda i, j: (i, j, 0)),
            pl.BlockSpec((None, POOL_HALO, c), lambda i, j: (i, jnp.maximum(j * halo_per_tile - 1, 0), 0)),
            pl.BlockSpec((None, ng, gw, gw), lambda i, j: (layer, 0, 0, 0)),
            pl.BlockSpec((1, c), lambda i, j: (0, 0)),
        ],
        out_specs=pl.BlockSpec((None, tt, c), lambda i, j: (i, j, 0)),
        out_shape=jax.ShapeDtypeStruct((b, l, c), F32),
        scratch_shapes=[pltpu.VMEM((POOL_HALO + tt, c), F32)],
        compiler_params=_cparams("parallel", "parallel"),
        name="pool_mix",
    )(u, u, w_group, scale.reshape(1, c))


def _cmp1_compute(page_refs, wk_ref, wv_ref, o_ref, stage_ref, rows):
    m = len(page_refs) * rows // CMP_STRIDE
    for slot, w_ref in ((0, wk_ref), (1, wv_ref)):
        xs = []
        for g in range(NSA_KV_GROUPS):
            j = slot * NSA_KV_GROUPS + g
            for p, pr in enumerate(page_refs):
                stage_ref[p * rows:(p + 1) * rows, :] = pr[:, j * HEAD_DIM:(j + 1) * HEAD_DIM]
            pieces = [stage_ref[pl.ds(r, m, stride=CMP_STRIDE), :] for r in range(CMP_STRIDE)]
            xs.append(jnp.concatenate(pieces, axis=1).astype(BF16))
        y = _dot(jnp.concatenate(xs, axis=0), w_ref[...])
        for g in range(NSA_KV_GROUPS):
            j = slot * NSA_KV_GROUPS + g
            o_ref[:, j * 2 * HEAD_DIM:(j + 1) * 2 * HEAD_DIM] = y[g * m:(g + 1) * m]


def _cmp1_dense_body(x_ref, wk_ref, wv_ref, o_ref, stage_ref, *, rows):
    _cmp1_compute([x_ref], wk_ref, wv_ref, o_ref, stage_ref, rows)


def cmp_stage1_dense(rows4, w1k, w1v, lane_block=0):
    b, l, _ = rows4.shape
    nh = l // CMP_STRIDE
    cw = 2 * NSA_KV_GROUPS * HEAD_DIM
    return pl.pallas_call(
        functools.partial(_cmp1_dense_body, rows=l),
        grid=(b,),
        in_specs=[
            pl.BlockSpec((None, l, cw), lambda i: (i, 0, lane_block)),
            pl.BlockSpec(w1k.shape, lambda i: (0, 0)),
            pl.BlockSpec(w1v.shape, lambda i: (0, 0)),
        ],
        out_specs=pl.BlockSpec((None, nh, 2 * cw), lambda i: (i, 0, 0)),
        out_shape=jax.ShapeDtypeStruct((b, nh, 2 * cw), F32),
        scratch_shapes=[pltpu.VMEM((l, HEAD_DIM), F32)],
        compiler_params=_cparams("parallel"),
        name="cmp_stage1_dense",
    )(rows4, w1k, w1v)


PAGE_CHUNKS = 4 * NSA_KV_GROUPS


def _cmp1_paged_body(pt_ref, *refs, npg, page):
    del pt_ref
    page_refs = refs[:npg]
    wk_ref, wv_ref, o_ref = refs[npg:npg + 3]
    nh = page // CMP_STRIDE
    m = npg * nh
    for slot, w_ref in ((0, wk_ref), (1, wv_ref)):
        xs = []
        for g in range(NSA_KV_GROUPS):
            j = slot * NSA_KV_GROUPS + g
            per_page = []
            for pr in page_refs:
                pieces = [pr[pl.ds(r * PAGE_CHUNKS + j, nh, stride=CMP_STRIDE * PAGE_CHUNKS), :]
                          for r in range(CMP_STRIDE)]
                per_page.append(jnp.concatenate(pieces, axis=1))
            xs.append(jnp.concatenate(per_page, axis=0).astype(BF16))
        y = _dot(jnp.concatenate(xs, axis=0), w_ref[...])
        for g in range(NSA_KV_GROUPS):
            j = slot * NSA_KV_GROUPS + g
            o_ref[:, j * 2 * HEAD_DIM:(j + 1) * 2 * HEAD_DIM] = y[g * m:(g + 1) * m]


def _page_spec(li, k, npg, page):
    return pl.BlockSpec((None, None, page * PAGE_CHUNKS, HEAD_DIM),
                        lambda i, s, pt: (li, pt[i, s * npg + k], 0, 0))


def cmp_stage1_paged(cache, li, page_table, w1k, w1v, npg):
    b, n_pages = page_table.shape
    page = cache.shape[2] // PAGE_CHUNKS
    nh = page // CMP_STRIDE
    cw = 2 * NSA_KV_GROUPS * HEAD_DIM
    assert n_pages % npg == 0
    grid_spec = pltpu.PrefetchScalarGridSpec(
        num_scalar_prefetch=1,
        grid=(b, n_pages // npg),
        in_specs=[_page_spec(li, k, npg, page) for k in range(npg)] + [
            pl.BlockSpec(w1k.shape, lambda i, s, pt: (0, 0)),
            pl.BlockSpec(w1v.shape, lambda i, s, pt: (0, 0)),
        ],
        out_specs=pl.BlockSpec((None, npg * nh, 2 * cw), lambda i, s, pt: (i, s, 0)),
    )
    return pl.pallas_call(
        functools.partial(_cmp1_paged_body, npg=npg, page=page),
        grid_spec=grid_spec,
        out_shape=jax.ShapeDtypeStruct((b, n_pages * nh, 2 * cw), F32),
        compiler_params=_cparams("parallel", "arbitrary"),
        name="cmp_stage1_paged",
    )(page_table, *([cache] * npg), w1k, w1v)


def _cmp2_body(*refs, n_ab, ncp):
    ab_refs = refs[:n_ab]
    pos_ref, w1_ref, w2_ref, gk_ref, o_ref = refs[n_ab:]
    n_have = sum(r.shape[0] for r in ab_refs)
    fill = [jnp.zeros((ncp + SUBLANE - n_have, HEAD_DIM), F32)]
    for slot in range(2):
        posflat = jnp.concatenate([pos_ref[slot, r:r + 1, :] for r in range(CMP_BLOCK)], axis=1)
        posb = jnp.broadcast_to(posflat, (SUBLANE, CMP_BLOCK * HEAD_DIM)).astype(BF16)
        posc = _dot(posb, w1_ref[slot])[0:1]
        for g in range(NSA_KV_GROUPS):
            j = slot * NSA_KV_GROUPS + g
            a_lanes = slice(j * 2 * HEAD_DIM, j * 2 * HEAD_DIM + HEAD_DIM)
            b_lanes = slice(j * 2 * HEAD_DIM + HEAD_DIM, (j + 1) * 2 * HEAD_DIM)
            a = jnp.concatenate([r[:, a_lanes] for r in ab_refs] + fill, axis=0)
            bm = jnp.concatenate([r[:, b_lanes] for r in ab_refs] + fill, axis=0)
            bm_next = pltpu.roll(bm, ncp + SUBLANE - 1, axis=0)
            hid = jax.nn.gelu(a[0:ncp] + bm_next[0:ncp] + posc)
            out = _dot(hid.astype(BF16), w2_ref[slot])
            if slot == 0:
                out = _rms(out, gk_ref[...])
            o_ref[:, j * HEAD_DIM:(j + 1) * HEAD_DIM] = out


def cmp_stage2(abs_, pos, w1, w2, gk, ncp):
    b, _, w = abs_[0].shape
    assert all(a.shape[1] % SUBLANE == 0 for a in abs_) and sum(a.shape[1] for a in abs_) <= ncp + SUBLANE
    cw = 2 * NSA_KV_GROUPS * HEAD_DIM
    return pl.pallas_call(
        functools.partial(_cmp2_body, n_ab=len(abs_), ncp=ncp),
        grid=(b,),
        in_specs=[pl.BlockSpec((None, a.shape[1], w), lambda i: (i, 0, 0)) for a in abs_] + [
            pl.BlockSpec(pos.shape, lambda i: (0, 0, 0)),
            pl.BlockSpec(w1.shape, lambda i: (0, 0, 0)),
            pl.BlockSpec(w2.shape, lambda i: (0, 0, 0)),
            pl.BlockSpec((1, HEAD_DIM), lambda i: (0, 0)),
        ],
        out_specs=pl.BlockSpec((None, ncp, cw), lambda i: (i, 0, 0)),
        out_shape=jax.ShapeDtypeStruct((b, ncp, cw), F32),
        compiler_params=_cparams("parallel"),
        name="cmp_stage2",
    )(*abs_, pos, w1, w2, gk.reshape(1, HEAD_DIM))


def _cmp_attn_body(q_ref, ct_ref, o_ref, sel_ref, st_ref, *, pos0, tq, ncp, n_blk, n_sel, hpg):
    t = pl.program_id(1)
    ng = NSA_KV_GROUPS
    gwid = hpg * HEAD_DIM
    ratio = SEL_BLOCK // CMP_STRIDE
    nbp = ncp // ratio
    q0 = pos0 + t * tq
    qpos = q0 + lax.broadcasted_iota(I32, (tq, 1), 0)
    cidx = lax.broadcasted_iota(I32, (1, ncp), 1)
    cmask = (cidx * CMP_STRIDE + (CMP_BLOCK - 1)) <= qpos
    bias = jnp.where(cmask, 0.0, NEG)
    ps_groups = []
    for g in range(ng):
        q6 = _stack_heads(q_ref, g * gwid, hpg)
        kc = ct_ref[:, g * HEAD_DIM:(g + 1) * HEAD_DIM].astype(BF16)
        vc = ct_ref[:, (ng + g) * HEAD_DIM:(ng + g + 1) * HEAD_DIM].astype(BF16)
        s_all = _dot_nt(q6, kc)
        imp = jnp.zeros((tq, ncp), F32)
        probs = []
        for i in range(hpg):
            s = s_all[i * tq:(i + 1) * tq] * ATTN_SCALE + bias
            e = jnp.exp(s - jnp.max(s, axis=-1, keepdims=True))
            p = jnp.where(cmask, e / jnp.sum(e, axis=-1, keepdims=True), 0.0)
            imp = imp + p
            probs.append(p)
        o = _dot(jnp.concatenate(probs, axis=0).astype(BF16), vc)
        for i in range(hpg):
            o_ref[:, g * gwid + i * HEAD_DIM:g * gwid + (i + 1) * HEAD_DIM] = o[i * tq:(i + 1) * tq]
        imp2 = imp + jnp.where(cidx == 0, 0.0, pltpu.roll(imp, 1, axis=1))
        t1 = imp2 + pltpu.roll(imp2, ncp - 1, axis=1)
        ps_groups.append(t1 + pltpu.roll(t1, ncp - 2, axis=1))

    rp = _round_up(ng * tq, LANE)
    if rp > ng * tq:
        ps_groups.append(jnp.zeros((rp - ng * tq, ncp), F32))
    ps_t = jnp.concatenate(ps_groups, axis=0).T
    tiles = []
    for h in range(rp // LANE):
        st_ref[...] = ps_t[:, h * LANE:(h + 1) * LANE]
        tiles.append(st_ref[pl.ds(0, nbp, stride=ratio), :])
    ps_blk = jnp.concatenate(tiles, axis=1)
    jb = lax.broadcasted_iota(I32, (nbp, 1), 0)
    lane = lax.broadcasted_iota(I32, (1, rp), 1)
    cur = (q0 + (lane & (tq - 1))) >> SEL_SHIFT
    forced = (jb == 0) | (jb == cur) | (jb == cur - 1)
    score = jnp.where(forced, BIG, jnp.where(jb <= cur, ps_blk, NEG))
    score = jnp.where(jb < n_blk, score, LOWEST)
    jf = jb.astype(F32)

    def pick_one(_, carry):
        score, sel = carry
        m = jnp.max(score, axis=0, keepdims=True)
        first = jnp.min(jnp.where(score == m, jf, float(nbp)), axis=0, keepdims=True)
        pick = jf == first
        sel = jnp.where(pick & (m > 0.5 * NEG), 1.0, sel)
        return jnp.where(pick, LOWEST, score), sel

    _, sel_t = lax.fori_loop(0, n_sel, pick_one, (score, jnp.zeros((nbp, rp), F32)))
    nbpp = sel_ref.shape[2]
    if nbpp > nbp:
        sel_t = jnp.concatenate([sel_t, jnp.zeros((nbpp - nbp, rp), F32)], axis=0)
    sel = sel_t.T
    for g in range(ng):
        sel_ref[g] = sel[g * tq:(g + 1) * tq]


def cmp_attend_topk(q, ct, pos0, n_blk, tq):
    b, t, qw = q.shape
    ncp = ct.shape[1]
    gwid = qw // NSA_KV_GROUPS
    hpg = gwid // HEAD_DIM
    ratio = SEL_BLOCK // CMP_STRIDE
    assert ncp % LANE == 0 and t % tq == 0 and tq & (tq - 1) == 0 and tq % SUBLANE == 0
    nbpp = _round_up(ncp // ratio, LANE)
    n_sel = min(N_SEL, n_blk)
    return pl.pallas_call(
        functools.partial(_cmp_attn_body, pos0=pos0, tq=tq, ncp=ncp, n_blk=n_blk, n_sel=n_sel, hpg=hpg),
        grid=(b, t // tq),
        in_specs=[
            pl.BlockSpec((None, tq, qw), lambda i, j: (i, j, 0)),
            pl.BlockSpec((None, ncp, ct.shape[2]), lambda i, j: (i, 0, 0)),
        ],
        out_specs=[
            pl.BlockSpec((None, tq, qw), lambda i, j: (i, j, 0)),
            pl.BlockSpec((None, NSA_KV_GROUPS, tq, nbpp), lambda i, j: (i, 0, j, 0)),
        ],
        out_shape=[
            jax.ShapeDtypeStruct((b, t, qw), F32),
            jax.ShapeDtypeStruct((b, NSA_KV_GROUPS, t, nbpp), F32),
        ],
        scratch_shapes=[pltpu.VMEM((ncp, LANE), F32)],
        compiler_params=_cparams("parallel", "parallel"),
        name="cmp_attend_topk",
    )(q, ct)


def _stack_heads(q_ref, lane0, hpg):
    return jnp.concatenate([q_ref[:, lane0 + i * HEAD_DIM:lane0 + (i + 1) * HEAD_DIM] for i in range(hpg)],
                           axis=0).astype(BF16)


def _prob_dtype(rows_per_head):
    return BF16 if rows_per_head % (2 * SUBLANE) == 0 else F32


def _flash_init(m_ref, l_ref, acc_ref):
    m_ref[...] = jnp.full_like(m_ref, NEG)
    l_ref[...] = jnp.zeros_like(l_ref)
    acc_ref[...] = jnp.zeros_like(acc_ref)


def _flash_tile(q6, k, v, bias, m_ref, l_ref, acc_ref, e_ref, hpg):
    tq, tk = bias.shape
    assert tk % LANE == 0 and acc_ref.shape[1] == LANE
    s_all = _dot_nt(q6, k)
    for i in range(hpg):
        rows = slice(i * tq, (i + 1) * tq)
        s = s_all[rows] * SCALE_LOG2E + bias
        m_old = m_ref[rows, :]
        m_new = jnp.maximum(m_old, jnp.max(s, axis=-1, keepdims=True))
        e = jnp.exp2(s - jnp.concatenate([m_new] * (tk // LANE), axis=1))
        alpha = jnp.exp2(m_old - m_new)
        l_ref[rows, :] = alpha * l_ref[rows, :] + jnp.sum(e, axis=-1, keepdims=True)
        acc_ref[rows, :] = alpha * acc_ref[rows, :]
        m_ref[rows, :] = m_new
        e_ref[rows, :] = e.astype(e_ref.dtype)
    acc_ref[...] += _dot(e_ref[...].astype(BF16), v)


def _block_mask(selb, k0, tk):
    nbp = selb.shape[1]
    blk = (k0 + lax.broadcasted_iota(I32, (nbp, tk), 1)) >> SEL_SHIFT
    expand = (lax.broadcasted_iota(I32, (nbp, tk), 0) == blk).astype(BF16)
    return _dot(selb, expand) > 0.5


def _slc_prompt_body(q_ref, k_ref, v_ref, sel_ref, o_ref, kb_ref, vb_ref, m_ref, l_ref, acc_ref, e_ref,
                     *, tq, tk, hpg):
    t = pl.program_id(2)

    @pl.when(t == 0)
    def _():
        kb_ref[...] = k_ref[...].astype(BF16)
        vb_ref[...] = v_ref[...].astype(BF16)

    q6 = _stack_heads(q_ref, 0, hpg)
    selb = sel_ref[...].astype(BF16)
    qpos = t * tq + lax.broadcasted_iota(I32, (tq, 1), 0)
    _flash_init(m_ref, l_ref, acc_ref)

    def body(kt, carry):
        k0 = pl.multiple_of(kt * tk, tk)
        kpos = k0 + lax.broadcasted_iota(I32, (1, tk), 1)
        allowed = _block_mask(selb, k0, tk) & (kpos <= qpos)
        _flash_tile(q6, kb_ref[pl.ds(k0, tk), :], vb_ref[pl.ds(k0, tk), :], jnp.where(allowed, 0.0, NEG),
                    m_ref, l_ref, acc_ref, e_ref, hpg)
        return carry

    lax.fori_loop(0, (t * tq + tq - 1) // tk + 1, body, 0)
    o = acc_ref[...] / l_ref[...]
    for i in range(hpg):
        o_ref[:, i * HEAD_DIM:(i + 1) * HEAD_DIM] = o[i * tq:(i + 1) * tq]


def slc_attend_prompt(q, kv4, sel, tq, tk):
    b, t, qw = q.shape
    gwid = qw // NSA_KV_GROUPS
    hpg = gwid // HEAD_DIM
    nbp = sel.shape[3]
    assert t % tq == 0 and t % tk == 0
    return pl.pallas_call(
        functools.partial(_slc_prompt_body, tq=tq, tk=tk, hpg=hpg),
        grid=(b, NSA_KV_GROUPS, t // tq),
        in_specs=[
            pl.BlockSpec((None, tq, gwid), lambda i, g, j: (i, j, g)),
            pl.BlockSpec((None, t, HEAD_DIM), lambda i, g, j: (i, 0, 2 * NSA_KV_GROUPS + g)),
            pl.BlockSpec((None, t, HEAD_DIM), lambda i, g, j: (i, 0, 3 * NSA_KV_GROUPS + g)),
            pl.BlockSpec((None, None, tq, nbp), lambda i, g, j: (i, g, j, 0)),
        ],
        out_specs=pl.BlockSpec((None, tq, gwid), lambda i, g, j: (i, j, g)),
        out_shape=jax.ShapeDtypeStruct((b, t, qw), F32),
        scratch_shapes=[pltpu.VMEM((t, HEAD_DIM), BF16), pltpu.VMEM((t, HEAD_DIM), BF16),
                        pltpu.VMEM((hpg * tq, LANE), F32), pltpu.VMEM((hpg * tq, LANE), F32),
                        pltpu.VMEM((hpg * tq, HEAD_DIM), F32), pltpu.VMEM((hpg * tq, tk), BF16)],
        compiler_params=_cparams("parallel", "parallel", "arbitrary"),
        name="slc_attend_prompt",
    )(q, kv4, kv4, sel)


def _slc_paged_body(pt_ref, q_ref, sel_ref, new_ref, *refs, npg, tq, hpg, past_len, page):
    del pt_ref
    page_refs = refs[:npg]
    o_ref, m_ref, l_ref, acc_ref, e_ref = refs[npg:]
    s_id = pl.program_id(1)
    gwid = hpg * HEAD_DIM
    qpos = past_len + lax.broadcasted_iota(I32, (tq, 1), 0)
    k_chunk, v_chunk = 2 * NSA_KV_GROUPS, 3 * NSA_KV_GROUPS

    @pl.when(s_id == 0)
    def _():
        _flash_init(m_ref, l_ref, acc_ref)

    tk = npg * page
    k0 = s_id * tk
    kpos = k0 + lax.broadcasted_iota(I32, (1, tk), 1)
    for g in range(NSA_KV_GROUPS):
        q6 = _stack_heads(q_ref, g * gwid, hpg)
        selb = sel_ref[g].astype(BF16)
        k = jnp.concatenate([pr[pl.ds(k_chunk + g, page, stride=PAGE_CHUNKS), :] for pr in page_refs],
                            axis=0).astype(BF16)
        v = jnp.concatenate([pr[pl.ds(v_chunk + g, page, stride=PAGE_CHUNKS), :] for pr in page_refs],
                            axis=0).astype(BF16)
        allowed = _block_mask(selb, k0, tk) & (kpos <= qpos)
        _flash_tile(q6, k, v, jnp.where(allowed, 0.0, NEG), m_ref.at[g], l_ref.at[g], acc_ref.at[g], e_ref, hpg)

    @pl.when(s_id == pl.num_programs(1) - 1)
    def _():
        cur = past_len // SEL_BLOCK
        nn = new_ref.shape[0]
        npos = past_len + lax.broadcasted_iota(I32, (1, LANE), 1)
        fill = jnp.zeros((LANE - nn, HEAD_DIM), F32)
        for g in range(NSA_KV_GROUPS):
            q6 = _stack_heads(q_ref, g * gwid, hpg)
            k = jnp.concatenate([new_ref[:, g * HEAD_DIM:(g + 1) * HEAD_DIM], fill], axis=0).astype(BF16)
            v = jnp.concatenate([new_ref[:, (NSA_KV_GROUPS + g) * HEAD_DIM:(NSA_KV_GROUPS + g + 1) * HEAD_DIM],
                                 fill], axis=0).astype(BF16)
            allowed = (sel_ref[g][:, cur:cur + 1] > 0.5) & (npos <= qpos) & (npos < past_len + nn)
            _flash_tile(q6, k, v, jnp.where(allowed, 0.0, NEG), m_ref.at[g], l_ref.at[g], acc_ref.at[g],
                        e_ref.at[:, 0:LANE], hpg)
            o = acc_ref[g] / l_ref[g]
            for i in range(hpg):
                o_ref[:, g * gwid + i * HEAD_DIM:g * gwid + (i + 1) * HEAD_DIM] = o[i * tq:(i + 1) * tq]


def slc_attend_paged(q, sel, kv4_new, cache, li, page_table, npg):
    b, tq, qw = q.shape
    n_pages = page_table.shape[1]
    page = cache.shape[2] // PAGE_CHUNKS
    past_len = n_pages * page
    gwid = qw // NSA_KV_GROUPS
    hpg = gwid // HEAD_DIM
    nbp = sel.shape[3]
    cw = 2 * NSA_KV_GROUPS * HEAD_DIM
    assert n_pages % npg == 0 and past_len % SEL_BLOCK == 0 and tq <= SEL_BLOCK
    grid_spec = pltpu.PrefetchScalarGridSpec(
        num_scalar_prefetch=1,
        grid=(b, n_pages // npg),
        in_specs=[
            pl.BlockSpec((None, tq, qw), lambda i, s, pt: (i, 0, 0)),
            pl.BlockSpec((None, NSA_KV_GROUPS, tq, nbp), lambda i, s, pt: (i, 0, 0, 0)),
            pl.BlockSpec((None, tq, cw), lambda i, s, pt: (i, 0, 1)),
        ] + [_page_spec(li, k, npg, page) for k in range(npg)],
        out_specs=pl.BlockSpec((None, tq, qw), lambda i, s, pt: (i, 0, 0)),
        scratch_shapes=[pltpu.VMEM((NSA_KV_GROUPS, hpg * tq, LANE), F32),
                        pltpu.VMEM((NSA_KV_GROUPS, hpg * tq, LANE), F32),
                        pltpu.VMEM((NSA_KV_GROUPS, hpg * tq, HEAD_DIM), F32),
                        pltpu.VMEM((hpg * tq, npg * page), _prob_dtype(tq))],
    )
    return pl.pallas_call(
        functools.partial(_slc_paged_body, npg=npg, tq=tq, hpg=hpg, past_len=past_len, page=page),
        grid_spec=grid_spec,
        out_shape=jax.ShapeDtypeStruct((b, tq, qw), F32),
        compiler_params=_cparams("parallel", "arbitrary"),
        name="slc_attend_paged",
    )(page_table, q, sel, kv4_new, *([cache] * npg))


def _win_body(q_ref, k_ref, v_ref, o_ref, l_ref, e_ref, *, qb, span, pos0, kpos0, hpg):
    blk = pl.program_id(2)
    q6 = _stack_heads(q_ref, 0, hpg)
    start = pl.multiple_of(blk * qb, qb)
    k = k_ref[pl.ds(start, span), :].astype(BF16)
    v = v_ref[pl.ds(start, span), :].astype(BF16)
    kpos = kpos0 + blk * qb + lax.broadcasted_iota(I32, (1, span), 1)
    qpos = pos0 + blk * qb + lax.broadcasted_iota(I32, (qb, 1), 0)
    dist = qpos - kpos
    allowed = (kpos >= 0) & (dist >= 0) & (dist < WINDOW)
    bias = jnp.where(allowed, 0.0, NEG)
    s_all = _dot_nt(q6, k)
    for i in range(hpg):
        rows = slice(i * qb, (i + 1) * qb)
        s = s_all[rows] * SCALE_LOG2E + bias
        e = jnp.exp2(s - jnp.max(s, axis=-1, keepdims=True))
        l_ref[rows, :] = jnp.broadcast_to(jnp.sum(e, axis=-1, keepdims=True), (qb, LANE))
        e_ref[rows, :] = e.astype(e_ref.dtype)
    o = _dot(e_ref[...].astype(BF16), v) / l_ref[...]
    for i in range(hpg):
        o_ref[:, i * HEAD_DIM:(i + 1) * HEAD_DIM] = o[i * qb:(i + 1) * qb]


def win_attend(q, kext, qb, span, pos0, kpos0):
    b, t, qw = q.shape
    lk = kext.shape[1]
    gwid = qw // NSA_KV_GROUPS
    hpg = gwid // HEAD_DIM
    assert t % qb == 0 and (t // qb - 1) * qb + span <= lk and qb % SUBLANE == 0 and span % SUBLANE == 0
    return pl.pallas_call(
        functools.partial(_win_body, qb=qb, span=span, pos0=pos0, kpos0=kpos0, hpg=hpg),
        grid=(b, NSA_KV_GROUPS, t // qb),
        in_specs=[
            pl.BlockSpec((None, qb, gwid), lambda i, g, j: (i, j, g)),
            pl.BlockSpec((None, lk, HEAD_DIM), lambda i, g, j: (i, 0, g)),
            pl.BlockSpec((None, lk, HEAD_DIM), lambda i, g, j: (i, 0, NSA_KV_GROUPS + g)),
        ],
        out_specs=pl.BlockSpec((None, qb, gwid), lambda i, g, j: (i, j, g)),
        out_shape=jax.ShapeDtypeStruct((b, t, qw), F32),
        scratch_shapes=[pltpu.VMEM((hpg * qb, LANE), F32), pltpu.VMEM((hpg * qb, span), _prob_dtype(qb))],
        compiler_params=_cparams("parallel", "parallel", "parallel"),
        name="win_attend",
    )(q, kext, kext)


def _row_tile(n, pref):
    t = min(n, pref)
    assert n % t == 0
    return t


def kernel(x_prompt, x_sample, mem_prompt, cache_mem_kv, cache_nsa_kv, cache_nsa_win, state_pool, page_table, g_norm_mix, g_norm_mlp, g_norm_mem, w_mem_kv, mem_qk_gain, w_out, w_mlp_up, w_mlp_down, w_in_pool, w_pool_group, pool_scale, w_in_nsa, nsa_gate_bias, nsa_qk_gain, cmp_pos, cmp_w1, cmp_w2):
    depth = g_norm_mix.shape[0]
    bp, tp, d = x_prompt.shape
    db, ts, _ = x_sample.shape
    n_pages = page_table.shape[1]
    page = cache_nsa_kv.shape[2]
    past_len = n_pages * page
    win_buf = cache_nsa_win.shape[2]
    tok_w = w_pool_group.shape[1] * w_pool_group.shape[2]
    n_heads = tok_w // HEAD_DIM
    n_gate = 3 * n_heads
    kvw = 4 * NSA_KV_GROUPS * HEAD_DIM
    winw = 2 * NSA_KV_GROUPS * HEAD_DIM
    assert win_buf == WINDOW and past_len % SEL_BLOCK == 0 and ts <= SUBLANE and tp % QBLOCK == 0

    tsp = SUBLANE
    xs = jnp.pad(x_sample, ((0, 0), (0, tsp - ts), (0, 0))).reshape(db * tsp, d)
    xp = x_prompt.reshape(bp * tp, d)
    n_p, n_s = bp * tp, db * tsp
    tm_p, tm_s = _row_tile(n_p, 256), n_s
    mem2d = mem_prompt.reshape(bp * mem_prompt.shape[1], d)
    mem_len = mem_prompt.shape[1]
    cache_view = cache_nsa_kv.reshape(cache_nsa_kv.shape[0], cache_nsa_kv.shape[1], page * PAGE_CHUNKS, HEAD_DIM)
    npg = min(16, n_pages)

    mem_kv_p, nsa_kv_p, nsa_kv_s, win_p, win_s, pool_p, pool_s = [], [], [], [], [], [], []
    mem_segs = ((2 * MEM_WIDTH, (1,) * MEM_HEADS + (None,) * MEM_HEADS),)
    pool_segs = ((tok_w, None), (MEM_WIDTH, None))
    nsa_segs = ((tok_w, (0,) * n_heads),
                (kvw, (None, None, None, None, 2, 2, None, None)),
                (winw, (3, 3, None, None)),
                (MEM_WIDTH, None),
                (LANE, None))

    wo_b = w_out.astype(BF16)
    wu_b = w_mlp_up.astype(BF16)
    wd_b = w_mlp_down.astype(BF16)
    wmem_b = w_mem_kv.astype(BF16)
    wpool_b = w_in_pool.astype(BF16)
    wg_b = w_pool_group.astype(BF16)
    kv_end = tok_w + kvw + winw
    wnsa_b = jnp.concatenate(
        [w_in_nsa[:, :, :kv_end], w_in_nsa[:, :, kv_end + n_gate:], w_in_nsa[:, :, kv_end:kv_end + n_gate],
         jnp.zeros(w_in_nsa.shape[:2] + (LANE - n_gate,), w_in_nsa.dtype)], axis=2).astype(BF16)
    mem_cache = cache_mem_kv.reshape(cache_mem_kv.shape[0], db, cache_mem_kv.shape[2] * 2 * MEM_HEADS, HEAD_DIM)

    for i in range(depth):
        li = i // 2
        (mkv,) = in_proj(mem2d, g_norm_mem[i], wmem_b, i, mem_qk_gain[i], mem_segs,
                         _row_tile(mem2d.shape[0], 256))
        mkv_p = mkv.reshape(bp, mem_len, 2 * MEM_WIDTH)
        mem_kv_p.append(mkv_p.reshape(bp, mem_len, 2, MEM_HEADS, HEAD_DIM))
        gated = None
        if i % 2 == 0:
            gains = jnp.ones((1, HEAD_DIM), F32)
            up, qmp = in_proj(xp, g_norm_mix[i], wpool_b, li, gains, pool_segs, tm_p)
            us, qms = in_proj(xs, g_norm_mix[i], wpool_b, li, gains, pool_segs, tm_s)
            up3 = up.reshape(bp, tp, tok_w)
            tok_p = pool_mix(up3, wg_b, li, pool_scale[li], 0, _row_tile(tp, 512)).reshape(n_p, tok_w)
            us3 = us.reshape(db, tsp, tok_w)[:, :ts]
            lead = _round_up(POOL_HIST + ts, SUBLANE) - (POOL_HIST + ts)
            ext = jnp.concatenate([state_pool[li], us3], axis=1)
            ext_pad = jnp.pad(ext, ((0, 0), (lead, 0), (0, 0)))
            l_ext = ext_pad.shape[1]
            tok_e = pool_mix(ext_pad, wg_b, li, pool_scale[li], past_len - POOL_HIST - lead, l_ext)
            tok_s = jnp.pad(tok_e[:, l_ext - ts:], ((0, 0), (0, tsp - ts), (0, 0))).reshape(n_s, tok_w)
            pool_p.append(up3[:, tp - POOL_HIST:])
            pool_s.append(ext[:, -POOL_HIST:])
        else:
            gains = nsa_qk_gain[li]
            bias = jnp.pad(nsa_gate_bias[li], (0, LANE - n_gate)).reshape(1, LANE)
            w1 = cmp_w1[li]
            half = CMP_STRIDE * HEAD_DIM
            w1k = jnp.concatenate([w1[0, :half], w1[0, half:]], axis=1).astype(BF16)
            w1v = jnp.concatenate([w1[1, :half], w1[1, half:]], axis=1).astype(BF16)
            w1b = w1.astype(BF16)
            w2b = cmp_w2[li].astype(BF16)

            q_p, kv4_p, wn_p, qmp, gt_p = in_proj(xp, g_norm_mix[i], wnsa_b, li, gains, nsa_segs, tm_p)
            q3 = q_p.reshape(bp, tp, tok_w)
            kv43 = kv4_p.reshape(bp, tp, kvw)
            wn3 = wn_p.reshape(bp, tp, winw)
            n_cmp = tp // CMP_STRIDE
            ncp = _round_up(n_cmp, LANE)
            ab = cmp_stage1_dense(kv43, w1k, w1v)
            ct = cmp_stage2([ab], cmp_pos[li], w1b, w2b, gains[1], ncp)
            oc_p, sel_p = cmp_attend_topk(q3, ct, 0, tp // SEL_BLOCK, _row_tile(tp, 256))
            os_p = slc_attend_prompt(q3, kv43, sel_p, QBLOCK, _row_tile(tp, 512))
            kext = jnp.concatenate([jnp.zeros((bp, WINDOW, winw), F32), wn3], axis=1)
            ow_p = win_attend(q3, kext, QBLOCK, WINDOW + QBLOCK, 0, -WINDOW)
            tok_p = (oc_p.reshape(n_p, tok_w), os_p.reshape(n_p, tok_w), ow_p.reshape(n_p, tok_w))

            q_s, kv4_s, wn_s, qms, gt_s = in_proj(xs, g_norm_mix[i], wnsa_b, li, gains, nsa_segs, tm_s)
            qs3 = q_s.reshape(db, tsp, tok_w)
            kv4s3 = kv4_s.reshape(db, tsp, kvw)
            wns3 = wn_s.reshape(db, tsp, winw)
            new_rows = jnp.pad(kv4s3[:, :ts], ((0, 0), (0, page - ts), (0, 0)))
            n_rows = _round_up(past_len + ts, SEL_BLOCK)
            n_cmp_s = n_rows // CMP_STRIDE
            ncp_s = _round_up(n_cmp_s, LANE)
            ab_past = cmp_stage1_paged(cache_view, li, page_table, w1k, w1v, npg)
            ab_new = cmp_stage1_dense(new_rows, w1k, w1v)
            ct_s = cmp_stage2([ab_past, ab_new], cmp_pos[li], w1b, w2b, gains[1], ncp_s)
            oc_s, sel_s = cmp_attend_topk(qs3, ct_s, past_len, n_rows // SEL_BLOCK, tsp)
            os_s = slc_attend_paged(qs3, sel_s, kv4s3, cache_view, li, page_table, npg)
            wext = jnp.concatenate([cache_nsa_win[li].reshape(db, win_buf, winw), wns3[:, :ts]], axis=1)
            kext_s = jnp.pad(wext, ((0, 0), (0, tsp - ts), (0, 0)))
            ow_s = win_attend(qs3, kext_s, tsp, win_buf + tsp, past_len, past_len - win_buf)
            tok_s = (oc_s.reshape(n_s, tok_w), os_s.reshape(n_s, tok_w), ow_s.reshape(n_s, tok_w))
            gated = ((gt_p, bias), (gt_s, bias))

            nsa_kv_p.append(kv43.reshape(bp, tp, 4, NSA_KV_GROUPS, HEAD_DIM))
            nsa_kv_s.append(kv4s3[:, :ts].reshape(db, ts, 4, NSA_KV_GROUPS, HEAD_DIM))
            wlen = min(WINDOW, tp)
            win_p.append(wn3[:, tp - wlen:].reshape(bp, wlen, 2, NSA_KV_GROUPS, HEAD_DIM))
            win_s.append(wext[:, -win_buf:].reshape(db, win_buf, 2, NSA_KV_GROUPS, HEAD_DIM))

        mem_p = mem_attend(qmp.reshape(bp, tp, MEM_WIDTH), mkv_p, mem_qk_gain[i, 0], _row_tile(tp, 512))
        mem_s = mem_attend(qms.reshape(db, tsp, MEM_WIDTH), mem_cache, mem_qk_gain[i, 0], tsp, layer=i)
        hp = out_proj(tok_p, mem_p.reshape(n_p, MEM_WIDTH), xp, wo_b, i, tm_p, gates=gated and gated[0])
        hs = out_proj(tok_s, mem_s.reshape(n_s, MEM_WIDTH), xs, wo_b, i, tm_s, gates=gated and gated[1])
        xp = mlp(hp, g_norm_mlp[i], wu_b, wd_b, i, _row_tile(n_p, 512), 1024)
        xs = mlp(hs, g_norm_mlp[i], wu_b, wd_b, i, tm_s, 1024)

    y_p = xp.reshape(bp, tp, d)
    y_s = xs.reshape(db, tsp, d)[:, :ts]
    return (y_p, y_s, jnp.stack(mem_kv_p), jnp.stack(nsa_kv_p), jnp.stack(nsa_kv_s),
            jnp.stack(win_p), jnp.stack(win_s), jnp.stack(pool_p), jnp.stack(pool_s))
```

```python
import functools

import jax
import jax.numpy as jnp
from jax import lax
from jax.experimental import pallas as pl
from jax.experimental.pallas import tpu as pltpu

F32 = jnp.float32
BF16 = jnp.bfloat16
I32 = jnp.int32

HEAD_DIM = 128
MEM_HEADS = 4
MEM_WIDTH = MEM_HEADS * HEAD_DIM
NSA_KV_GROUPS = 2
CMP_STRIDE = 16
CMP_BLOCK = 2 * CMP_STRIDE
SEL_BLOCK = 64
N_SEL = 16
WINDOW = 512
QBLOCK = 128
POOL_WINDOWS = (2, 4, 8, 16)
POOL_HIST = max(POOL_WINDOWS) - 1
POOL_HALO = 16
ATTN_SCALE = HEAD_DIM ** -0.5
SCALE_LOG2E = ATTN_SCALE * 1.4426950408889634
EPS = 1e-6
NEG = -1e30
BIG = 1e30
LOWEST = -3.0e38
SEL_SHIFT = SEL_BLOCK.bit_length() - 1
RATIO_SHIFT = (SEL_BLOCK // CMP_STRIDE).bit_length() - 1
assert 1 << SEL_SHIFT == SEL_BLOCK and 1 << RATIO_SHIFT == SEL_BLOCK // CMP_STRIDE

LANE = 128
SUBLANE = 8
VMEM_LIMIT_BYTES = 56 * 1024 * 1024


def _cparams(*sem):
    return pltpu.CompilerParams(dimension_semantics=sem, vmem_limit_bytes=VMEM_LIMIT_BYTES)


def _round_up(x, m):
    return (x + m - 1) // m * m


def _rms(x, gain):
    return x * lax.rsqrt(jnp.mean(x * x, axis=-1, keepdims=True) + EPS) * gain


def _dot(a, b):
    return jnp.dot(a, b, preferred_element_type=F32)


def _dot_nt(a, b):
    return lax.dot_general(a, b, (((1,), (1,)), ((), ())), preferred_element_type=F32)


def _in_proj_body(x_ref, g_ref, w_ref, gains_ref, *out_refs, segs):
    xb = _rms(x_ref[...], g_ref[...]).astype(BF16)
    col = 0
    for o_ref, (width, norms) in zip(out_refs, segs):
        y = _dot(xb, w_ref[:, col:col + width])
        if norms is None:
            o_ref[...] = y
        else:
            for c, gi in enumerate(norms):
                yc = y[:, c * LANE:(c + 1) * LANE]
                if gi is not None:
                    yc = _rms(yc, gains_ref[gi:gi + 1, :])
                o_ref[:, c * LANE:(c + 1) * LANE] = yc
        col += width


def in_proj(x2d, g, w, layer, gains, segs, tm):
    n, d = x2d.shape
    wtot = w.shape[2]
    assert n % tm == 0 and wtot == sum(s[0] for s in segs)
    return pl.pallas_call(
        functools.partial(_in_proj_body, segs=segs),
        grid=(n // tm,),
        in_specs=[
            pl.BlockSpec((tm, d), lambda i: (i, 0)),
            pl.BlockSpec((1, d), lambda i: (0, 0)),
            pl.BlockSpec((None, d, wtot), lambda i: (layer, 0, 0)),
            pl.BlockSpec(gains.shape, lambda i: (0, 0)),
        ],
        out_specs=[pl.BlockSpec((tm, s[0]), lambda i: (i, 0)) for s in segs],
        out_shape=[jax.ShapeDtypeStruct((n, s[0]), F32) for s in segs],
        compiler_params=_cparams("parallel"),
        name="in_proj",
    )(x2d, g.reshape(1, d), w, gains)


def _mem_attend(qm, kv_ref, gq_ref, row_major):
    chunks = 2 * MEM_HEADS
    m = kv_ref.shape[0] // chunks if row_major else kv_ref.shape[0]
    heads = []
    for h in range(MEM_HEADS):
        sl = slice(h * HEAD_DIM, (h + 1) * HEAD_DIM)
        q = _rms(qm[:, sl], gq_ref[...]).astype(BF16)
        if row_major:
            k = kv_ref[pl.ds(h, m, stride=chunks), :].astype(BF16)
            v = kv_ref[pl.ds(MEM_HEADS + h, m, stride=chunks), :].astype(BF16)
        else:
            k = kv_ref[:, sl].astype(BF16)
            v = kv_ref[:, MEM_WIDTH + h * HEAD_DIM:MEM_WIDTH + (h + 1) * HEAD_DIM].astype(BF16)
        s = _dot_nt(q, k) * ATTN_SCALE
        e = jnp.exp(s - jnp.max(s, axis=-1, keepdims=True))
        heads.append(_dot(e.astype(BF16), v) / jnp.sum(e, axis=-1, keepdims=True))
    return jnp.concatenate(heads, axis=1)


def _out_proj_body(*refs, gated, row_major, rows_per_batch):
    if gated:
        gt_ref, bias_ref, oc_ref, os_ref, ow_ref = refs[:5]
        refs = refs[5:]
        gs = jax.nn.sigmoid(gt_ref[...] + bias_ref[...])
        heads = []
        for h in range(oc_ref.shape[1] // HEAD_DIM):
            sl = slice(h * HEAD_DIM, (h + 1) * HEAD_DIM)
            heads.append(gs[:, 3 * h:3 * h + 1] * oc_ref[:, sl] + gs[:, 3 * h + 1:3 * h + 2] * os_ref[:, sl]
                         + gs[:, 3 * h + 2:3 * h + 3] * ow_ref[:, sl])
        tok = jnp.concatenate(heads, axis=1)
    else:
        tok = refs[0][...]
        refs = refs[1:]
    qm_ref, kv_ref, gq_ref, x_ref, w_ref, o_ref = refs
    tw = tok.shape[1]
    acc = _dot(tok.astype(BF16), w_ref[0:tw, :])
    if len(kv_ref.shape) == 2:
        mem = _mem_attend(qm_ref[...], kv_ref, gq_ref, row_major)
    else:
        mem = jnp.concatenate([_mem_attend(qm_ref[b * rows_per_batch:(b + 1) * rows_per_batch, :], kv_ref.at[b],
                                           gq_ref, row_major) for b in range(kv_ref.shape[0])], axis=0)
    acc = acc + _dot(mem.astype(BF16), w_ref[tw:, :])
    o_ref[...] = x_ref[...] + acc


def out_proj(tok, qm, mkv, gq, x, w, layer, tm, rows_per_batch, gates=None, kv_layer=None):
    n, d = x.shape
    toks = tok if gates is not None else (tok,)
    tw = toks[0].shape[1]
    assert rows_per_batch % tm == 0 or (tm == n and tm % rows_per_batch == 0 and rows_per_batch % SUBLANE == 0)
    tiles = max(rows_per_batch // tm, 1)
    one = rows_per_batch % tm == 0
    row = lambda width: pl.BlockSpec((tm, width), lambda i: (i, 0))
    lead_specs, lead_args = [], []
    if gates is not None:
        lead_specs = [row(LANE), pl.BlockSpec((1, LANE), lambda i: (0, 0))]
        lead_args = list(gates)
    if kv_layer is None:
        kv_spec = (pl.BlockSpec((None,) + mkv.shape[1:], lambda i: (i // tiles, 0, 0)) if one else
                   pl.BlockSpec(mkv.shape, lambda i: (0, 0, 0)))
    else:
        kv_spec = (pl.BlockSpec((None, None) + mkv.shape[2:], lambda i: (kv_layer, i // tiles, 0, 0)) if one else
                   pl.BlockSpec((None,) + mkv.shape[1:], lambda i: (kv_layer, 0, 0, 0)))
    return pl.pallas_call(
        functools.partial(_out_proj_body, gated=gates is not None, row_major=kv_layer is not None,
                          rows_per_batch=rows_per_batch),
        grid=(n // tm,),
        in_specs=lead_specs + [row(tw) for _ in toks] + [
            row(MEM_WIDTH), kv_spec, pl.BlockSpec((1, HEAD_DIM), lambda i: (0, 0)), row(d),
            pl.BlockSpec((None, tw + MEM_WIDTH, d), lambda i: (layer, 0, 0)),
        ],
        out_specs=row(d),
        out_shape=jax.ShapeDtypeStruct((n, d), F32),
        compiler_params=_cparams("parallel"),
        name="out_proj",
    )(*lead_args, *toks, qm, mkv, gq.reshape(1, HEAD_DIM), x, w)


def _mlp_body(h_ref, g_ref, wu_ref, wd_ref, o_ref, xn_ref, acc_ref):
    k = pl.program_id(1)

    @pl.when(k == 0)
    def _():
        xn_ref[...] = _rms(h_ref[...], g_ref[...]).astype(BF16)
        acc_ref[...] = jnp.zeros_like(acc_ref)

    a = jnp.maximum(_dot(xn_ref[...], wu_ref[...]), 0.0)
    acc_ref[...] += _dot((a * a).astype(BF16), wd_ref[...])

    @pl.when(k == pl.num_programs(1) - 1)
    def _():
        o_ref[...] = h_ref[...] + acc_ref[...]


def mlp(h, g, w_up, w_down, layer, tm, tf):
    n, d = h.shape
    ff = w_up.shape[2]
    return pl.pallas_call(
        _mlp_body,
        grid=(n // tm, ff // tf),
        in_specs=[
            pl.BlockSpec((tm, d), lambda i, k: (i, 0)),
            pl.BlockSpec((1, d), lambda i, k: (0, 0)),
            pl.BlockSpec((None, d, tf), lambda i, k: (layer, 0, k)),
            pl.BlockSpec((None, tf, d), lambda i, k: (layer, k, 0)),
        ],
        out_specs=pl.BlockSpec((tm, d), lambda i, k: (i, 0)),
        out_shape=jax.ShapeDtypeStruct((n, d), F32),
        scratch_shapes=[pltpu.VMEM((tm, d), BF16), pltpu.VMEM((tm, d), F32)],
        compiler_params=_cparams("parallel", "arbitrary"),
        name="mlp",
    )(h, g.reshape(1, d), w_up, w_down)


def _pool_body(cur_ref, halo_ref, wg_ref, sc_ref, o_ref, ext_ref, *, pos0, tt, gw):
    t = pl.program_id(1)
    ext_ref[0:POOL_HALO, :] = jnp.where(t == 0, 0.0, halo_ref[...])
    ext_ref[POOL_HALO:POOL_HALO + tt, :] = cur_ref[...]
    row = t * tt + lax.broadcasted_iota(I32, (tt, 1), 0)
    for gi, w in enumerate(POOL_WINDOWS):
        c0, c1 = gi * gw, (gi + 1) * gw
        x = ext_ref[POOL_HALO:POOL_HALO + tt, c0:c1]
        s = x
        for dd in range(1, w):
            s = s + ext_ref[POOL_HALO - dd:POOL_HALO - dd + tt, c0:c1]
        cnt = jnp.minimum(pos0 + row + 1, w).astype(F32)
        pooled = s / cnt - x
        o_ref[:, c0:c1] = _dot(pooled.astype(BF16), wg_ref[gi]) * sc_ref[:, c0:c1]


def pool_mix(u, w_group, layer, scale, pos0, tt):
    b, l, c = u.shape
    _, ng, gw, _ = w_group.shape
    assert l % tt == 0 and tt % POOL_HALO == 0 or l == tt
    halo_per_tile = tt // POOL_HALO if tt % POOL_HALO == 0 else 0
    return pl.pallas_call(
        functools.partial(_pool_body, pos0=pos0, tt=tt, gw=gw),
        grid=(b, l // tt),
        in_specs=[
            pl.BlockSpec((None, tt, c), lambda i, j: (i, j, 0)),
            pl.BlockSpec((None, POOL_HALO, c), lambda i, j: (i, jnp.maximum(j * halo_per_tile - 1, 0), 0)),
            pl.BlockSpec((None, ng, gw, gw), lambda i, j: (layer, 0, 0, 0)),
            pl.BlockSpec((1, c), lambda i, j: (0, 0)),
        ],
        out_specs=pl.BlockSpec((None, tt, c), lambda i, j: (i, j, 0)),
        out_shape=jax.ShapeDtypeStruct((b, l, c), F32),
        scratch_shapes=[pltpu.VMEM((POOL_HALO + tt, c), F32)],
        compiler_params=_cparams("parallel", "parallel"),
        name="pool_mix",
    )(u, u, w_group, scale.reshape(1, c))


def _cmp1_compute(page_refs, wk_ref, wv_ref, o_ref, stage_ref, rows):
    m = len(page_refs) * rows // CMP_STRIDE
    for slot, w_ref in ((0, wk_ref), (1, wv_ref)):
        xs = []
        for g in range(NSA_KV_GROUPS):
            j = slot * NSA_KV_GROUPS + g
            for p, pr in enumerate(page_refs):
                stage_ref[p * rows:(p + 1) * rows, :] = pr[:, j * HEAD_DIM:(j + 1) * HEAD_DIM]
            pieces = [stage_ref[pl.ds(r, m, stride=CMP_STRIDE), :] for r in range(CMP_STRIDE)]
            xs.append(jnp.concatenate(pieces, axis=1).astype(BF16))
        y = _dot(jnp.concatenate(xs, axis=0), w_ref[...])
        for g in range(NSA_KV_GROUPS):
            j = slot * NSA_KV_GROUPS + g
            o_ref[:, j * 2 * HEAD_DIM:(j + 1) * 2 * HEAD_DIM] = y[g * m:(g + 1) * m]


def _cmp1_dense_body(x_ref, wk_ref, wv_ref, o_ref, stage_ref, *, rows):
    _cmp1_compute([x_ref], wk_ref, wv_ref, o_ref, stage_ref, rows)


def cmp_stage1_dense(rows4, w1k, w1v, lane_block=0):
    b, l, _ = rows4.shape
    nh = l // CMP_STRIDE
    cw = 2 * NSA_KV_GROUPS * HEAD_DIM
    return pl.pallas_call(
        functools.partial(_cmp1_dense_body, rows=l),
        grid=(b,),
        in_specs=[
            pl.BlockSpec((None, l, cw), lambda i: (i, 0, lane_block)),
            pl.BlockSpec(w1k.shape, lambda i: (0, 0)),
            pl.BlockSpec(w1v.shape, lambda i: (0, 0)),
        ],
        out_specs=pl.BlockSpec((None, nh, 2 * cw), lambda i: (i, 0, 0)),
        out_shape=jax.ShapeDtypeStruct((b, nh, 2 * cw), F32),
        scratch_shapes=[pltpu.VMEM((l, HEAD_DIM), F32)],
        compiler_params=_cparams("parallel"),
        name="cmp_stage1_dense",
    )(rows4, w1k, w1v)


PAGE_CHUNKS = 4 * NSA_KV_GROUPS


def _cmp1_paged_body(pt_ref, *refs, npg, page):
    del pt_ref
    page_refs = refs[:npg]
    wk_ref, wv_ref, o_ref = refs[npg:npg + 3]
    nh = page // CMP_STRIDE
    m = npg * nh
    for slot, w_ref in ((0, wk_ref), (1, wv_ref)):
        xs = []
        for g in range(NSA_KV_GROUPS):
            j = slot * NSA_KV_GROUPS + g
            per_page = []
            for pr in page_refs:
                pieces = [pr[pl.ds(r * PAGE_CHUNKS + j, nh, stride=CMP_STRIDE * PAGE_CHUNKS), :]
                          for r in range(CMP_STRIDE)]
                per_page.append(jnp.concatenate(pieces, axis=1))
            xs.append(jnp.concatenate(per_page, axis=0).astype(BF16))
        y = _dot(jnp.concatenate(xs, axis=0), w_ref[...])
        for g in range(NSA_KV_GROUPS):
            j = slot * NSA_KV_GROUPS + g
            o_ref[:, j * 2 * HEAD_DIM:(j + 1) * 2 * HEAD_DIM] = y[g * m:(g + 1) * m]


def _page_spec(li, k, npg, page):
    return pl.BlockSpec((None, None, page * PAGE_CHUNKS, HEAD_DIM),
                        lambda i, s, pt: (li, pt[i, s * npg + k], 0, 0))


def cmp_stage1_paged(cache, li, page_table, w1k, w1v, npg):
    b, n_pages = page_table.shape
    page = cache.shape[2] // PAGE_CHUNKS
    nh = page // CMP_STRIDE
    cw = 2 * NSA_KV_GROUPS * HEAD_DIM
    assert n_pages % npg == 0
    grid_spec = pltpu.PrefetchScalarGridSpec(
        num_scalar_prefetch=1,
        grid=(b, n_pages // npg),
        in_specs=[_page_spec(li, k, npg, page) for k in range(npg)] + [
            pl.BlockSpec(w1k.shape, lambda i, s, pt: (0, 0)),
            pl.BlockSpec(w1v.shape, lambda i, s, pt: (0, 0)),
        ],
        out_specs=pl.BlockSpec((None, npg * nh, 2 * cw), lambda i, s, pt: (i, s, 0)),
    )
    return pl.pallas_call(
        functools.partial(_cmp1_paged_body, npg=npg, page=page),
        grid_spec=grid_spec,
        out_shape=jax.ShapeDtypeStruct((b, n_pages * nh, 2 * cw), F32),
        compiler_params=_cparams("parallel", "arbitrary"),
        name="cmp_stage1_paged",
    )(page_table, *([cache] * npg), w1k, w1v)


def _cmp2_body(*refs, n_ab, ncp):
    ab_refs = refs[:n_ab]
    pos_ref, w1_ref, w2_ref, gk_ref, o_ref = refs[n_ab:]
    n_have = sum(r.shape[0] for r in ab_refs)
    fill = [jnp.zeros((ncp + SUBLANE - n_have, HEAD_DIM), F32)]
    for slot in range(2):
        posflat = jnp.concatenate([pos_ref[slot, r:r + 1, :] for r in range(CMP_BLOCK)], axis=1)
        posb = jnp.broadcast_to(posflat, (SUBLANE, CMP_BLOCK * HEAD_DIM)).astype(BF16)
        posc = _dot(posb, w1_ref[slot])[0:1]
        for g in range(NSA_KV_GROUPS):
            j = slot * NSA_KV_GROUPS + g
            a_lanes = slice(j * 2 * HEAD_DIM, j * 2 * HEAD_DIM + HEAD_DIM)
            b_lanes = slice(j * 2 * HEAD_DIM + HEAD_DIM, (j + 1) * 2 * HEAD_DIM)
            a = jnp.concatenate([r[:, a_lanes] for r in ab_refs] + fill, axis=0)
            bm = jnp.concatenate([r[:, b_lanes] for r in ab_refs] + fill, axis=0)
            bm_next = pltpu.roll(bm, ncp + SUBLANE - 1, axis=0)
            hid = jax.nn.gelu(a[0:ncp] + bm_next[0:ncp] + posc)
            out = _dot(hid.astype(BF16), w2_ref[slot])
            if slot == 0:
                out = _rms(out, gk_ref[...])
            o_ref[:, j * HEAD_DIM:(j + 1) * HEAD_DIM] = out


def cmp_stage2(abs_, pos, w1, w2, gk, ncp):
    b, _, w = abs_[0].shape
    assert all(a.shape[1] % SUBLANE == 0 for a in abs_) and sum(a.shape[1] for a in abs_) <= ncp + SUBLANE
    cw = 2 * NSA_KV_GROUPS * HEAD_DIM
    return pl.pallas_call(
        functools.partial(_cmp2_body, n_ab=len(abs_), ncp=ncp),
        grid=(b,),
        in_specs=[pl.BlockSpec((None, a.shape[1], w), lambda i: (i, 0, 0)) for a in abs_] + [
            pl.BlockSpec(pos.shape, lambda i: (0, 0, 0)),
            pl.BlockSpec(w1.shape, lambda i: (0, 0, 0)),
            pl.BlockSpec(w2.shape, lambda i: (0, 0, 0)),
            pl.BlockSpec((1, HEAD_DIM), lambda i: (0, 0)),
        ],
        out_specs=pl.BlockSpec((None, ncp, cw), lambda i: (i, 0, 0)),
        out_shape=jax.ShapeDtypeStruct((b, ncp, cw), F32),
        compiler_params=_cparams("parallel"),
        name="cmp_stage2",
    )(*abs_, pos, w1, w2, gk.reshape(1, HEAD_DIM))


def _cmp_attn_body(q_ref, ct_ref, o_ref, sel_ref, st_ref, *, pos0, tq, ncp, n_blk, n_sel, hpg):
    t = pl.program_id(1)
    ng = NSA_KV_GROUPS
    gwid = hpg * HEAD_DIM
    ratio = SEL_BLOCK // CMP_STRIDE
    nbp = ncp // ratio
    q0 = pos0 + t * tq
    qpos = q0 + lax.broadcasted_iota(I32, (tq, 1), 0)
    cidx = lax.broadcasted_iota(I32, (1, ncp), 1)
    cmask = (cidx * CMP_STRIDE + (CMP_BLOCK - 1)) <= qpos
    bias = jnp.where(cmask, 0.0, NEG)
    ps_groups = []
    for g in range(ng):
        q6 = _stack_heads(q_ref, g * gwid, hpg)
        kc = ct_ref[:, g * HEAD_DIM:(g + 1) * HEAD_DIM].astype(BF16)
        vc = ct_ref[:, (ng + g) * HEAD_DIM:(ng + g + 1) * HEAD_DIM].astype(BF16)
        s_all = _dot_nt(q6, kc)
        imp = jnp.zeros((tq, ncp), F32)
        probs = []
        for i in range(hpg):
            s = s_all[i * tq:(i + 1) * tq] * ATTN_SCALE + bias
            e = jnp.exp(s - jnp.max(s, axis=-1, keepdims=True))
            p = jnp.where(cmask, e / jnp.sum(e, axis=-1, keepdims=True), 0.0)
            imp = imp + p
            probs.append(p)
        o = _dot(jnp.concatenate(probs, axis=0).astype(BF16), vc)
        for i in range(hpg):
            o_ref[:, g * gwid + i * HEAD_DIM:g * gwid + (i + 1) * HEAD_DIM] = o[i * tq:(i + 1) * tq]
        imp2 = imp + jnp.where(cidx == 0, 0.0, pltpu.roll(imp, 1, axis=1))
        t1 = imp2 + pltpu.roll(imp2, ncp - 1, axis=1)
        ps_groups.append(t1 + pltpu.roll(t1, ncp - 2, axis=1))

    rp = _round_up(ng * tq, LANE)
    if rp > ng * tq:
        ps_groups.append(jnp.zeros((rp - ng * tq, ncp), F32))
    ps_t = jnp.concatenate(ps_groups, axis=0).T
    tiles = []
    for h in range(rp // LANE):
        st_ref[...] = ps_t[:, h * LANE:(h + 1) * LANE]
        tiles.append(st_ref[pl.ds(0, nbp, stride=ratio), :])
    ps_blk = jnp.concatenate(tiles, axis=1)
    jb = lax.broadcasted_iota(I32, (nbp, 1), 0)
    lane = lax.broadcasted_iota(I32, (1, rp), 1)
    cur = (q0 + (lane & (tq - 1))) >> SEL_SHIFT
    forced = (jb == 0) | (jb == cur) | (jb == cur - 1)
    score = jnp.where(forced, BIG, jnp.where(jb <= cur, ps_blk, NEG))
    score = jnp.where(jb < n_blk, score, LOWEST)
    jf = jb.astype(F32)

    def pick_one(_, carry):
        score, sel = carry
        m = jnp.max(score, axis=0, keepdims=True)
        first = jnp.min(jnp.where(score == m, jf, float(nbp)), axis=0, keepdims=True)
        pick = jf == first
        sel = jnp.where(pick & (m > 0.5 * NEG), 1.0, sel)
        return jnp.where(pick, LOWEST, score), sel

    _, sel_t = lax.fori_loop(0, n_sel, pick_one, (score, jnp.zeros((nbp, rp), F32)))
    nbpp = sel_ref.shape[2]
    if nbpp > nbp:
        sel_t = jnp.concatenate([sel_t, jnp.zeros((nbpp - nbp, rp), F32)], axis=0)
    sel = sel_t.T
    for g in range(ng):
        sel_ref[g] = sel[g * tq:(g + 1) * tq]


def cmp_attend_topk(q, ct, pos0, n_blk, tq):
    b, t, qw = q.shape
    ncp = ct.shape[1]
    gwid = qw // NSA_KV_GROUPS
    hpg = gwid // HEAD_DIM
    ratio = SEL_BLOCK // CMP_STRIDE
    assert ncp % LANE == 0 and t % tq == 0 and tq & (tq - 1) == 0 and tq % SUBLANE == 0
    nbpp = _round_up(ncp // ratio, LANE)
    n_sel = min(N_SEL, n_blk)
    return pl.pallas_call(
        functools.partial(_cmp_attn_body, pos0=pos0, tq=tq, ncp=ncp, n_blk=n_blk, n_sel=n_sel, hpg=hpg),
        grid=(b, t // tq),
        in_specs=[
            pl.BlockSpec((None, tq, qw), lambda i, j: (i, j, 0)),
            pl.BlockSpec((None, ncp, ct.shape[2]), lambda i, j: (i, 0, 0)),
        ],
        out_specs=[
            pl.BlockSpec((None, tq, qw), lambda i, j: (i, j, 0)),
            pl.BlockSpec((None, NSA_KV_GROUPS, tq, nbpp), lambda i, j: (i, 0, j, 0)),
        ],
        out_shape=[
            jax.ShapeDtypeStruct((b, t, qw), F32),
            jax.ShapeDtypeStruct((b, NSA_KV_GROUPS, t, nbpp), F32),
        ],
        scratch_shapes=[pltpu.VMEM((ncp, LANE), F32)],
        compiler_params=_cparams("parallel", "parallel"),
        name="cmp_attend_topk",
    )(q, ct)


def _stack_heads(q_ref, lane0, hpg):
    return jnp.concatenate([q_ref[:, lane0 + i * HEAD_DIM:lane0 + (i + 1) * HEAD_DIM] for i in range(hpg)],
                           axis=0).astype(BF16)


def _prob_dtype(rows_per_head):
    return BF16 if rows_per_head % (2 * SUBLANE) == 0 else F32


def _flash_init(m_ref, l_ref, acc_ref):
    m_ref[...] = jnp.full_like(m_ref, NEG)
    l_ref[...] = jnp.zeros_like(l_ref)
    acc_ref[...] = jnp.zeros_like(acc_ref)


def _flash_tile(q6, k, v, bias, m_ref, l_ref, acc_ref, e_ref, hpg):
    tq, tk = bias.shape
    assert tk % LANE == 0 and acc_ref.shape[1] == LANE
    s_all = _dot_nt(q6, k)
    for i in range(hpg):
        rows = slice(i * tq, (i + 1) * tq)
        s = s_all[rows] * SCALE_LOG2E + bias
        m_old = m_ref[rows, :]
        m_new = jnp.maximum(m_old, jnp.max(s, axis=-1, keepdims=True))
        e = jnp.exp2(s - jnp.concatenate([m_new] * (tk // LANE), axis=1))
        alpha = jnp.exp2(m_old - m_new)
        l_ref[rows, :] = alpha * l_ref[rows, :] + jnp.sum(e, axis=-1, keepdims=True)
        acc_ref[rows, :] = alpha * acc_ref[rows, :]
        m_ref[rows, :] = m_new
        e_ref[rows, :] = e.astype(e_ref.dtype)
    acc_ref[...] += _dot(e_ref[...].astype(BF16), v)


def _block_mask(selb, k0, tk):
    nbp = selb.shape[1]
    blk = (k0 + lax.broadcasted_iota(I32, (nbp, tk), 1)) >> SEL_SHIFT
    expand = (lax.broadcasted_iota(I32, (nbp, tk), 0) == blk).astype(BF16)
    return _dot(selb, expand) > 0.5


def _slc_prompt_body(q_ref, k_ref, v_ref, sel_ref, o_ref, kb_ref, vb_ref, m_ref, l_ref, acc_ref, e_ref,
                     *, tq, tk, hpg):
    t = pl.program_id(2)

    @pl.when(t == 0)
    def _():
        kb_ref[...] = k_ref[...].astype(BF16)
        vb_ref[...] = v_ref[...].astype(BF16)

    q6 = _stack_heads(q_ref, 0, hpg)
    selb = sel_ref[...].astype(BF16)
    qpos = t * tq + lax.broadcasted_iota(I32, (tq, 1), 0)
    _flash_init(m_ref, l_ref, acc_ref)

    def body(kt, carry):
        k0 = pl.multiple_of(kt * tk, tk)
        kpos = k0 + lax.broadcasted_iota(I32, (1, tk), 1)
        allowed = _block_mask(selb, k0, tk) & (kpos <= qpos)
        _flash_tile(q6, kb_ref[pl.ds(k0, tk), :], vb_ref[pl.ds(k0, tk), :], jnp.where(allowed, 0.0, NEG),
                    m_ref, l_ref, acc_ref, e_ref, hpg)
        return carry

    lax.fori_loop(0, (t * tq + tq - 1) // tk + 1, body, 0)
    o = acc_ref[...] / l_ref[...]
    for i in range(hpg):
        o_ref[:, i * HEAD_DIM:(i + 1) * HEAD_DIM] = o[i * tq:(i + 1) * tq]


def slc_attend_prompt(q, kv4, sel, tq, tk):
    b, t, qw = q.shape
    gwid = qw // NSA_KV_GROUPS
    hpg = gwid // HEAD_DIM
    nbp = sel.shape[3]
    assert t % tq == 0 and t % tk == 0
    return pl.pallas_call(
        functools.partial(_slc_prompt_body, tq=tq, tk=tk, hpg=hpg),
        grid=(b, NSA_KV_GROUPS, t // tq),
        in_specs=[
            pl.BlockSpec((None, tq, gwid), lambda i, g, j: (i, j, g)),
            pl.BlockSpec((None, t, HEAD_DIM), lambda i, g, j: (i, 0, 2 * NSA_KV_GROUPS + g)),
            pl.BlockSpec((None, t, HEAD_DIM), lambda i, g, j: (i, 0, 3 * NSA_KV_GROUPS + g)),
            pl.BlockSpec((None, None, tq, nbp), lambda i, g, j: (i, g, j, 0)),
        ],
        out_specs=pl.BlockSpec((None, tq, gwid), lambda i, g, j: (i, j, g)),
        out_shape=jax.ShapeDtypeStruct((b, t, qw), F32),
        scratch_shapes=[pltpu.VMEM((t, HEAD_DIM), BF16), pltpu.VMEM((t, HEAD_DIM), BF16),
                        pltpu.VMEM((hpg * tq, LANE), F32), pltpu.VMEM((hpg * tq, LANE), F32),
                        pltpu.VMEM((hpg * tq, HEAD_DIM), F32), pltpu.VMEM((hpg * tq, tk), BF16)],
        compiler_params=_cparams("parallel", "parallel", "arbitrary"),
        name="slc_attend_prompt",
    )(q, kv4, kv4, sel)


def _slc_paged_body(pt_ref, q_ref, sel_ref, new_ref, *refs, npg, tq, hpg, past_len, page):
    del pt_ref
    page_refs = refs[:npg]
    o_ref, m_ref, l_ref, acc_ref, e_ref = refs[npg:]
    s_id = pl.program_id(1)
    gwid = hpg * HEAD_DIM
    qpos = past_len + lax.broadcasted_iota(I32, (tq, 1), 0)
    k_chunk, v_chunk = 2 * NSA_KV_GROUPS, 3 * NSA_KV_GROUPS

    @pl.when(s_id == 0)
    def _():
        _flash_init(m_ref, l_ref, acc_ref)

    tk = npg * page
    k0 = s_id * tk
    kpos = k0 + lax.broadcasted_iota(I32, (1, tk), 1)
    for g in range(NSA_KV_GROUPS):
        q6 = _stack_heads(q_ref, g * gwid, hpg)
        selb = sel_ref[g].astype(BF16)
        k = jnp.concatenate([pr[pl.ds(k_chunk + g, page, stride=PAGE_CHUNKS), :] for pr in page_refs],
                            axis=0).astype(BF16)
        v = jnp.concatenate([pr[pl.ds(v_chunk + g, page, stride=PAGE_CHUNKS), :] for pr in page_refs],
                            axis=0).astype(BF16)
        allowed = _block_mask(selb, k0, tk) & (kpos <= qpos)
        _flash_tile(q6, k, v, jnp.where(allowed, 0.0, NEG), m_ref.at[g], l_ref.at[g], acc_ref.at[g], e_ref, hpg)

    @pl.when(s_id == pl.num_programs(1) - 1)
    def _():
        cur = past_len // SEL_BLOCK
        nn = new_ref.shape[0]
        npos = past_len + lax.broadcasted_iota(I32, (1, LANE), 1)
        fill = jnp.zeros((LANE - nn, HEAD_DIM), F32)
        for g in range(NSA_KV_GROUPS):
            q6 = _stack_heads(q_ref, g * gwid, hpg)
            k = jnp.concatenate([new_ref[:, g * HEAD_DIM:(g + 1) * HEAD_DIM], fill], axis=0).astype(BF16)
            v = jnp.concatenate([new_ref[:, (NSA_KV_GROUPS + g) * HEAD_DIM:(NSA_KV_GROUPS + g + 1) * HEAD_DIM],
                                 fill], axis=0).astype(BF16)
            allowed = (sel_ref[g][:, cur:cur + 1] > 0.5) & (npos <= qpos) & (npos < past_len + nn)
            _flash_tile(q6, k, v, jnp.where(allowed, 0.0, NEG), m_ref.at[g], l_ref.at[g], acc_ref.at[g],
                        e_ref.at[:, 0:LANE], hpg)
            o = acc_ref[g] / l_ref[g]
            for i in range(hpg):
                o_ref[:, g * gwid + i * HEAD_DIM:g * gwid + (i + 1) * HEAD_DIM] = o[i * tq:(i + 1) * tq]


def slc_attend_paged(q, sel, kv4_new, cache, li, page_table, npg):
    b, tq, qw = q.shape
    n_pages = page_table.shape[1]
    page = cache.shape[2] // PAGE_CHUNKS
    past_len = n_pages * page
    gwid = qw // NSA_KV_GROUPS
    hpg = gwid // HEAD_DIM
    nbp = sel.shape[3]
    cw = 2 * NSA_KV_GROUPS * HEAD_DIM
    assert n_pages % npg == 0 and past_len % SEL_BLOCK == 0 and tq <= SEL_BLOCK
    grid_spec = pltpu.PrefetchScalarGridSpec(
        num_scalar_prefetch=1,
        grid=(b, n_pages // npg),
        in_specs=[
            pl.BlockSpec((None, tq, qw), lambda i, s, pt: (i, 0, 0)),
            pl.BlockSpec((None, NSA_KV_GROUPS, tq, nbp), lambda i, s, pt: (i, 0, 0, 0)),
            pl.BlockSpec((None, tq, cw), lambda i, s, pt: (i, 0, 1)),
        ] + [_page_spec(li, k, npg, page) for k in range(npg)],
        out_specs=pl.BlockSpec((None, tq, qw), lambda i, s, pt: (i, 0, 0)),
        scratch_shapes=[pltpu.VMEM((NSA_KV_GROUPS, hpg * tq, LANE), F32),
                        pltpu.VMEM((NSA_KV_GROUPS, hpg * tq, LANE), F32),
                        pltpu.VMEM((NSA_KV_GROUPS, hpg * tq, HEAD_DIM), F32),
                        pltpu.VMEM((hpg * tq, npg * page), _prob_dtype(tq))],
    )
    return pl.pallas_call(
        functools.partial(_slc_paged_body, npg=npg, tq=tq, hpg=hpg, past_len=past_len, page=page),
        grid_spec=grid_spec,
        out_shape=jax.ShapeDtypeStruct((b, tq, qw), F32),
        compiler_params=_cparams("parallel", "arbitrary"),
        name="slc_attend_paged",
    )(page_table, q, sel, kv4_new, *([cache] * npg))


def _win_body(q_ref, k_ref, v_ref, o_ref, l_ref, e_ref, *, qb, span, pos0, kpos0, hpg):
    blk = pl.program_id(2)
    q6 = _stack_heads(q_ref, 0, hpg)
    start = pl.multiple_of(blk * qb, qb)
    k = k_ref[pl.ds(start, span), :].astype(BF16)
    v = v_ref[pl.ds(start, span), :].astype(BF16)
    kpos = kpos0 + blk * qb + lax.broadcasted_iota(I32, (1, span), 1)
    qpos = pos0 + blk * qb + lax.broadcasted_iota(I32, (qb, 1), 0)
    dist = qpos - kpos
    allowed = (kpos >= 0) & (dist >= 0) & (dist < WINDOW)
    bias = jnp.where(allowed, 0.0, NEG)
    s_all = _dot_nt(q6, k)
    for i in range(hpg):
        rows = slice(i * qb, (i + 1) * qb)
        s = s_all[rows] * SCALE_LOG2E + bias
        e = jnp.exp2(s - jnp.max(s, axis=-1, keepdims=True))
        l_ref[rows, :] = jnp.broadcast_to(jnp.sum(e, axis=-1, keepdims=True), (qb, LANE))
        e_ref[rows, :] = e.astype(e_ref.dtype)
    o = _dot(e_ref[...].astype(BF16), v) / l_ref[...]
    for i in range(hpg):
        o_ref[:, i * HEAD_DIM:(i + 1) * HEAD_DIM] = o[i * qb:(i + 1) * qb]


def win_attend(q, kext, qb, span, pos0, kpos0):
    b, t, qw = q.shape
    lk = kext.shape[1]
    gwid = qw // NSA_KV_GROUPS
    hpg = gwid // HEAD_DIM
    assert t % qb == 0 and (t // qb - 1) * qb + span <= lk and qb % SUBLANE == 0 and span % SUBLANE == 0
    return pl.pallas_call(
        functools.partial(_win_body, qb=qb, span=span, pos0=pos0, kpos0=kpos0, hpg=hpg),
        grid=(b, NSA_KV_GROUPS, t // qb),
        in_specs=[
            pl.BlockSpec((None, qb, gwid), lambda i, g, j: (i, j, g)),
            pl.BlockSpec((None, lk, HEAD_DIM), lambda i, g, j: (i, 0, g)),
            pl.BlockSpec((None, lk, HEAD_DIM), lambda i, g, j: (i, 0, NSA_KV_GROUPS + g)),
        ],
        out_specs=pl.BlockSpec((None, qb, gwid), lambda i, g, j: (i, j, g)),
        out_shape=jax.ShapeDtypeStruct((b, t, qw), F32),
        scratch_shapes=[pltpu.VMEM((hpg * qb, LANE), F32), pltpu.VMEM((hpg * qb, span), _prob_dtype(qb))],
        compiler_params=_cparams("parallel", "parallel", "parallel"),
        name="win_attend",
    )(q, kext, kext)


def _row_tile(n, pref):
    t = min(n, pref)
    assert n % t == 0
    return t


def kernel(x_prompt, x_sample, mem_prompt, cache_mem_kv, cache_nsa_kv, cache_nsa_win, state_pool, page_table, g_norm_mix, g_norm_mlp, g_norm_mem, w_mem_kv, mem_qk_gain, w_out, w_mlp_up, w_mlp_down, w_in_pool, w_pool_group, pool_scale, w_in_nsa, nsa_gate_bias, nsa_qk_gain, cmp_pos, cmp_w1, cmp_w2):
    depth = g_norm_mix.shape[0]
    bp, tp, d = x_prompt.shape
    db, ts, _ = x_sample.shape
    n_pages = page_table.shape[1]
    page = cache_nsa_kv.shape[2]
    past_len = n_pages * page
    win_buf = cache_nsa_win.shape[2]
    tok_w = w_pool_group.shape[1] * w_pool_group.shape[2]
    n_heads = tok_w // HEAD_DIM
    n_gate = 3 * n_heads
    kvw = 4 * NSA_KV_GROUPS * HEAD_DIM
    winw = 2 * NSA_KV_GROUPS * HEAD_DIM
    assert win_buf == WINDOW and past_len % SEL_BLOCK == 0 and ts <= SUBLANE and tp % QBLOCK == 0

    tsp = SUBLANE
    xs = jnp.pad(x_sample, ((0, 0), (0, tsp - ts), (0, 0))).reshape(db * tsp, d)
    xp = x_prompt.reshape(bp * tp, d)
    n_p, n_s = bp * tp, db * tsp
    tm_p, tm_s = _row_tile(n_p, 256), n_s
    mem2d = mem_prompt.reshape(bp * mem_prompt.shape[1], d)
    mem_len = mem_prompt.shape[1]
    cache_view = cache_nsa_kv.reshape(cache_nsa_kv.shape[0], cache_nsa_kv.shape[1], page * PAGE_CHUNKS, HEAD_DIM)
    npg = min(16, n_pages)

    mem_kv_p, nsa_kv_p, nsa_kv_s, win_p, win_s, pool_p, pool_s = [], [], [], [], [], [], []
    mem_segs = ((2 * MEM_WIDTH, (1,) * MEM_HEADS + (None,) * MEM_HEADS),)
    pool_segs = ((tok_w, None), (MEM_WIDTH, None))
    nsa_segs = ((tok_w, (0,) * n_heads),
                (kvw, (None, None, None, None, 2, 2, None, None)),
                (winw, (3, 3, None, None)),
                (MEM_WIDTH, None),
                (LANE, None))

    wo_b = w_out.astype(BF16)
    wu_b = w_mlp_up.astype(BF16)
    wd_b = w_mlp_down.astype(BF16)
    wmem_b = w_mem_kv.astype(BF16)
    wpool_b = w_in_pool.astype(BF16)
    wg_b = w_pool_group.astype(BF16)
    kv_end = tok_w + kvw + winw
    wnsa_b = jnp.concatenate(
        [w_in_nsa[:, :, :kv_end], w_in_nsa[:, :, kv_end + n_gate:], w_in_nsa[:, :, kv_end:kv_end + n_gate],
         jnp.zeros(w_in_nsa.shape[:2] + (LANE - n_gate,), w_in_nsa.dtype)], axis=2).astype(BF16)
    mem_cache = cache_mem_kv.reshape(cache_mem_kv.shape[0], db, cache_mem_kv.shape[2] * 2 * MEM_HEADS, HEAD_DIM)

    for i in range(depth):
        li = i // 2
        (mkv,) = in_proj(mem2d, g_norm_mem[i], wmem_b, i, mem_qk_gain[i], mem_segs,
                         _row_tile(mem2d.shape[0], 256))
        mkv_p = mkv.reshape(bp, mem_len, 2 * MEM_WIDTH)
        mem_kv_p.append(mkv_p.reshape(bp, mem_len, 2, MEM_HEADS, HEAD_DIM))
        gated = None
        if i % 2 == 0:
            gains = jnp.ones((1, HEAD_DIM), F32)
            up, qmp = in_proj(xp, g_norm_mix[i], wpool_b, li, gains, pool_segs, tm_p)
            us, qms = in_proj(xs, g_norm_mix[i], wpool_b, li, gains, pool_segs, tm_s)
            up3 = up.reshape(bp, tp, tok_w)
            tok_p = pool_mix(up3, wg_b, li, pool_scale[li], 0, _row_tile(tp, 512)).reshape(n_p, tok_w)
            us3 = us.reshape(db, tsp, tok_w)[:, :ts]
            lead = _round_up(POOL_HIST + ts, SUBLANE) - (POOL_HIST + ts)
            ext = jnp.concatenate([state_pool[li], us3], axis=1)
            ext_pad = jnp.pad(ext, ((0, 0), (lead, 0), (0, 0)))
            l_ext = ext_pad.shape[1]
            tok_e = pool_mix(ext_pad, wg_b, li, pool_scale[li], past_len - POOL_HIST - lead, l_ext)
            tok_s = jnp.pad(tok_e[:, l_ext - ts:], ((0, 0), (0, tsp - ts), (0, 0))).reshape(n_s, tok_w)
            pool_p.append(up3[:, tp - POOL_HIST:])
            pool_s.append(ext[:, -POOL_HIST:])
        else:
            gains = nsa_qk_gain[li]
            bias = jnp.pad(nsa_gate_bias[li], (0, LANE - n_gate)).reshape(1, LANE)
            w1 = cmp_w1[li]
            half = CMP_STRIDE * HEAD_DIM
            w1k = jnp.concatenate([w1[0, :half], w1[0, half:]], axis=1).astype(BF16)
            w1v = jnp.concatenate([w1[1, :half], w1[1, half:]], axis=1).astype(BF16)
            w1b = w1.astype(BF16)
            w2b = cmp_w2[li].astype(BF16)

            q_p, kv4_p, wn_p, qmp, gt_p = in_proj(xp, g_norm_mix[i], wnsa_b, li, gains, nsa_segs, tm_p)
            q3 = q_p.reshape(bp, tp, tok_w)
            kv43 = kv4_p.reshape(bp, tp, kvw)
            wn3 = wn_p.reshape(bp, tp, winw)
            n_cmp = tp // CMP_STRIDE
            ncp = _round_up(n_cmp, LANE)
            ab = cmp_stage1_dense(kv43, w1k, w1v)
            ct = cmp_stage2([ab], cmp_pos[li], w1b, w2b, gains[1], ncp)
            oc_p, sel_p = cmp_attend_topk(q3, ct, 0, tp // SEL_BLOCK, _row_tile(tp, 256))
            os_p = slc_attend_prompt(q3, kv43, sel_p, _row_tile(tp, 256), _row_tile(tp, 512))
            kext = jnp.concatenate([jnp.zeros((bp, WINDOW, winw), F32), wn3], axis=1)
            ow_p = win_attend(q3, kext, QBLOCK, WINDOW + QBLOCK, 0, -WINDOW)
            tok_p = (oc_p.reshape(n_p, tok_w), os_p.reshape(n_p, tok_w), ow_p.reshape(n_p, tok_w))

            q_s, kv4_s, wn_s, qms, gt_s = in_proj(xs, g_norm_mix[i], wnsa_b, li, gains, nsa_segs, tm_s)
            qs3 = q_s.reshape(db, tsp, tok_w)
            kv4s3 = kv4_s.reshape(db, tsp, kvw)
            wns3 = wn_s.reshape(db, tsp, winw)
            new_rows = jnp.pad(kv4s3[:, :ts], ((0, 0), (0, page - ts), (0, 0)))
            n_rows = _round_up(past_len + ts, SEL_BLOCK)
            n_cmp_s = n_rows // CMP_STRIDE
            ncp_s = _round_up(n_cmp_s, LANE)
            ab_past = cmp_stage1_paged(cache_view, li, page_table, w1k, w1v, npg)
            ab_new = cmp_stage1_dense(new_rows, w1k, w1v)
            ct_s = cmp_stage2([ab_past, ab_new], cmp_pos[li], w1b, w2b, gains[1], ncp_s)
            oc_s, sel_s = cmp_attend_topk(qs3, ct_s, past_len, n_rows // SEL_BLOCK, tsp)
            os_s = slc_attend_paged(qs3, sel_s, kv4s3, cache_view, li, page_table, npg)
            wext = jnp.concatenate([cache_nsa_win[li].reshape(db, win_buf, winw), wns3[:, :ts]], axis=1)
            kext_s = jnp.pad(wext, ((0, 0), (0, tsp - ts), (0, 0)))
            ow_s = win_attend(qs3, kext_s, tsp, win_buf + tsp, past_len, past_len - win_buf)
            tok_s = (oc_s.reshape(n_s, tok_w), os_s.reshape(n_s, tok_w), ow_s.reshape(n_s, tok_w))
            gated = ((gt_p, bias), (gt_s, bias))

            nsa_kv_p.append(kv43.reshape(bp, tp, 4, NSA_KV_GROUPS, HEAD_DIM))
            nsa_kv_s.append(kv4s3[:, :ts].reshape(db, ts, 4, NSA_KV_GROUPS, HEAD_DIM))
            wlen = min(WINDOW, tp)
            win_p.append(wn3[:, tp - wlen:].reshape(bp, wlen, 2, NSA_KV_GROUPS, HEAD_DIM))
            win_s.append(wext[:, -win_buf:].reshape(db, win_buf, 2, NSA_KV_GROUPS, HEAD_DIM))

        hp = out_proj(tok_p, qmp, mkv_p, mem_qk_gain[i, 0], xp, wo_b, i, tm_p, tp, gates=gated and gated[0])
        hs = out_proj(tok_s, qms, mem_cache, mem_qk_gain[i, 0], xs, wo_b, i, tm_s, tsp, gates=gated and gated[1],
                      kv_layer=i)
        xp = mlp(hp, g_norm_mlp[i], wu_b, wd_b, i, _row_tile(n_p, 512), 1024)
        xs = mlp(hs, g_norm_mlp[i], wu_b, wd_b, i, tm_s, 1024)

    y_p = xp.reshape(bp, tp, d)
    y_s = xs.reshape(db, tsp, d)[:, :ts]
    return (y_p, y_s, jnp.stack(mem_kv_p), jnp.stack(nsa_kv_p), jnp.stack(nsa_kv_s),
            jnp.stack(win_p), jnp.stack(win_s), jnp.stack(pool_p), jnp.stack(pool_s))
```

```python
import functools

import jax
import jax.numpy as jnp
from jax import lax
from jax.experimental import pallas as pl
from jax.experimental.pallas import tpu as pltpu

F32 = jnp.float32
BF16 = jnp.bfloat16
I32 = jnp.int32

HEAD_DIM = 128
MEM_HEADS = 4
MEM_WIDTH = MEM_HEADS * HEAD_DIM
NSA_KV_GROUPS = 2
CMP_STRIDE = 16
CMP_BLOCK = 2 * CMP_STRIDE
SEL_BLOCK = 64
N_SEL = 16
WINDOW = 512
QBLOCK = 128
POOL_WINDOWS = (2, 4, 8, 16)
POOL_HIST = max(POOL_WINDOWS) - 1
POOL_HALO = 16
ATTN_SCALE = HEAD_DIM ** -0.5
SCALE_LOG2E = ATTN_SCALE * 1.4426950408889634
EPS = 1e-6
NEG = -1e30
BIG = 1e30
LOWEST = -3.0e38
SEL_SHIFT = SEL_BLOCK.bit_length() - 1
RATIO_SHIFT = (SEL_BLOCK // CMP_STRIDE).bit_length() - 1
assert 1 << SEL_SHIFT == SEL_BLOCK and 1 << RATIO_SHIFT == SEL_BLOCK // CMP_STRIDE

LANE = 128
SUBLANE = 8
VMEM_LIMIT_BYTES = 56 * 1024 * 1024


def _cparams(*sem):
    return pltpu.CompilerParams(dimension_semantics=sem, vmem_limit_bytes=VMEM_LIMIT_BYTES)


def _round_up(x, m):
    return (x + m - 1) // m * m


def _rms(x, gain):
    return x * lax.rsqrt(jnp.mean(x * x, axis=-1, keepdims=True) + EPS) * gain


def _dot(a, b):
    return jnp.dot(a, b, preferred_element_type=F32)


def _dot_nt(a, b):
    return lax.dot_general(a, b, (((1,), (1,)), ((), ())), preferred_element_type=F32)


def _in_proj_body(x_ref, g_ref, w_ref, gains_ref, *out_refs, segs):
    xb = _rms(x_ref[...], g_ref[...]).astype(BF16)
    col = 0
    for o_ref, (width, norms) in zip(out_refs, segs):
        y = _dot(xb, w_ref[:, col:col + width])
        if norms is None:
            o_ref[...] = y
        else:
            for c, gi in enumerate(norms):
                yc = y[:, c * LANE:(c + 1) * LANE]
                if gi is not None:
                    yc = _rms(yc, gains_ref[gi:gi + 1, :])
                o_ref[:, c * LANE:(c + 1) * LANE] = yc
        col += width


def in_proj(x2d, g, w, layer, gains, segs, tm):
    n, d = x2d.shape
    wtot = w.shape[2]
    assert n % tm == 0 and wtot == sum(s[0] for s in segs)
    return pl.pallas_call(
        functools.partial(_in_proj_body, segs=segs),
        grid=(n // tm,),
        in_specs=[
            pl.BlockSpec((tm, d), lambda i: (i, 0)),
            pl.BlockSpec((1, d), lambda i: (0, 0)),
            pl.BlockSpec((None, d, wtot), lambda i: (layer, 0, 0)),
            pl.BlockSpec(gains.shape, lambda i: (0, 0)),
        ],
        out_specs=[pl.BlockSpec((tm, s[0]), lambda i: (i, 0)) for s in segs],
        out_shape=[jax.ShapeDtypeStruct((n, s[0]), F32) for s in segs],
        compiler_params=_cparams("parallel"),
        name="in_proj",
    )(x2d, g.reshape(1, d), w, gains)


def _mem_attend(qm, kv_ref, gq_ref, row_major):
    chunks = 2 * MEM_HEADS
    m = kv_ref.shape[0] // chunks if row_major else kv_ref.shape[0]
    heads = []
    for h in range(MEM_HEADS):
        sl = slice(h * HEAD_DIM, (h + 1) * HEAD_DIM)
        q = _rms(qm[:, sl], gq_ref[...]).astype(BF16)
        if row_major:
            k = kv_ref[pl.ds(h, m, stride=chunks), :].astype(BF16)
            v = kv_ref[pl.ds(MEM_HEADS + h, m, stride=chunks), :].astype(BF16)
        else:
            k = kv_ref[:, sl].astype(BF16)
            v = kv_ref[:, MEM_WIDTH + h * HEAD_DIM:MEM_WIDTH + (h + 1) * HEAD_DIM].astype(BF16)
        s = _dot_nt(q, k) * ATTN_SCALE
        e = jnp.exp(s - jnp.max(s, axis=-1, keepdims=True))
        heads.append(_dot(e.astype(BF16), v) / jnp.sum(e, axis=-1, keepdims=True))
    return jnp.concatenate(heads, axis=1)


def _out_proj_body(*refs, gated, row_major, rows_per_batch):
    if gated:
        gt_ref, bias_ref, oc_ref, os_ref, ow_ref = refs[:5]
        refs = refs[5:]
        gs = jax.nn.sigmoid(gt_ref[...] + bias_ref[...])
        heads = []
        for h in range(oc_ref.shape[1] // HEAD_DIM):
            sl = slice(h * HEAD_DIM, (h + 1) * HEAD_DIM)
            heads.append(gs[:, 3 * h:3 * h + 1] * oc_ref[:, sl] + gs[:, 3 * h + 1:3 * h + 2] * os_ref[:, sl]
                         + gs[:, 3 * h + 2:3 * h + 3] * ow_ref[:, sl])
        tok = jnp.concatenate(heads, axis=1)
    else:
        tok = refs[0][...]
        refs = refs[1:]
    qm_ref, kv_ref, gq_ref, x_ref, w_ref, o_ref = refs
    tw = tok.shape[1]
    acc = _dot(tok.astype(BF16), w_ref[0:tw, :])
    if len(kv_ref.shape) == 2:
        mem = _mem_attend(qm_ref[...], kv_ref, gq_ref, row_major)
    else:
        mem = jnp.concatenate([_mem_attend(qm_ref[b * rows_per_batch:(b + 1) * rows_per_batch, :], kv_ref.at[b],
                                           gq_ref, row_major) for b in range(kv_ref.shape[0])], axis=0)
    acc = acc + _dot(mem.astype(BF16), w_ref[tw:, :])
    o_ref[...] = x_ref[...] + acc


def out_proj(tok, qm, mkv, gq, x, w, layer, tm, rows_per_batch, gates=None, kv_layer=None):
    n, d = x.shape
    toks = tok if gates is not None else (tok,)
    tw = toks[0].shape[1]
    assert rows_per_batch % tm == 0 or (tm == n and tm % rows_per_batch == 0 and rows_per_batch % SUBLANE == 0)
    tiles = max(rows_per_batch // tm, 1)
    one = rows_per_batch % tm == 0
    row = lambda width: pl.BlockSpec((tm, width), lambda i: (i, 0))
    lead_specs, lead_args = [], []
    if gates is not None:
        lead_specs = [row(LANE), pl.BlockSpec((1, LANE), lambda i: (0, 0))]
        lead_args = list(gates)
    if kv_layer is None:
        kv_spec = (pl.BlockSpec((None,) + mkv.shape[1:], lambda i: (i // tiles, 0, 0)) if one else
                   pl.BlockSpec(mkv.shape, lambda i: (0, 0, 0)))
    else:
        kv_spec = (pl.BlockSpec((None, None) + mkv.shape[2:], lambda i: (kv_layer, i // tiles, 0, 0)) if one else
                   pl.BlockSpec((None,) + mkv.shape[1:], lambda i: (kv_layer, 0, 0, 0)))
    return pl.pallas_call(
        functools.partial(_out_proj_body, gated=gates is not None, row_major=kv_layer is not None,
                          rows_per_batch=rows_per_batch),
        grid=(n // tm,),
        in_specs=lead_specs + [row(tw) for _ in toks] + [
            row(MEM_WIDTH), kv_spec, pl.BlockSpec((1, HEAD_DIM), lambda i: (0, 0)), row(d),
            pl.BlockSpec((None, tw + MEM_WIDTH, d), lambda i: (layer, 0, 0)),
        ],
        out_specs=row(d),
        out_shape=jax.ShapeDtypeStruct((n, d), F32),
        compiler_params=_cparams("parallel"),
        name="out_proj",
    )(*lead_args, *toks, qm, mkv, gq.reshape(1, HEAD_DIM), x, w)


def _mlp_body(h_ref, g_ref, wu_ref, wd_ref, o_ref, xn_ref, acc_ref):
    k = pl.program_id(1)

    @pl.when(k == 0)
    def _():
        xn_ref[...] = _rms(h_ref[...], g_ref[...]).astype(BF16)
        acc_ref[...] = jnp.zeros_like(acc_ref)

    a = jnp.maximum(_dot(xn_ref[...], wu_ref[...]), 0.0)
    acc_ref[...] += _dot((a * a).astype(BF16), wd_ref[...])

    @pl.when(k == pl.num_programs(1) - 1)
    def _():
        o_ref[...] = h_ref[...] + acc_ref[...]


def mlp(h, g, w_up, w_down, layer, tm, tf):
    n, d = h.shape
    ff = w_up.shape[2]
    return pl.pallas_call(
        _mlp_body,
        grid=(n // tm, ff // tf),
        in_specs=[
            pl.BlockSpec((tm, d), lambda i, k: (i, 0)),
            pl.BlockSpec((1, d), lambda i, k: (0, 0)),
            pl.BlockSpec((None, d, tf), lambda i, k: (layer, 0, k)),
            pl.BlockSpec((None, tf, d), lambda i, k: (layer, k, 0)),
        ],
        out_specs=pl.BlockSpec((tm, d), lambda i, k: (i, 0)),
        out_shape=jax.ShapeDtypeStruct((n, d), F32),
        scratch_shapes=[pltpu.VMEM((tm, d), BF16), pltpu.VMEM((tm, d), F32)],
        compiler_params=_cparams("parallel", "arbitrary"),
        name="mlp",
    )(h, g.reshape(1, d), w_up, w_down)


def _pool_body(cur_ref, halo_ref, wg_ref, sc_ref, o_ref, ext_ref, *, pos0, tt, gw):
    t = pl.program_id(1)
    ext_ref[0:POOL_HALO, :] = jnp.where(t == 0, 0.0, halo_ref[...])
    ext_ref[POOL_HALO:POOL_HALO + tt, :] = cur_ref[...]
    row = t * tt + lax.broadcasted_iota(I32, (tt, 1), 0)
    for gi, w in enumerate(POOL_WINDOWS):
        c0, c1 = gi * gw, (gi + 1) * gw
        x = ext_ref[POOL_HALO:POOL_HALO + tt, c0:c1]
        s = x
        for dd in range(1, w):
            s = s + ext_ref[POOL_HALO - dd:POOL_HALO - dd + tt, c0:c1]
        cnt = jnp.minimum(pos0 + row + 1, w).astype(F32)
        pooled = s / cnt - x
        o_ref[:, c0:c1] = _dot(pooled.astype(BF16), wg_ref[gi]) * sc_ref[:, c0:c1]


def pool_mix(u, w_group, layer, scale, pos0, tt):
    b, l, c = u.shape
    _, ng, gw, _ = w_group.shape
    assert l % tt == 0 and tt % POOL_HALO == 0 or l == tt
    halo_per_tile = tt // POOL_HALO if tt % POOL_HALO == 0 else 0
    return pl.pallas_call(
        functools.partial(_pool_body, pos0=pos0, tt=tt, gw=gw),
        grid=(b, l // tt),
        in_specs=[
            pl.BlockSpec((None, tt, c), lambda i, j: (i, j, 0)),
            pl.BlockSpec((None, POOL_HALO, c), lambda i, j: (i, jnp.maximum(j * halo_per_tile - 1, 0), 0)),
            pl.BlockSpec((None, ng, gw, gw), lambda i, j: (layer, 0, 0, 0)),
            pl.BlockSpec((1, c), lambda i, j: (0, 0)),
        ],
        out_specs=pl.BlockSpec((None, tt, c), lambda i, j: (i, j, 0)),
        out_shape=jax.ShapeDtypeStruct((b, l, c), F32),
        scratch_shapes=[pltpu.VMEM((POOL_HALO + tt, c), F32)],
        compiler_params=_cparams("parallel", "parallel"),
        name="pool_mix",
    )(u, u, w_group, scale.reshape(1, c))


def _cmp1_compute(page_refs, wk_ref, wv_ref, o_ref, stage_ref, rows):
    m = len(page_refs) * rows // CMP_STRIDE
    for slot, w_ref in ((0, wk_ref), (1, wv_ref)):
        xs = []
        for g in range(NSA_KV_GROUPS):
            j = slot * NSA_KV_GROUPS + g
            for p, pr in enumerate(page_refs):
                stage_ref[p * rows:(p + 1) * rows, :] = pr[:, j * HEAD_DIM:(j + 1) * HEAD_DIM]
            pieces = [stage_ref[pl.ds(r, m, stride=CMP_STRIDE), :] for r in range(CMP_STRIDE)]
            xs.append(jnp.concatenate(pieces, axis=1).astype(BF16))
        y = _dot(jnp.concatenate(xs, axis=0), w_ref[...])
        for g in range(NSA_KV_GROUPS):
            j = slot * NSA_KV_GROUPS + g
            o_ref[:, j * 2 * HEAD_DIM:(j + 1) * 2 * HEAD_DIM] = y[g * m:(g + 1) * m]


def _cmp1_dense_body(x_ref, wk_ref, wv_ref, o_ref, stage_ref, *, rows):
    _cmp1_compute([x_ref], wk_ref, wv_ref, o_ref, stage_ref, rows)


def cmp_stage1_dense(rows4, w1k, w1v, lane_block=0):
    b, l, _ = rows4.shape
    nh = l // CMP_STRIDE
    cw = 2 * NSA_KV_GROUPS * HEAD_DIM
    return pl.pallas_call(
        functools.partial(_cmp1_dense_body, rows=l),
        grid=(b,),
        in_specs=[
            pl.BlockSpec((None, l, cw), lambda i: (i, 0, lane_block)),
            pl.BlockSpec(w1k.shape, lambda i: (0, 0)),
            pl.BlockSpec(w1v.shape, lambda i: (0, 0)),
        ],
        out_specs=pl.BlockSpec((None, nh, 2 * cw), lambda i: (i, 0, 0)),
        out_shape=jax.ShapeDtypeStruct((b, nh, 2 * cw), F32),
        scratch_shapes=[pltpu.VMEM((l, HEAD_DIM), F32)],
        compiler_params=_cparams("parallel"),
        name="cmp_stage1_dense",
    )(rows4, w1k, w1v)


PAGE_CHUNKS = 4 * NSA_KV_GROUPS


def _cmp1_paged_body(pt_ref, *refs, npg, page):
    del pt_ref
    page_refs = refs[:npg]
    wk_ref, wv_ref, o_ref = refs[npg:npg + 3]
    nh = page // CMP_STRIDE
    m = npg * nh
    for slot, w_ref in ((0, wk_ref), (1, wv_ref)):
        xs = []
        for g in range(NSA_KV_GROUPS):
            j = slot * NSA_KV_GROUPS + g
            per_page = []
            for pr in page_refs:
                pieces = [pr[pl.ds(r * PAGE_CHUNKS + j, nh, stride=CMP_STRIDE * PAGE_CHUNKS), :]
                          for r in range(CMP_STRIDE)]
                per_page.append(jnp.concatenate(pieces, axis=1))
            xs.append(jnp.concatenate(per_page, axis=0).astype(BF16))
        y = _dot(jnp.concatenate(xs, axis=0), w_ref[...])
        for g in range(NSA_KV_GROUPS):
            j = slot * NSA_KV_GROUPS + g
            o_ref[:, j * 2 * HEAD_DIM:(j + 1) * 2 * HEAD_DIM] = y[g * m:(g + 1) * m]


def _page_spec(li, k, npg, page):
    return pl.BlockSpec((None, None, page * PAGE_CHUNKS, HEAD_DIM),
                        lambda i, s, pt: (li, pt[i, s * npg + k], 0, 0))


def cmp_stage1_paged(cache, li, page_table, w1k, w1v, npg):
    b, n_pages = page_table.shape
    page = cache.shape[2] // PAGE_CHUNKS
    nh = page // CMP_STRIDE
    cw = 2 * NSA_KV_GROUPS * HEAD_DIM
    assert n_pages % npg == 0
    grid_spec = pltpu.PrefetchScalarGridSpec(
        num_scalar_prefetch=1,
        grid=(b, n_pages // npg),
        in_specs=[_page_spec(li, k, npg, page) for k in range(npg)] + [
            pl.BlockSpec(w1k.shape, lambda i, s, pt: (0, 0)),
            pl.BlockSpec(w1v.shape, lambda i, s, pt: (0, 0)),
        ],
        out_specs=pl.BlockSpec((None, npg * nh, 2 * cw), lambda i, s, pt: (i, s, 0)),
    )
    return pl.pallas_call(
        functools.partial(_cmp1_paged_body, npg=npg, page=page),
        grid_spec=grid_spec,
        out_shape=jax.ShapeDtypeStruct((b, n_pages * nh, 2 * cw), F32),
        compiler_params=_cparams("parallel", "arbitrary"),
        name="cmp_stage1_paged",
    )(page_table, *([cache] * npg), w1k, w1v)


def _cmp2_body(*refs, n_ab, ncp):
    ab_refs = refs[:n_ab]
    pos_ref, w1_ref, w2_ref, gk_ref, o_ref = refs[n_ab:]
    n_have = sum(r.shape[0] for r in ab_refs)
    fill = [jnp.zeros((ncp + SUBLANE - n_have, HEAD_DIM), F32)]
    for slot in range(2):
        posflat = jnp.concatenate([pos_ref[slot, r:r + 1, :] for r in range(CMP_BLOCK)], axis=1)
        posb = jnp.broadcast_to(posflat, (SUBLANE, CMP_BLOCK * HEAD_DIM)).astype(BF16)
        posc = _dot(posb, w1_ref[slot])[0:1]
        for g in range(NSA_KV_GROUPS):
            j = slot * NSA_KV_GROUPS + g
            a_lanes = slice(j * 2 * HEAD_DIM, j * 2 * HEAD_DIM + HEAD_DIM)
            b_lanes = slice(j * 2 * HEAD_DIM + HEAD_DIM, (j + 1) * 2 * HEAD_DIM)
            a = jnp.concatenate([r[:, a_lanes] for r in ab_refs] + fill, axis=0)
            bm = jnp.concatenate([r[:, b_lanes] for r in ab_refs] + fill, axis=0)
            bm_next = pltpu.roll(bm, ncp + SUBLANE - 1, axis=0)
            hid = jax.nn.gelu(a[0:ncp] + bm_next[0:ncp] + posc)
            out = _dot(hid.astype(BF16), w2_ref[slot])
            if slot == 0:
                out = _rms(out, gk_ref[...])
            o_ref[:, j * HEAD_DIM:(j + 1) * HEAD_DIM] = out


def cmp_stage2(abs_, pos, w1, w2, gk, ncp):
    b, _, w = abs_[0].shape
    assert all(a.shape[1] % SUBLANE == 0 for a in abs_) and sum(a.shape[1] for a in abs_) <= ncp + SUBLANE
    cw = 2 * NSA_KV_GROUPS * HEAD_DIM
    return pl.pallas_call(
        functools.partial(_cmp2_body, n_ab=len(abs_), ncp=ncp),
        grid=(b,),
        in_specs=[pl.BlockSpec((None, a.shape[1], w), lambda i: (i, 0, 0)) for a in abs_] + [
            pl.BlockSpec(pos.shape, lambda i: (0, 0, 0)),
            pl.BlockSpec(w1.shape, lambda i: (0, 0, 0)),
            pl.BlockSpec(w2.shape, lambda i: (0, 0, 0)),
            pl.BlockSpec((1, HEAD_DIM), lambda i: (0, 0)),
        ],
        out_specs=pl.BlockSpec((None, ncp, cw), lambda i: (i, 0, 0)),
        out_shape=jax.ShapeDtypeStruct((b, ncp, cw), F32),
        compiler_params=_cparams("parallel"),
        name="cmp_stage2",
    )(*abs_, pos, w1, w2, gk.reshape(1, HEAD_DIM))


def _cmp_attn_body(q_ref, ct_ref, o_ref, sel_ref, st_ref, *, pos0, tq, ncp, n_blk, n_sel, hpg):
    t = pl.program_id(1)
    ng = NSA_KV_GROUPS
    gwid = hpg * HEAD_DIM
    ratio = SEL_BLOCK // CMP_STRIDE
    nbp = ncp // ratio
    q0 = pos0 + t * tq
    qpos = q0 + lax.broadcasted_iota(I32, (tq, 1), 0)
    cidx = lax.broadcasted_iota(I32, (1, ncp), 1)
    cmask = (cidx * CMP_STRIDE + (CMP_BLOCK - 1)) <= qpos
    bias = jnp.where(cmask, 0.0, NEG)
    rp = _round_up(tq, LANE)
    nbpp = sel_ref.shape[2]
    jb = lax.broadcasted_iota(I32, (nbp, 1), 0)
    jf = jb.astype(F32)
    cur = (q0 + lax.broadcasted_iota(I32, (1, rp), 1)) >> SEL_SHIFT
    forced = (jb == 0) | (jb == cur) | (jb == cur - 1)
    for g in range(ng):
        q6 = _stack_heads(q_ref, g * gwid, hpg, ATTN_SCALE)
        kc = ct_ref[:, g * HEAD_DIM:(g + 1) * HEAD_DIM].astype(BF16)
        vc = ct_ref[:, (ng + g) * HEAD_DIM:(ng + g + 1) * HEAD_DIM].astype(BF16)
        s_all = _dot_nt(q6, kc)
        imp = jnp.zeros((tq, ncp), F32)
        probs = []
        for i in range(hpg):
            s = s_all[i * tq:(i + 1) * tq] + bias
            e = jnp.exp(s - jnp.max(s, axis=-1, keepdims=True))
            p = jnp.where(cmask, e / jnp.sum(e, axis=-1, keepdims=True), 0.0)
            imp = imp + p
            probs.append(p)
        o = _dot(jnp.concatenate(probs, axis=0).astype(BF16), vc)
        for i in range(hpg):
            o_ref[:, g * gwid + i * HEAD_DIM:g * gwid + (i + 1) * HEAD_DIM] = o[i * tq:(i + 1) * tq]
        imp2 = imp + jnp.where(cidx == 0, 0.0, pltpu.roll(imp, 1, axis=1))
        t1 = imp2 + pltpu.roll(imp2, ncp - 1, axis=1)
        ps = t1 + pltpu.roll(t1, ncp - 2, axis=1)

        if rp > tq:
            ps = jnp.concatenate([ps, jnp.zeros((rp - tq, ncp), F32)], axis=0)
        ps_t = ps.T
        tiles = []
        for h in range(rp // LANE):
            st = st_ref.at[g * (rp // LANE) + h]
            st[...] = ps_t[:, h * LANE:(h + 1) * LANE]
            tiles.append(st[pl.ds(0, nbp, stride=ratio), :])
        ps_blk = jnp.concatenate(tiles, axis=1)
        score = jnp.where(forced, BIG, jnp.where(jb <= cur, ps_blk, NEG))
        score = jnp.where(jb < n_blk, score, LOWEST)
        sel_t = jnp.zeros((nbp, rp), F32)
        for _ in range(n_sel):
            m = jnp.max(score, axis=0, keepdims=True)
            first = jnp.min(jnp.where(score == m, jf, float(nbp)), axis=0, keepdims=True)
            pick = jf == first
            sel_t = jnp.where(pick & (m > 0.5 * NEG), 1.0, sel_t)
            score = jnp.where(pick, LOWEST, score)
        if nbpp > nbp:
            sel_t = jnp.concatenate([sel_t, jnp.zeros((nbpp - nbp, rp), F32)], axis=0)
        sel_ref[g] = sel_t.T[0:tq]


def cmp_attend_topk(q, ct, pos0, n_blk, tq):
    b, t, qw = q.shape
    ncp = ct.shape[1]
    gwid = qw // NSA_KV_GROUPS
    hpg = gwid // HEAD_DIM
    ratio = SEL_BLOCK // CMP_STRIDE
    assert ncp % LANE == 0 and t % tq == 0 and tq & (tq - 1) == 0 and tq % SUBLANE == 0
    nbpp = _round_up(ncp // ratio, LANE)
    n_sel = min(N_SEL, n_blk)
    return pl.pallas_call(
        functools.partial(_cmp_attn_body, pos0=pos0, tq=tq, ncp=ncp, n_blk=n_blk, n_sel=n_sel, hpg=hpg),
        grid=(b, t // tq),
        in_specs=[
            pl.BlockSpec((None, tq, qw), lambda i, j: (i, j, 0)),
            pl.BlockSpec((None, ncp, ct.shape[2]), lambda i, j: (i, 0, 0)),
        ],
        out_specs=[
            pl.BlockSpec((None, tq, qw), lambda i, j: (i, j, 0)),
            pl.BlockSpec((None, NSA_KV_GROUPS, tq, nbpp), lambda i, j: (i, 0, j, 0)),
        ],
        out_shape=[
            jax.ShapeDtypeStruct((b, t, qw), F32),
            jax.ShapeDtypeStruct((b, NSA_KV_GROUPS, t, nbpp), F32),
        ],
        scratch_shapes=[pltpu.VMEM((NSA_KV_GROUPS * (_round_up(tq, LANE) // LANE), ncp, LANE), F32)],
        compiler_params=_cparams("parallel", "parallel"),
        name="cmp_attend_topk",
    )(q, ct)


def _stack_heads(q_ref, lane0, hpg, scale=None):
    q = jnp.concatenate([q_ref[:, lane0 + i * HEAD_DIM:lane0 + (i + 1) * HEAD_DIM] for i in range(hpg)], axis=0)
    return (q if scale is None else q * scale).astype(BF16)


def _prob_dtype(rows_per_head):
    return BF16 if rows_per_head % (2 * SUBLANE) == 0 else F32


def _flash_init(m_ref, l_ref, acc_ref):
    m_ref[...] = jnp.full_like(m_ref, NEG)
    l_ref[...] = jnp.zeros_like(l_ref)
    acc_ref[...] = jnp.zeros_like(acc_ref)


def _flash_tile(q6, k, v, bias, m_ref, l_ref, acc_ref, e_ref, hpg):
    tq, tk = bias.shape
    assert tk % LANE == 0 and acc_ref.shape[1] == LANE
    s_all = _dot_nt(q6, k)
    for i in range(hpg):
        rows = slice(i * tq, (i + 1) * tq)
        s = s_all[rows] + bias
        m_old = m_ref[rows, :]
        m_new = jnp.maximum(m_old, jnp.max(s, axis=-1, keepdims=True))
        e = jnp.exp2(s - jnp.concatenate([m_new] * (tk // LANE), axis=1))
        alpha = jnp.exp2(m_old - m_new)
        l_ref[rows, :] = alpha * l_ref[rows, :] + jnp.sum(e, axis=-1, keepdims=True)
        acc_ref[rows, :] = alpha * acc_ref[rows, :]
        m_ref[rows, :] = m_new
        e_ref[rows, :] = e.astype(e_ref.dtype)
    acc_ref[...] += _dot(e_ref[...].astype(BF16), v)


def _block_mask(selb, k0, tk):
    nbp = selb.shape[1]
    blk = (k0 + lax.broadcasted_iota(I32, (nbp, tk), 1)) >> SEL_SHIFT
    expand = (lax.broadcasted_iota(I32, (nbp, tk), 0) == blk).astype(BF16)
    return _dot(selb, expand) > 0.5


def _slc_prompt_body(q_ref, k_ref, v_ref, sel_ref, o_ref, kb_ref, vb_ref, m_ref, l_ref, acc_ref, e_ref,
                     *, tq, tk, hpg):
    t = pl.program_id(2)

    @pl.when(t == 0)
    def _():
        kb_ref[...] = k_ref[...].astype(BF16)
        vb_ref[...] = v_ref[...].astype(BF16)

    q6 = _stack_heads(q_ref, 0, hpg, SCALE_LOG2E)
    selb = sel_ref[...].astype(BF16)
    qpos = t * tq + lax.broadcasted_iota(I32, (tq, 1), 0)
    _flash_init(m_ref, l_ref, acc_ref)

    def body(kt, carry):
        k0 = pl.multiple_of(kt * tk, tk)
        kpos = k0 + lax.broadcasted_iota(I32, (1, tk), 1)
        allowed = _block_mask(selb, k0, tk) & (kpos <= qpos)
        _flash_tile(q6, kb_ref[pl.ds(k0, tk), :], vb_ref[pl.ds(k0, tk), :], jnp.where(allowed, 0.0, NEG),
                    m_ref, l_ref, acc_ref, e_ref, hpg)
        return carry

    lax.fori_loop(0, (t * tq + tq - 1) // tk + 1, body, 0)
    o = acc_ref[...] / l_ref[...]
    for i in range(hpg):
        o_ref[:, i * HEAD_DIM:(i + 1) * HEAD_DIM] = o[i * tq:(i + 1) * tq]


def slc_attend_prompt(q, kv4, sel, tq, tk):
    b, t, qw = q.shape
    gwid = qw // NSA_KV_GROUPS
    hpg = gwid // HEAD_DIM
    nbp = sel.shape[3]
    assert t % tq == 0 and t % tk == 0
    return pl.pallas_call(
        functools.partial(_slc_prompt_body, tq=tq, tk=tk, hpg=hpg),
        grid=(b, NSA_KV_GROUPS, t // tq),
        in_specs=[
            pl.BlockSpec((None, tq, gwid), lambda i, g, j: (i, j, g)),
            pl.BlockSpec((None, t, HEAD_DIM), lambda i, g, j: (i, 0, 2 * NSA_KV_GROUPS + g)),
            pl.BlockSpec((None, t, HEAD_DIM), lambda i, g, j: (i, 0, 3 * NSA_KV_GROUPS + g)),
            pl.BlockSpec((None, None, tq, nbp), lambda i, g, j: (i, g, j, 0)),
        ],
        out_specs=pl.BlockSpec((None, tq, gwid), lambda i, g, j: (i, j, g)),
        out_shape=jax.ShapeDtypeStruct((b, t, qw), F32),
        scratch_shapes=[pltpu.VMEM((t, HEAD_DIM), BF16), pltpu.VMEM((t, HEAD_DIM), BF16),
                        pltpu.VMEM((hpg * tq, LANE), F32), pltpu.VMEM((hpg * tq, LANE), F32),
                        pltpu.VMEM((hpg * tq, HEAD_DIM), F32), pltpu.VMEM((hpg * tq, tk), BF16)],
        compiler_params=_cparams("parallel", "parallel", "arbitrary"),
        name="slc_attend_prompt",
    )(q, kv4, kv4, sel)


def _slc_paged_body(pt_ref, q_ref, sel_ref, new_ref, *refs, npg, tq, hpg, past_len, page):
    del pt_ref
    page_refs = refs[:npg]
    o_ref, m_ref, l_ref, acc_ref, e_ref = refs[npg:]
    s_id = pl.program_id(1)
    gwid = hpg * HEAD_DIM
    qpos = past_len + lax.broadcasted_iota(I32, (tq, 1), 0)
    k_chunk, v_chunk = 2 * NSA_KV_GROUPS, 3 * NSA_KV_GROUPS

    @pl.when(s_id == 0)
    def _():
        _flash_init(m_ref, l_ref, acc_ref)

    tk = npg * page
    k0 = s_id * tk
    kpos = k0 + lax.broadcasted_iota(I32, (1, tk), 1)
    for g in range(NSA_KV_GROUPS):
        q6 = _stack_heads(q_ref, g * gwid, hpg, SCALE_LOG2E)
        selb = sel_ref[g].astype(BF16)
        k = jnp.concatenate([pr[pl.ds(k_chunk + g, page, stride=PAGE_CHUNKS), :] for pr in page_refs],
                            axis=0).astype(BF16)
        v = jnp.concatenate([pr[pl.ds(v_chunk + g, page, stride=PAGE_CHUNKS), :] for pr in page_refs],
                            axis=0).astype(BF16)
        allowed = _block_mask(selb, k0, tk) & (kpos <= qpos)
        _flash_tile(q6, k, v, jnp.where(allowed, 0.0, NEG), m_ref.at[g], l_ref.at[g], acc_ref.at[g], e_ref, hpg)

    @pl.when(s_id == pl.num_programs(1) - 1)
    def _():
        cur = past_len // SEL_BLOCK
        nn = new_ref.shape[0]
        npos = past_len + lax.broadcasted_iota(I32, (1, LANE), 1)
        fill = jnp.zeros((LANE - nn, HEAD_DIM), F32)
        for g in range(NSA_KV_GROUPS):
            q6 = _stack_heads(q_ref, g * gwid, hpg, SCALE_LOG2E)
            k = jnp.concatenate([new_ref[:, g * HEAD_DIM:(g + 1) * HEAD_DIM], fill], axis=0).astype(BF16)
            v = jnp.concatenate([new_ref[:, (NSA_KV_GROUPS + g) * HEAD_DIM:(NSA_KV_GROUPS + g + 1) * HEAD_DIM],
                                 fill], axis=0).astype(BF16)
            allowed = (sel_ref[g][:, cur:cur + 1] > 0.5) & (npos <= qpos) & (npos < past_len + nn)
            _flash_tile(q6, k, v, jnp.where(allowed, 0.0, NEG), m_ref.at[g], l_ref.at[g], acc_ref.at[g],
                        e_ref.at[:, 0:LANE], hpg)
            o = acc_ref[g] / l_ref[g]
            for i in range(hpg):
                o_ref[:, g * gwid + i * HEAD_DIM:g * gwid + (i + 1) * HEAD_DIM] = o[i * tq:(i + 1) * tq]


def slc_attend_paged(q, sel, kv4_new, cache, li, page_table, npg):
    b, tq, qw = q.shape
    n_pages = page_table.shape[1]
    page = cache.shape[2] // PAGE_CHUNKS
    past_len = n_pages * page
    gwid = qw // NSA_KV_GROUPS
    hpg = gwid // HEAD_DIM
    nbp = sel.shape[3]
    cw = 2 * NSA_KV_GROUPS * HEAD_DIM
    assert n_pages % npg == 0 and past_len % SEL_BLOCK == 0 and tq <= SEL_BLOCK
    grid_spec = pltpu.PrefetchScalarGridSpec(
        num_scalar_prefetch=1,
        grid=(b, n_pages // npg),
        in_specs=[
            pl.BlockSpec((None, tq, qw), lambda i, s, pt: (i, 0, 0)),
            pl.BlockSpec((None, NSA_KV_GROUPS, tq, nbp), lambda i, s, pt: (i, 0, 0, 0)),
            pl.BlockSpec((None, tq, cw), lambda i, s, pt: (i, 0, 1)),
        ] + [_page_spec(li, k, npg, page) for k in range(npg)],
        out_specs=pl.BlockSpec((None, tq, qw), lambda i, s, pt: (i, 0, 0)),
        scratch_shapes=[pltpu.VMEM((NSA_KV_GROUPS, hpg * tq, LANE), F32),
                        pltpu.VMEM((NSA_KV_GROUPS, hpg * tq, LANE), F32),
                        pltpu.VMEM((NSA_KV_GROUPS, hpg * tq, HEAD_DIM), F32),
                        pltpu.VMEM((hpg * tq, npg * page), _prob_dtype(tq))],
    )
    return pl.pallas_call(
        functools.partial(_slc_paged_body, npg=npg, tq=tq, hpg=hpg, past_len=past_len, page=page),
        grid_spec=grid_spec,
        out_shape=jax.ShapeDtypeStruct((b, tq, qw), F32),
        compiler_params=_cparams("parallel", "arbitrary"),
        name="slc_attend_paged",
    )(page_table, q, sel, kv4_new, *([cache] * npg))


def _win_body(q_ref, k_ref, v_ref, o_ref, l_ref, e_ref, *, qb, nb, span, pos0, kpos0, hpg):
    for u in range(nb):
        blk = pl.program_id(2) * nb + u
        q6 = jnp.concatenate([q_ref[u * qb:(u + 1) * qb, i * HEAD_DIM:(i + 1) * HEAD_DIM] for i in range(hpg)],
                             axis=0)
        q6 = (q6 * SCALE_LOG2E).astype(BF16)
        start = pl.multiple_of(blk * qb, qb)
        k = k_ref[pl.ds(start, span), :].astype(BF16)
        v = v_ref[pl.ds(start, span), :].astype(BF16)
        kpos = kpos0 + blk * qb + lax.broadcasted_iota(I32, (1, span), 1)
        qpos = pos0 + blk * qb + lax.broadcasted_iota(I32, (qb, 1), 0)
        dist = qpos - kpos
        allowed = (kpos >= 0) & (dist >= 0) & (dist < WINDOW)
        bias = jnp.where(allowed, 0.0, NEG)
        s_all = _dot_nt(q6, k)
        for i in range(hpg):
            rows = slice(i * qb, (i + 1) * qb)
            s = s_all[rows] + bias
            e = jnp.exp2(s - jnp.max(s, axis=-1, keepdims=True))
            l_ref[u, rows, :] = jnp.broadcast_to(jnp.sum(e, axis=-1, keepdims=True), (qb, LANE))
            e_ref[u, rows, :] = e.astype(e_ref.dtype)
        o = _dot(e_ref[u].astype(BF16), v) / l_ref[u]
        for i in range(hpg):
            o_ref[u * qb:(u + 1) * qb, i * HEAD_DIM:(i + 1) * HEAD_DIM] = o[i * qb:(i + 1) * qb]


def win_attend(q, kext, qb, span, pos0, kpos0, nb=1):
    b, t, qw = q.shape
    lk = kext.shape[1]
    gwid = qw // NSA_KV_GROUPS
    hpg = gwid // HEAD_DIM
    assert t % (qb * nb) == 0 and (t // qb - 1) * qb + span <= lk and qb % SUBLANE == 0 and span % SUBLANE == 0
    return pl.pallas_call(
        functools.partial(_win_body, qb=qb, nb=nb, span=span, pos0=pos0, kpos0=kpos0, hpg=hpg),
        grid=(b, NSA_KV_GROUPS, t // (qb * nb)),
        in_specs=[
            pl.BlockSpec((None, qb * nb, gwid), lambda i, g, j: (i, j, g)),
            pl.BlockSpec((None, lk, HEAD_DIM), lambda i, g, j: (i, 0, g)),
            pl.BlockSpec((None, lk, HEAD_DIM), lambda i, g, j: (i, 0, NSA_KV_GROUPS + g)),
        ],
        out_specs=pl.BlockSpec((None, qb * nb, gwid), lambda i, g, j: (i, j, g)),
        out_shape=jax.ShapeDtypeStruct((b, t, qw), F32),
        scratch_shapes=[pltpu.VMEM((nb, hpg * qb, LANE), F32),
                        pltpu.VMEM((nb, hpg * qb, span), _prob_dtype(qb))],
        compiler_params=_cparams("parallel", "parallel", "parallel"),
        name="win_attend",
    )(q, kext, kext)


def _row_tile(n, pref):
    t = min(n, pref)
    assert n % t == 0
    return t


def kernel(x_prompt, x_sample, mem_prompt, cache_mem_kv, cache_nsa_kv, cache_nsa_win, state_pool, page_table, g_norm_mix, g_norm_mlp, g_norm_mem, w_mem_kv, mem_qk_gain, w_out, w_mlp_up, w_mlp_down, w_in_pool, w_pool_group, pool_scale, w_in_nsa, nsa_gate_bias, nsa_qk_gain, cmp_pos, cmp_w1, cmp_w2):
    depth = g_norm_mix.shape[0]
    bp, tp, d = x_prompt.shape
    db, ts, _ = x_sample.shape
    n_pages = page_table.shape[1]
    page = cache_nsa_kv.shape[2]
    past_len = n_pages * page
    win_buf = cache_nsa_win.shape[2]
    tok_w = w_pool_group.shape[1] * w_pool_group.shape[2]
    n_heads = tok_w // HEAD_DIM
    n_gate = 3 * n_heads
    kvw = 4 * NSA_KV_GROUPS * HEAD_DIM
    winw = 2 * NSA_KV_GROUPS * HEAD_DIM
    assert win_buf == WINDOW and past_len % SEL_BLOCK == 0 and ts <= SUBLANE and tp % QBLOCK == 0

    tsp = SUBLANE
    xs = jnp.pad(x_sample, ((0, 0), (0, tsp - ts), (0, 0))).reshape(db * tsp, d)
    xp = x_prompt.reshape(bp * tp, d)
    n_p, n_s = bp * tp, db * tsp
    tm_p, tm_s = _row_tile(n_p, 256), n_s
    mem2d = mem_prompt.reshape(bp * mem_prompt.shape[1], d)
    mem_len = mem_prompt.shape[1]
    cache_view = cache_nsa_kv.reshape(cache_nsa_kv.shape[0], cache_nsa_kv.shape[1], page * PAGE_CHUNKS, HEAD_DIM)
    npg = min(16, n_pages)

    mem_kv_p, nsa_kv_p, nsa_kv_s, win_p, win_s, pool_p, pool_s = [], [], [], [], [], [], []
    mem_segs = ((2 * MEM_WIDTH, (1,) * MEM_HEADS + (None,) * MEM_HEADS),)
    pool_segs = ((tok_w, None), (MEM_WIDTH, None))
    nsa_segs = ((tok_w, (0,) * n_heads),
                (kvw, (None, None, None, None, 2, 2, None, None)),
                (winw, (3, 3, None, None)),
                (MEM_WIDTH, None),
                (LANE, None))

    wo_b = w_out.astype(BF16)
    wu_b = w_mlp_up.astype(BF16)
    wd_b = w_mlp_down.astype(BF16)
    wmem_b = w_mem_kv.astype(BF16)
    wpool_b = w_in_pool.astype(BF16)
    wg_b = w_pool_group.astype(BF16)
    kv_end = tok_w + kvw + winw
    wnsa_b = jnp.concatenate(
        [w_in_nsa[:, :, :kv_end], w_in_nsa[:, :, kv_end + n_gate:], w_in_nsa[:, :, kv_end:kv_end + n_gate],
         jnp.zeros(w_in_nsa.shape[:2] + (LANE - n_gate,), w_in_nsa.dtype)], axis=2).astype(BF16)
    mem_cache = cache_mem_kv.reshape(cache_mem_kv.shape[0], db, cache_mem_kv.shape[2] * 2 * MEM_HEADS, HEAD_DIM)

    for i in range(depth):
        li = i // 2
        (mkv,) = in_proj(mem2d, g_norm_mem[i], wmem_b, i, mem_qk_gain[i], mem_segs,
                         _row_tile(mem2d.shape[0], 256))
        mkv_p = mkv.reshape(bp, mem_len, 2 * MEM_WIDTH)
        mem_kv_p.append(mkv_p.reshape(bp, mem_len, 2, MEM_HEADS, HEAD_DIM))
        gated = None
        if i % 2 == 0:
            gains = jnp.ones((1, HEAD_DIM), F32)
            up, qmp = in_proj(xp, g_norm_mix[i], wpool_b, li, gains, pool_segs, tm_p)
            us, qms = in_proj(xs, g_norm_mix[i], wpool_b, li, gains, pool_segs, tm_s)
            up3 = up.reshape(bp, tp, tok_w)
            tok_p = pool_mix(up3, wg_b, li, pool_scale[li], 0, _row_tile(tp, 512)).reshape(n_p, tok_w)
            us3 = us.reshape(db, tsp, tok_w)[:, :ts]
            lead = _round_up(POOL_HIST + ts, SUBLANE) - (POOL_HIST + ts)
            ext = jnp.concatenate([state_pool[li], us3], axis=1)
            ext_pad = jnp.pad(ext, ((0, 0), (lead, 0), (0, 0)))
            l_ext = ext_pad.shape[1]
            tok_e = pool_mix(ext_pad, wg_b, li, pool_scale[li], past_len - POOL_HIST - lead, l_ext)
            tok_s = jnp.pad(tok_e[:, l_ext - ts:], ((0, 0), (0, tsp - ts), (0, 0))).reshape(n_s, tok_w)
            pool_p.append(up3[:, tp - POOL_HIST:])
            pool_s.append(ext[:, -POOL_HIST:])
        else:
            gains = nsa_qk_gain[li]
            bias = jnp.pad(nsa_gate_bias[li], (0, LANE - n_gate)).reshape(1, LANE)
            w1 = cmp_w1[li]
            half = CMP_STRIDE * HEAD_DIM
            w1k = jnp.concatenate([w1[0, :half], w1[0, half:]], axis=1).astype(BF16)
            w1v = jnp.concatenate([w1[1, :half], w1[1, half:]], axis=1).astype(BF16)
            w1b = w1.astype(BF16)
            w2b = cmp_w2[li].astype(BF16)

            q_p, kv4_p, wn_p, qmp, gt_p = in_proj(xp, g_norm_mix[i], wnsa_b, li, gains, nsa_segs, tm_p)
            q3 = q_p.reshape(bp, tp, tok_w)
            kv43 = kv4_p.reshape(bp, tp, kvw)
            wn3 = wn_p.reshape(bp, tp, winw)
            n_cmp = tp // CMP_STRIDE
            ncp = _round_up(n_cmp, LANE)
            ab = cmp_stage1_dense(kv43, w1k, w1v)
            ct = cmp_stage2([ab], cmp_pos[li], w1b, w2b, gains[1], ncp)
            oc_p, sel_p = cmp_attend_topk(q3, ct, 0, tp // SEL_BLOCK, _row_tile(tp, 256))
            os_p = slc_attend_prompt(q3, kv43, sel_p, _row_tile(tp, 256), _row_tile(tp, 512))
            kext = jnp.concatenate([jnp.zeros((bp, WINDOW, winw), F32), wn3], axis=1)
            ow_p = win_attend(q3, kext, QBLOCK, WINDOW + QBLOCK, 0, -WINDOW, nb=2 if tp % (2 * QBLOCK) == 0 else 1)
            tok_p = (oc_p.reshape(n_p, tok_w), os_p.reshape(n_p, tok_w), ow_p.reshape(n_p, tok_w))

            q_s, kv4_s, wn_s, qms, gt_s = in_proj(xs, g_norm_mix[i], wnsa_b, li, gains, nsa_segs, tm_s)
            qs3 = q_s.reshape(db, tsp, tok_w)
            kv4s3 = kv4_s.reshape(db, tsp, kvw)
            wns3 = wn_s.reshape(db, tsp, winw)
            new_rows = jnp.pad(kv4s3[:, :ts], ((0, 0), (0, page - ts), (0, 0)))
            n_rows = _round_up(past_len + ts, SEL_BLOCK)
            n_cmp_s = n_rows // CMP_STRIDE
            ncp_s = _round_up(n_cmp_s, LANE)
            ab_past = cmp_stage1_paged(cache_view, li, page_table, w1k, w1v, npg)
            ab_new = cmp_stage1_dense(new_rows, w1k, w1v)
            ct_s = cmp_stage2([ab_past, ab_new], cmp_pos[li], w1b, w2b, gains[1], ncp_s)
            oc_s, sel_s = cmp_attend_topk(qs3, ct_s, past_len, n_rows // SEL_BLOCK, tsp)
            os_s = slc_attend_paged(qs3, sel_s, kv4s3, cache_view, li, page_table, npg)
            wext = jnp.concatenate([cache_nsa_win[li].reshape(db, win_buf, winw), wns3[:, :ts]], axis=1)
            kext_s = jnp.pad(wext, ((0, 0), (0, tsp - ts), (0, 0)))
            ow_s = win_attend(qs3, kext_s, tsp, win_buf + tsp, past_len, past_len - win_buf)
            tok_s = (oc_s.reshape(n_s, tok_w), os_s.reshape(n_s, tok_w), ow_s.reshape(n_s, tok_w))
            gated = ((gt_p, bias), (gt_s, bias))

            nsa_kv_p.append(kv43.reshape(bp, tp, 4, NSA_KV_GROUPS, HEAD_DIM))
            nsa_kv_s.append(kv4s3[:, :ts].reshape(db, ts, 4, NSA_KV_GROUPS, HEAD_DIM))
            wlen = min(WINDOW, tp)
            win_p.append(wn3[:, tp - wlen:].reshape(bp, wlen, 2, NSA_KV_GROUPS, HEAD_DIM))
            win_s.append(wext[:, -win_buf:].reshape(db, win_buf, 2, NSA_KV_GROUPS, HEAD_DIM))

        hp = out_proj(tok_p, qmp, mkv_p, mem_qk_gain[i, 0], xp, wo_b, i, tm_p, tp, gates=gated and gated[0])
        hs = out_proj(tok_s, qms, mem_cache, mem_qk_gain[i, 0], xs, wo_b, i, tm_s, tsp, gates=gated and gated[1],
                      kv_layer=i)
        xp = mlp(hp, g_norm_mlp[i], wu_b, wd_b, i, _row_tile(n_p, 512), 1024)
        xs = mlp(hs, g_norm_mlp[i], wu_b, wd_b, i, tm_s, 1024)

    y_p = xp.reshape(bp, tp, d)
    y_s = xs.reshape(db, tsp, d)[:, :ts]
    return (y_p, y_s, jnp.stack(mem_kv_p), jnp.stack(nsa_kv_p), jnp.stack(nsa_kv_s),
            jnp.stack(win_p), jnp.stack(win_s), jnp.stack(pool_p), jnp.stack(pool_s))
```

```python
import functools

import jax
import jax.numpy as jnp
from jax import lax
from jax.experimental import pallas as pl
from jax.experimental.pallas import tpu as pltpu

F32 = jnp.float32
BF16 = jnp.bfloat16
I32 = jnp.int32

HEAD_DIM = 128
MEM_HEADS = 4
MEM_WIDTH = MEM_HEADS * HEAD_DIM
NSA_KV_GROUPS = 2
CMP_STRIDE = 16
CMP_BLOCK = 2 * CMP_STRIDE
SEL_BLOCK = 64
N_SEL = 16
WINDOW = 512
QBLOCK = 128
POOL_WINDOWS = (2, 4, 8, 16)
POOL_HIST = max(POOL_WINDOWS) - 1
POOL_HALO = 16
ATTN_SCALE = HEAD_DIM ** -0.5
SCALE_LOG2E = ATTN_SCALE * 1.4426950408889634
EPS = 1e-6
NEG = -1e30
BIG = 1e30
LOWEST = -3.0e38
MASK_OFF = 2.0 ** 20
SEL_SHIFT = SEL_BLOCK.bit_length() - 1
RATIO_SHIFT = (SEL_BLOCK // CMP_STRIDE).bit_length() - 1
assert 1 << SEL_SHIFT == SEL_BLOCK and 1 << RATIO_SHIFT == SEL_BLOCK // CMP_STRIDE

LANE = 128
SUBLANE = 8
VMEM_LIMIT_BYTES = 56 * 1024 * 1024


def _cparams(*sem):
    return pltpu.CompilerParams(dimension_semantics=sem, vmem_limit_bytes=VMEM_LIMIT_BYTES)


def _round_up(x, m):
    return (x + m - 1) // m * m


def _rms(x, gain):
    return x * lax.rsqrt(jnp.mean(x * x, axis=-1, keepdims=True) + EPS) * gain


def _dot(a, b):
    return jnp.dot(a, b, preferred_element_type=F32)


def _dot_nt(a, b):
    return lax.dot_general(a, b, (((1,), (1,)), ((), ())), preferred_element_type=F32)


def _in_proj_body(x_ref, g_ref, w_ref, gains_ref, *out_refs, segs):
    xb = _rms(x_ref[...], g_ref[...]).astype(BF16)
    col = 0
    for o_ref, (width, norms) in zip(out_refs, segs):
        y = _dot(xb, w_ref[:, col:col + width])
        if norms is None:
            o_ref[...] = y
        else:
            for c, gi in enumerate(norms):
                yc = y[:, c * LANE:(c + 1) * LANE]
                if gi is not None:
                    yc = _rms(yc, gains_ref[gi:gi + 1, :])
                o_ref[:, c * LANE:(c + 1) * LANE] = yc
        col += width


def in_proj(x2d, g, w, layer, gains, segs, tm):
    n, d = x2d.shape
    wtot = w.shape[2]
    assert n % tm == 0 and wtot == sum(s[0] for s in segs)
    return pl.pallas_call(
        functools.partial(_in_proj_body, segs=segs),
        grid=(n // tm,),
        in_specs=[
            pl.BlockSpec((tm, d), lambda i: (i, 0)),
            pl.BlockSpec((1, d), lambda i: (0, 0)),
            pl.BlockSpec((None, d, wtot), lambda i: (layer, 0, 0)),
            pl.BlockSpec(gains.shape, lambda i: (0, 0)),
        ],
        out_specs=[pl.BlockSpec((tm, s[0]), lambda i: (i, 0)) for s in segs],
        out_shape=[jax.ShapeDtypeStruct((n, s[0]), F32) for s in segs],
        compiler_params=_cparams("parallel"),
        name="in_proj",
    )(x2d, g.reshape(1, d), w, gains)


def _mem_attend(qm, kv_ref, gq_ref, row_major):
    chunks = 2 * MEM_HEADS
    m = kv_ref.shape[0] // chunks if row_major else kv_ref.shape[0]
    heads = []
    for h in range(MEM_HEADS):
        sl = slice(h * HEAD_DIM, (h + 1) * HEAD_DIM)
        q = _rms(qm[:, sl], gq_ref[...]).astype(BF16)
        if row_major:
            k = kv_ref[pl.ds(h, m, stride=chunks), :].astype(BF16)
            v = kv_ref[pl.ds(MEM_HEADS + h, m, stride=chunks), :].astype(BF16)
        else:
            k = kv_ref[:, sl].astype(BF16)
            v = kv_ref[:, MEM_WIDTH + h * HEAD_DIM:MEM_WIDTH + (h + 1) * HEAD_DIM].astype(BF16)
        s = _dot_nt(q, k) * ATTN_SCALE
        e = jnp.exp(s - jnp.max(s, axis=-1, keepdims=True))
        heads.append(_dot(e.astype(BF16), v) / jnp.sum(e, axis=-1, keepdims=True))
    return jnp.concatenate(heads, axis=1)


def _out_proj_body(*refs, gated, row_major, rows_per_batch):
    if gated:
        gt_ref, bias_ref, oc_ref, os_ref, ow_ref = refs[:5]
        refs = refs[5:]
        gs = jax.nn.sigmoid(gt_ref[...] + bias_ref[...])
        heads = []
        for h in range(oc_ref.shape[1] // HEAD_DIM):
            sl = slice(h * HEAD_DIM, (h + 1) * HEAD_DIM)
            heads.append(gs[:, 3 * h:3 * h + 1] * oc_ref[:, sl] + gs[:, 3 * h + 1:3 * h + 2] * os_ref[:, sl]
                         + gs[:, 3 * h + 2:3 * h + 3] * ow_ref[:, sl])
        tok = jnp.concatenate(heads, axis=1)
    else:
        tok = refs[0][...]
        refs = refs[1:]
    qm_ref, kv_ref, gq_ref, x_ref, w_ref, o_ref = refs
    tw = tok.shape[1]
    acc = _dot(tok.astype(BF16), w_ref[0:tw, :])
    if len(kv_ref.shape) == 2:
        mem = _mem_attend(qm_ref[...], kv_ref, gq_ref, row_major)
    else:
        mem = jnp.concatenate([_mem_attend(qm_ref[b * rows_per_batch:(b + 1) * rows_per_batch, :], kv_ref.at[b],
                                           gq_ref, row_major) for b in range(kv_ref.shape[0])], axis=0)
    acc = acc + _dot(mem.astype(BF16), w_ref[tw:, :])
    o_ref[...] = x_ref[...] + acc


def out_proj(tok, qm, mkv, gq, x, w, layer, tm, rows_per_batch, gates=None, kv_layer=None):
    n, d = x.shape
    toks = tok if gates is not None else (tok,)
    tw = toks[0].shape[1]
    assert rows_per_batch % tm == 0 or (tm == n and tm % rows_per_batch == 0 and rows_per_batch % SUBLANE == 0)
    tiles = max(rows_per_batch // tm, 1)
    one = rows_per_batch % tm == 0
    row = lambda width: pl.BlockSpec((tm, width), lambda i: (i, 0))
    lead_specs, lead_args = [], []
    if gates is not None:
        lead_specs = [row(LANE), pl.BlockSpec((1, LANE), lambda i: (0, 0))]
        lead_args = list(gates)
    if kv_layer is None:
        kv_spec = (pl.BlockSpec((None,) + mkv.shape[1:], lambda i: (i // tiles, 0, 0)) if one else
                   pl.BlockSpec(mkv.shape, lambda i: (0, 0, 0)))
    else:
        kv_spec = (pl.BlockSpec((None, None) + mkv.shape[2:], lambda i: (kv_layer, i // tiles, 0, 0)) if one else
                   pl.BlockSpec((None,) + mkv.shape[1:], lambda i: (kv_layer, 0, 0, 0)))
    return pl.pallas_call(
        functools.partial(_out_proj_body, gated=gates is not None, row_major=kv_layer is not None,
                          rows_per_batch=rows_per_batch),
        grid=(n // tm,),
        in_specs=lead_specs + [row(tw) for _ in toks] + [
            row(MEM_WIDTH), kv_spec, pl.BlockSpec((1, HEAD_DIM), lambda i: (0, 0)), row(d),
            pl.BlockSpec((None, tw + MEM_WIDTH, d), lambda i: (layer, 0, 0)),
        ],
        out_specs=row(d),
        out_shape=jax.ShapeDtypeStruct((n, d), F32),
        compiler_params=_cparams("parallel"),
        name="out_proj",
    )(*lead_args, *toks, qm, mkv, gq.reshape(1, HEAD_DIM), x, w)


def _mlp_body(h_ref, g_ref, wu_ref, wd_ref, o_ref, xn_ref, acc_ref):
    k = pl.program_id(1)

    @pl.when(k == 0)
    def _():
        xn_ref[...] = _rms(h_ref[...], g_ref[...]).astype(BF16)
        acc_ref[...] = jnp.zeros_like(acc_ref)

    a = jnp.maximum(_dot(xn_ref[...], wu_ref[...]), 0.0)
    acc_ref[...] += _dot((a * a).astype(BF16), wd_ref[...])

    @pl.when(k == pl.num_programs(1) - 1)
    def _():
        o_ref[...] = h_ref[...] + acc_ref[...]


def mlp(h, g, w_up, w_down, layer, tm, tf):
    n, d = h.shape
    ff = w_up.shape[2]
    return pl.pallas_call(
        _mlp_body,
        grid=(n // tm, ff // tf),
        in_specs=[
            pl.BlockSpec((tm, d), lambda i, k: (i, 0)),
            pl.BlockSpec((1, d), lambda i, k: (0, 0)),
            pl.BlockSpec((None, d, tf), lambda i, k: (layer, 0, k)),
            pl.BlockSpec((None, tf, d), lambda i, k: (layer, k, 0)),
        ],
        out_specs=pl.BlockSpec((tm, d), lambda i, k: (i, 0)),
        out_shape=jax.ShapeDtypeStruct((n, d), F32),
        scratch_shapes=[pltpu.VMEM((tm, d), BF16), pltpu.VMEM((tm, d), F32)],
        compiler_params=_cparams("parallel", "arbitrary"),
        name="mlp",
    )(h, g.reshape(1, d), w_up, w_down)


def _pool_body(cur_ref, halo_ref, wg_ref, sc_ref, o_ref, ext_ref, *, pos0, tt, gw):
    t = pl.program_id(1)
    ext_ref[0:POOL_HALO, :] = jnp.where(t == 0, 0.0, halo_ref[...])
    ext_ref[POOL_HALO:POOL_HALO + tt, :] = cur_ref[...]
    row = t * tt + lax.broadcasted_iota(I32, (tt, 1), 0)
    for gi, w in enumerate(POOL_WINDOWS):
        c0, c1 = gi * gw, (gi + 1) * gw
        x = ext_ref[POOL_HALO:POOL_HALO + tt, c0:c1]
        s = x
        for dd in range(1, w):
            s = s + ext_ref[POOL_HALO - dd:POOL_HALO - dd + tt, c0:c1]
        cnt = jnp.minimum(pos0 + row + 1, w).astype(F32)
        pooled = s / cnt - x
        o_ref[:, c0:c1] = _dot(pooled.astype(BF16), wg_ref[gi]) * sc_ref[:, c0:c1]


def pool_mix(u, w_group, layer, scale, pos0, tt):
    b, l, c = u.shape
    _, ng, gw, _ = w_group.shape
    assert l % tt == 0 and tt % POOL_HALO == 0 or l == tt
    halo_per_tile = tt // POOL_HALO if tt % POOL_HALO == 0 else 0
    return pl.pallas_call(
        functools.partial(_pool_body, pos0=pos0, tt=tt, gw=gw),
        grid=(b, l // tt),
        in_specs=[
            pl.BlockSpec((None, tt, c), lambda i, j: (i, j, 0)),
            pl.BlockSpec((None, POOL_HALO, c), lambda i, j: (i, jnp.maximum(j * halo_per_tile - 1, 0), 0)),
            pl.BlockSpec((None, ng, gw, gw), lambda i, j: (layer, 0, 0, 0)),
            pl.BlockSpec((1, c), lambda i, j: (0, 0)),
        ],
        out_specs=pl.BlockSpec((None, tt, c), lambda i, j: (i, j, 0)),
        out_shape=jax.ShapeDtypeStruct((b, l, c), F32),
        scratch_shapes=[pltpu.VMEM((POOL_HALO + tt, c), F32)],
        compiler_params=_cparams("parallel", "parallel"),
        name="pool_mix",
    )(u, u, w_group, scale.reshape(1, c))


def _cmp1_compute(page_refs, wk_ref, wv_ref, o_ref, stage_ref, rows):
    m = len(page_refs) * rows // CMP_STRIDE
    for slot, w_ref in ((0, wk_ref), (1, wv_ref)):
        xs = []
        for g in range(NSA_KV_GROUPS):
            j = slot * NSA_KV_GROUPS + g
            for p, pr in enumerate(page_refs):
                stage_ref[p * rows:(p + 1) * rows, :] = pr[:, j * HEAD_DIM:(j + 1) * HEAD_DIM]
            pieces = [stage_ref[pl.ds(r, m, stride=CMP_STRIDE), :] for r in range(CMP_STRIDE)]
            xs.append(jnp.concatenate(pieces, axis=1).astype(BF16))
        y = _dot(jnp.concatenate(xs, axis=0), w_ref[...])
        for g in range(NSA_KV_GROUPS):
            j = slot * NSA_KV_GROUPS + g
            o_ref[:, j * 2 * HEAD_DIM:(j + 1) * 2 * HEAD_DIM] = y[g * m:(g + 1) * m]


def _cmp1_dense_body(x_ref, wk_ref, wv_ref, o_ref, stage_ref, *, rows):
    _cmp1_compute([x_ref], wk_ref, wv_ref, o_ref, stage_ref, rows)


def cmp_stage1_dense(rows4, w1k, w1v, lane_block=0):
    b, l, _ = rows4.shape
    nh = l // CMP_STRIDE
    cw = 2 * NSA_KV_GROUPS * HEAD_DIM
    return pl.pallas_call(
        functools.partial(_cmp1_dense_body, rows=l),
        grid=(b,),
        in_specs=[
            pl.BlockSpec((None, l, cw), lambda i: (i, 0, lane_block)),
            pl.BlockSpec(w1k.shape, lambda i: (0, 0)),
            pl.BlockSpec(w1v.shape, lambda i: (0, 0)),
        ],
        out_specs=pl.BlockSpec((None, nh, 2 * cw), lambda i: (i, 0, 0)),
        out_shape=jax.ShapeDtypeStruct((b, nh, 2 * cw), F32),
        scratch_shapes=[pltpu.VMEM((l, HEAD_DIM), F32)],
        compiler_params=_cparams("parallel"),
        name="cmp_stage1_dense",
    )(rows4, w1k, w1v)


PAGE_CHUNKS = 4 * NSA_KV_GROUPS


def _cmp1_paged_body(pt_ref, *refs, npg, page):
    del pt_ref
    page_refs = refs[:npg]
    wk_ref, wv_ref, o_ref = refs[npg:npg + 3]
    nh = page // CMP_STRIDE
    m = npg * nh
    for slot, w_ref in ((0, wk_ref), (1, wv_ref)):
        xs = []
        for g in range(NSA_KV_GROUPS):
            j = slot * NSA_KV_GROUPS + g
            per_page = []
            for pr in page_refs:
                pieces = [pr[pl.ds(r * PAGE_CHUNKS + j, nh, stride=CMP_STRIDE * PAGE_CHUNKS), :]
                          for r in range(CMP_STRIDE)]
                per_page.append(jnp.concatenate(pieces, axis=1))
            xs.append(jnp.concatenate(per_page, axis=0).astype(BF16))
        y = _dot(jnp.concatenate(xs, axis=0), w_ref[...])
        for g in range(NSA_KV_GROUPS):
            j = slot * NSA_KV_GROUPS + g
            o_ref[:, j * 2 * HEAD_DIM:(j + 1) * 2 * HEAD_DIM] = y[g * m:(g + 1) * m]


def _page_spec(li, k, npg, page):
    return pl.BlockSpec((None, None, page * PAGE_CHUNKS, HEAD_DIM),
                        lambda i, s, pt: (li, pt[i, s * npg + k], 0, 0))


def cmp_stage1_paged(cache, li, page_table, w1k, w1v, npg):
    b, n_pages = page_table.shape
    page = cache.shape[2] // PAGE_CHUNKS
    nh = page // CMP_STRIDE
    cw = 2 * NSA_KV_GROUPS * HEAD_DIM
    assert n_pages % npg == 0
    grid_spec = pltpu.PrefetchScalarGridSpec(
        num_scalar_prefetch=1,
        grid=(b, n_pages // npg),
        in_specs=[_page_spec(li, k, npg, page) for k in range(npg)] + [
            pl.BlockSpec(w1k.shape, lambda i, s, pt: (0, 0)),
            pl.BlockSpec(w1v.shape, lambda i, s, pt: (0, 0)),
        ],
        out_specs=pl.BlockSpec((None, npg * nh, 2 * cw), lambda i, s, pt: (i, s, 0)),
    )
    return pl.pallas_call(
        functools.partial(_cmp1_paged_body, npg=npg, page=page),
        grid_spec=grid_spec,
        out_shape=jax.ShapeDtypeStruct((b, n_pages * nh, 2 * cw), F32),
        compiler_params=_cparams("parallel", "arbitrary"),
        name="cmp_stage1_paged",
    )(page_table, *([cache] * npg), w1k, w1v)


def _cmp2_body(*refs, n_ab, ncp):
    ab_refs = refs[:n_ab]
    pos_ref, w1_ref, w2_ref, gk_ref, o_ref = refs[n_ab:]
    n_have = sum(r.shape[0] for r in ab_refs)
    fill = [jnp.zeros((ncp + SUBLANE - n_have, HEAD_DIM), F32)]
    for slot in range(2):
        posflat = jnp.concatenate([pos_ref[slot, r:r + 1, :] for r in range(CMP_BLOCK)], axis=1)
        posb = jnp.broadcast_to(posflat, (SUBLANE, CMP_BLOCK * HEAD_DIM)).astype(BF16)
        posc = _dot(posb, w1_ref[slot])[0:1]
        for g in range(NSA_KV_GROUPS):
            j = slot * NSA_KV_GROUPS + g
            a_lanes = slice(j * 2 * HEAD_DIM, j * 2 * HEAD_DIM + HEAD_DIM)
            b_lanes = slice(j * 2 * HEAD_DIM + HEAD_DIM, (j + 1) * 2 * HEAD_DIM)
            a = jnp.concatenate([r[:, a_lanes] for r in ab_refs] + fill, axis=0)
            bm = jnp.concatenate([r[:, b_lanes] for r in ab_refs] + fill, axis=0)
            bm_next = pltpu.roll(bm, ncp + SUBLANE - 1, axis=0)
            hid = jax.nn.gelu(a[0:ncp] + bm_next[0:ncp] + posc)
            out = _dot(hid.astype(BF16), w2_ref[slot])
            if slot == 0:
                out = _rms(out, gk_ref[...])
            o_ref[:, j * HEAD_DIM:(j + 1) * HEAD_DIM] = out


def cmp_stage2(abs_, pos, w1, w2, gk, ncp):
    b, _, w = abs_[0].shape
    assert all(a.shape[1] % SUBLANE == 0 for a in abs_) and sum(a.shape[1] for a in abs_) <= ncp + SUBLANE
    cw = 2 * NSA_KV_GROUPS * HEAD_DIM
    return pl.pallas_call(
        functools.partial(_cmp2_body, n_ab=len(abs_), ncp=ncp),
        grid=(b,),
        in_specs=[pl.BlockSpec((None, a.shape[1], w), lambda i: (i, 0, 0)) for a in abs_] + [
            pl.BlockSpec(pos.shape, lambda i: (0, 0, 0)),
            pl.BlockSpec(w1.shape, lambda i: (0, 0, 0)),
            pl.BlockSpec(w2.shape, lambda i: (0, 0, 0)),
            pl.BlockSpec((1, HEAD_DIM), lambda i: (0, 0)),
        ],
        out_specs=pl.BlockSpec((None, ncp, cw), lambda i: (i, 0, 0)),
        out_shape=jax.ShapeDtypeStruct((b, ncp, cw), F32),
        compiler_params=_cparams("parallel"),
        name="cmp_stage2",
    )(*abs_, pos, w1, w2, gk.reshape(1, HEAD_DIM))


def _cmp_attn_body(q_ref, ct_ref, o_ref, sel_ref, st_ref, *, pos0, tq, ncp, n_blk, n_sel, hpg):
    t = pl.program_id(1)
    ng = NSA_KV_GROUPS
    gwid = hpg * HEAD_DIM
    ratio = SEL_BLOCK // CMP_STRIDE
    nbp = ncp // ratio
    q0 = pos0 + t * tq
    qpos = q0 + lax.broadcasted_iota(I32, (tq, 1), 0)
    cidx = lax.broadcasted_iota(I32, (1, ncp), 1)
    cmask = (cidx * CMP_STRIDE + (CMP_BLOCK - 1)) <= qpos
    bias = jnp.where(cmask, 0.0, NEG)
    rp = _round_up(tq, LANE)
    nbpp = sel_ref.shape[2]
    jb = lax.broadcasted_iota(I32, (nbp, 1), 0)
    jf = jb.astype(F32)
    cur = (q0 + lax.broadcasted_iota(I32, (1, rp), 1)) >> SEL_SHIFT
    forced = (jb == 0) | (jb == cur) | (jb == cur - 1)
    for g in range(ng):
        q6 = _stack_heads(q_ref, g * gwid, hpg, ATTN_SCALE)
        kc = ct_ref[:, g * HEAD_DIM:(g + 1) * HEAD_DIM].astype(BF16)
        vc = ct_ref[:, (ng + g) * HEAD_DIM:(ng + g + 1) * HEAD_DIM].astype(BF16)
        s_all = _dot_nt(q6, kc)
        imp = jnp.zeros((tq, ncp), F32)
        probs = []
        for i in range(hpg):
            s = s_all[i * tq:(i + 1) * tq] + bias
            e = jnp.exp(s - jnp.max(s, axis=-1, keepdims=True))
            p = jnp.where(cmask, e / jnp.sum(e, axis=-1, keepdims=True), 0.0)
            imp = imp + p
            probs.append(p)
        o = _dot(jnp.concatenate(probs, axis=0).astype(BF16), vc)
        for i in range(hpg):
            o_ref[:, g * gwid + i * HEAD_DIM:g * gwid + (i + 1) * HEAD_DIM] = o[i * tq:(i + 1) * tq]
        imp2 = imp + jnp.where(cidx == 0, 0.0, pltpu.roll(imp, 1, axis=1))
        t1 = imp2 + pltpu.roll(imp2, ncp - 1, axis=1)
        ps = t1 + pltpu.roll(t1, ncp - 2, axis=1)

        if rp > tq:
            ps = jnp.concatenate([ps, jnp.zeros((rp - tq, ncp), F32)], axis=0)
        ps_t = ps.T
        tiles = []
        for h in range(rp // LANE):
            st = st_ref.at[g * (rp // LANE) + h]
            st[...] = ps_t[:, h * LANE:(h + 1) * LANE]
            tiles.append(st[pl.ds(0, nbp, stride=ratio), :])
        ps_blk = jnp.concatenate(tiles, axis=1)
        score = jnp.where(forced, BIG, jnp.where(jb <= cur, ps_blk, NEG))
        score = jnp.where(jb < n_blk, score, LOWEST)
        sel_t = jnp.zeros((nbp, rp), F32)
        for _ in range(n_sel):
            m = jnp.max(score, axis=0, keepdims=True)
            first = jnp.min(jnp.where(score == m, jf, float(nbp)), axis=0, keepdims=True)
            pick = jf == first
            sel_t = jnp.where(pick & (m > 0.5 * NEG), 1.0, sel_t)
            score = jnp.where(pick, LOWEST, score)
        if nbpp > nbp:
            sel_t = jnp.concatenate([sel_t, jnp.zeros((nbpp - nbp, rp), F32)], axis=0)
        sel_ref[g] = sel_t.T[0:tq]


def cmp_attend_topk(q, ct, pos0, n_blk, tq):
    b, t, qw = q.shape
    ncp = ct.shape[1]
    gwid = qw // NSA_KV_GROUPS
    hpg = gwid // HEAD_DIM
    ratio = SEL_BLOCK // CMP_STRIDE
    assert ncp % LANE == 0 and t % tq == 0 and tq & (tq - 1) == 0 and tq % SUBLANE == 0
    nbpp = _round_up(ncp // ratio, LANE)
    n_sel = min(N_SEL, n_blk)
    return pl.pallas_call(
        functools.partial(_cmp_attn_body, pos0=pos0, tq=tq, ncp=ncp, n_blk=n_blk, n_sel=n_sel, hpg=hpg),
        grid=(b, t // tq),
        in_specs=[
            pl.BlockSpec((None, tq, qw), lambda i, j: (i, j, 0)),
            pl.BlockSpec((None, ncp, ct.shape[2]), lambda i, j: (i, 0, 0)),
        ],
        out_specs=[
            pl.BlockSpec((None, tq, qw), lambda i, j: (i, j, 0)),
            pl.BlockSpec((None, NSA_KV_GROUPS, tq, nbpp), lambda i, j: (i, 0, j, 0)),
        ],
        out_shape=[
            jax.ShapeDtypeStruct((b, t, qw), F32),
            jax.ShapeDtypeStruct((b, NSA_KV_GROUPS, t, nbpp), F32),
        ],
        scratch_shapes=[pltpu.VMEM((NSA_KV_GROUPS * (_round_up(tq, LANE) // LANE), ncp, LANE), F32)],
        compiler_params=_cparams("parallel", "parallel"),
        name="cmp_attend_topk",
    )(q, ct)


def _stack_heads(q_ref, lane0, hpg, scale=None):
    q = jnp.concatenate([q_ref[:, lane0 + i * HEAD_DIM:lane0 + (i + 1) * HEAD_DIM] for i in range(hpg)], axis=0)
    return (q if scale is None else q * scale).astype(BF16)


def _prob_dtype(rows_per_head):
    return BF16 if rows_per_head % (2 * SUBLANE) == 0 else F32


def _flash_init(m_ref, l_ref, acc_ref):
    m_ref[...] = jnp.full_like(m_ref, NEG)
    l_ref[...] = jnp.zeros_like(l_ref)
    acc_ref[...] = jnp.zeros_like(acc_ref)


def _flash_tile(q6, k, v, bias, m_ref, l_ref, acc_ref, e_ref, hpg):
    tk = k.shape[0]
    tq = q6.shape[0] // hpg
    assert tk % LANE == 0 and acc_ref.shape[1] == LANE
    s_all = _dot_nt(q6, k)
    for i in range(hpg):
        rows = slice(i * tq, (i + 1) * tq)
        s = s_all[rows] if bias is None else s_all[rows] + bias
        m_old = m_ref[rows, :]
        m_new = jnp.maximum(m_old, jnp.max(s, axis=-1, keepdims=True))
        e = jnp.exp2(s - jnp.concatenate([m_new] * (tk // LANE), axis=1))
        alpha = jnp.exp2(m_old - m_new)
        l_ref[rows, :] = alpha * l_ref[rows, :] + jnp.sum(e, axis=-1, keepdims=True)
        acc_ref[rows, :] = alpha * acc_ref[rows, :]
        m_ref[rows, :] = m_new
        e_ref[rows, :] = e.astype(e_ref.dtype)
    acc_ref[...] += _dot(e_ref[...].astype(BF16), v)


def _block_mask(selb, k0, tk):
    nbp = selb.shape[1]
    blk = (k0 + lax.broadcasted_iota(I32, (nbp, tk), 1)) >> SEL_SHIFT
    expand = (lax.broadcasted_iota(I32, (nbp, tk), 0) == blk).astype(BF16)
    return _dot(selb, expand) > 0.5


def _slc_prompt_body(q_ref, k_ref, v_ref, sel_ref, o_ref, kb_ref, vb_ref, m_ref, l_ref, acc_ref, e_ref,
                     *, tq, tk, hpg):
    t = pl.program_id(2)

    n_keys = k_ref.shape[0]
    nbp = sel_ref.shape[1]

    @pl.when(t == 0)
    def _():
        kb_ref[:, 0:HEAD_DIM] = k_ref[...].astype(BF16)
        key_blk = lax.broadcasted_iota(I32, (n_keys, nbp), 0) >> SEL_SHIFT
        kb_ref[:, HEAD_DIM:] = (key_blk == lax.broadcasted_iota(I32, (n_keys, nbp), 1)).astype(BF16)
        vb_ref[...] = v_ref[...].astype(BF16)

    off = ((sel_ref[...] - 1.0) * MASK_OFF).astype(BF16)
    q6 = jnp.concatenate([_stack_heads(q_ref, 0, hpg, SCALE_LOG2E), jnp.concatenate([off] * hpg, axis=0)], axis=1)
    qpos = t * tq + lax.broadcasted_iota(I32, (tq, 1), 0)
    _flash_init(m_ref, l_ref, acc_ref)

    def tile(kt, causal):
        k0 = pl.multiple_of(kt * tk, tk)
        bias = None
        if causal:
            kpos = k0 + lax.broadcasted_iota(I32, (1, tk), 1)
            bias = jnp.where(kpos <= qpos, 0.0, NEG)
        _flash_tile(q6, kb_ref[pl.ds(k0, tk), :], vb_ref[pl.ds(k0, tk), :], bias, m_ref, l_ref, acc_ref, e_ref, hpg)

    def below_diagonal(kt, carry):
        tile(kt, False)
        return carry

    def on_diagonal(kt, carry):
        tile(kt, True)
        return carry

    n_below = (t * tq + 1) // tk
    lax.fori_loop(0, n_below, below_diagonal, 0)
    lax.fori_loop(n_below, (t * tq + tq - 1) // tk + 1, on_diagonal, 0)
    o = acc_ref[...] / l_ref[...]
    for i in range(hpg):
        o_ref[:, i * HEAD_DIM:(i + 1) * HEAD_DIM] = o[i * tq:(i + 1) * tq]


def slc_attend_prompt(q, kv4, sel, tq, tk):
    b, t, qw = q.shape
    gwid = qw // NSA_KV_GROUPS
    hpg = gwid // HEAD_DIM
    nbp = sel.shape[3]
    assert t % tq == 0 and t % tk == 0
    return pl.pallas_call(
        functools.partial(_slc_prompt_body, tq=tq, tk=tk, hpg=hpg),
        grid=(b, NSA_KV_GROUPS, t // tq),
        in_specs=[
            pl.BlockSpec((None, tq, gwid), lambda i, g, j: (i, j, g)),
            pl.BlockSpec((None, t, HEAD_DIM), lambda i, g, j: (i, 0, 2 * NSA_KV_GROUPS + g)),
            pl.BlockSpec((None, t, HEAD_DIM), lambda i, g, j: (i, 0, 3 * NSA_KV_GROUPS + g)),
            pl.BlockSpec((None, None, tq, nbp), lambda i, g, j: (i, g, j, 0)),
        ],
        out_specs=pl.BlockSpec((None, tq, gwid), lambda i, g, j: (i, j, g)),
        out_shape=jax.ShapeDtypeStruct((b, t, qw), F32),
        scratch_shapes=[pltpu.VMEM((t, HEAD_DIM + nbp), BF16), pltpu.VMEM((t, HEAD_DIM), BF16),
                        pltpu.VMEM((hpg * tq, LANE), F32), pltpu.VMEM((hpg * tq, LANE), F32),
                        pltpu.VMEM((hpg * tq, HEAD_DIM), F32), pltpu.VMEM((hpg * tq, tk), BF16)],
        compiler_params=_cparams("parallel", "parallel", "arbitrary"),
        name="slc_attend_prompt",
    )(q, kv4, kv4, sel)


def _slc_paged_body(pt_ref, q_ref, sel_ref, new_ref, *refs, npg, tq, hpg, past_len, page):
    del pt_ref
    page_refs = refs[:npg]
    o_ref, m_ref, l_ref, acc_ref, e_ref = refs[npg:]
    s_id = pl.program_id(1)
    gwid = hpg * HEAD_DIM
    qpos = past_len + lax.broadcasted_iota(I32, (tq, 1), 0)
    k_chunk, v_chunk = 2 * NSA_KV_GROUPS, 3 * NSA_KV_GROUPS

    @pl.when(s_id == 0)
    def _():
        _flash_init(m_ref, l_ref, acc_ref)

    tk = npg * page
    k0 = s_id * tk
    kpos = k0 + lax.broadcasted_iota(I32, (1, tk), 1)
    for g in range(NSA_KV_GROUPS):
        q6 = _stack_heads(q_ref, g * gwid, hpg, SCALE_LOG2E)
        selb = sel_ref[g].astype(BF16)
        k = jnp.concatenate([pr[pl.ds(k_chunk + g, page, stride=PAGE_CHUNKS), :] for pr in page_refs],
                            axis=0).astype(BF16)
        v = jnp.concatenate([pr[pl.ds(v_chunk + g, page, stride=PAGE_CHUNKS), :] for pr in page_refs],
                            axis=0).astype(BF16)
        allowed = _block_mask(selb, k0, tk) & (kpos <= qpos)
        _flash_tile(q6, k, v, jnp.where(allowed, 0.0, NEG), m_ref.at[g], l_ref.at[g], acc_ref.at[g], e_ref, hpg)

    @pl.when(s_id == pl.num_programs(1) - 1)
    def _():
        cur = past_len // SEL_BLOCK
        nn = new_ref.shape[0]
        npos = past_len + lax.broadcasted_iota(I32, (1, LANE), 1)
        fill = jnp.zeros((LANE - nn, HEAD_DIM), F32)
        for g in range(NSA_KV_GROUPS):
            q6 = _stack_heads(q_ref, g * gwid, hpg, SCALE_LOG2E)
            k = jnp.concatenate([new_ref[:, g * HEAD_DIM:(g + 1) * HEAD_DIM], fill], axis=0).astype(BF16)
            v = jnp.concatenate([new_ref[:, (NSA_KV_GROUPS + g) * HEAD_DIM:(NSA_KV_GROUPS + g + 1) * HEAD_DIM],
                                 fill], axis=0).astype(BF16)
            allowed = (sel_ref[g][:, cur:cur + 1] > 0.5) & (npos <= qpos) & (npos < past_len + nn)
            _flash_tile(q6, k, v, jnp.where(allowed, 0.0, NEG), m_ref.at[g], l_ref.at[g], acc_ref.at[g],
                        e_ref.at[:, 0:LANE], hpg)
            o = acc_ref[g] / l_ref[g]
            for i in range(hpg):
                o_ref[:, g * gwid + i * HEAD_DIM:g * gwid + (i + 1) * HEAD_DIM] = o[i * tq:(i + 1) * tq]


def slc_attend_paged(q, sel, kv4_new, cache, li, page_table, npg):
    b, tq, qw = q.shape
    n_pages = page_table.shape[1]
    page = cache.shape[2] // PAGE_CHUNKS
    past_len = n_pages * page
    gwid = qw // NSA_KV_GROUPS
    hpg = gwid // HEAD_DIM
    nbp = sel.shape[3]
    cw = 2 * NSA_KV_GROUPS * HEAD_DIM
    assert n_pages % npg == 0 and past_len % SEL_BLOCK == 0 and tq <= SEL_BLOCK
    grid_spec = pltpu.PrefetchScalarGridSpec(
        num_scalar_prefetch=1,
        grid=(b, n_pages // npg),
        in_specs=[
            pl.BlockSpec((None, tq, qw), lambda i, s, pt: (i, 0, 0)),
            pl.BlockSpec((None, NSA_KV_GROUPS, tq, nbp), lambda i, s, pt: (i, 0, 0, 0)),
            pl.BlockSpec((None, tq, cw), lambda i, s, pt: (i, 0, 1)),
        ] + [_page_spec(li, k, npg, page) for k in range(npg)],
        out_specs=pl.BlockSpec((None, tq, qw), lambda i, s, pt: (i, 0, 0)),
        scratch_shapes=[pltpu.VMEM((NSA_KV_GROUPS, hpg * tq, LANE), F32),
                        pltpu.VMEM((NSA_KV_GROUPS, hpg * tq, LANE), F32),
                        pltpu.VMEM((NSA_KV_GROUPS, hpg * tq, HEAD_DIM), F32),
                        pltpu.VMEM((hpg * tq, npg * page), _prob_dtype(tq))],
    )
    return pl.pallas_call(
        functools.partial(_slc_paged_body, npg=npg, tq=tq, hpg=hpg, past_len=past_len, page=page),
        grid_spec=grid_spec,
        out_shape=jax.ShapeDtypeStruct((b, tq, qw), F32),
        compiler_params=_cparams("parallel", "arbitrary"),
        name="slc_attend_paged",
    )(page_table, q, sel, kv4_new, *([cache] * npg))


def _win_body(q_ref, k_ref, v_ref, o_ref, l_ref, e_ref, *, qb, nb, span, pos0, kpos0, hpg):
    for u in range(nb):
        blk = pl.program_id(2) * nb + u
        q6 = jnp.concatenate([q_ref[u * qb:(u + 1) * qb, i * HEAD_DIM:(i + 1) * HEAD_DIM] for i in range(hpg)],
                             axis=0)
        q6 = (q6 * SCALE_LOG2E).astype(BF16)
        start = pl.multiple_of(blk * qb, qb)
        k = k_ref[pl.ds(start, span), :].astype(BF16)
        v = v_ref[pl.ds(start, span), :].astype(BF16)
        kpos = kpos0 + blk * qb + lax.broadcasted_iota(I32, (1, span), 1)
        qpos = pos0 + blk * qb + lax.broadcasted_iota(I32, (qb, 1), 0)
        dist = qpos - kpos
        allowed = (kpos >= 0) & (dist >= 0) & (dist < WINDOW)
        bias = jnp.where(allowed, 0.0, NEG)
        s_all = _dot_nt(q6, k)
        for i in range(hpg):
            rows = slice(i * qb, (i + 1) * qb)
            s = s_all[rows] + bias
            e = jnp.exp2(s - jnp.max(s, axis=-1, keepdims=True))
            l_ref[u, rows, :] = jnp.broadcast_to(jnp.sum(e, axis=-1, keepdims=True), (qb, LANE))
            e_ref[u, rows, :] = e.astype(e_ref.dtype)
        o = _dot(e_ref[u].astype(BF16), v) / l_ref[u]
        for i in range(hpg):
            o_ref[u * qb:(u + 1) * qb, i * HEAD_DIM:(i + 1) * HEAD_DIM] = o[i * qb:(i + 1) * qb]


def win_attend(q, kext, qb, span, pos0, kpos0, nb=1):
    b, t, qw = q.shape
    lk = kext.shape[1]
    gwid = qw // NSA_KV_GROUPS
    hpg = gwid // HEAD_DIM
    assert t % (qb * nb) == 0 and (t // qb - 1) * qb + span <= lk and qb % SUBLANE == 0 and span % SUBLANE == 0
    return pl.pallas_call(
        functools.partial(_win_body, qb=qb, nb=nb, span=span, pos0=pos0, kpos0=kpos0, hpg=hpg),
        grid=(b, NSA_KV_GROUPS, t // (qb * nb)),
        in_specs=[
            pl.BlockSpec((None, qb * nb, gwid), lambda i, g, j: (i, j, g)),
            pl.BlockSpec((None, lk, HEAD_DIM), lambda i, g, j: (i, 0, g)),
            pl.BlockSpec((None, lk, HEAD_DIM), lambda i, g, j: (i, 0, NSA_KV_GROUPS + g)),
        ],
        out_specs=pl.BlockSpec((None, qb * nb, gwid), lambda i, g, j: (i, j, g)),
        out_shape=jax.ShapeDtypeStruct((b, t, qw), F32),
        scratch_shapes=[pltpu.VMEM((nb, hpg * qb, LANE), F32),
                        pltpu.VMEM((nb, hpg * qb, span), _prob_dtype(qb))],
        compiler_params=_cparams("parallel", "parallel", "parallel"),
        name="win_attend",
    )(q, kext, kext)


def _row_tile(n, pref):
    t = min(n, pref)
    assert n % t == 0
    return t


def kernel(x_prompt, x_sample, mem_prompt, cache_mem_kv, cache_nsa_kv, cache_nsa_win, state_pool, page_table, g_norm_mix, g_norm_mlp, g_norm_mem, w_mem_kv, mem_qk_gain, w_out, w_mlp_up, w_mlp_down, w_in_pool, w_pool_group, pool_scale, w_in_nsa, nsa_gate_bias, nsa_qk_gain, cmp_pos, cmp_w1, cmp_w2):
    depth = g_norm_mix.shape[0]
    bp, tp, d = x_prompt.shape
    db, ts, _ = x_sample.shape
    n_pages = page_table.shape[1]
    page = cache_nsa_kv.shape[2]
    past_len = n_pages * page
    win_buf = cache_nsa_win.shape[2]
    tok_w = w_pool_group.shape[1] * w_pool_group.shape[2]
    n_heads = tok_w // HEAD_DIM
    n_gate = 3 * n_heads
    kvw = 4 * NSA_KV_GROUPS * HEAD_DIM
    winw = 2 * NSA_KV_GROUPS * HEAD_DIM
    assert win_buf == WINDOW and past_len % SEL_BLOCK == 0 and ts <= SUBLANE and tp % QBLOCK == 0

    tsp = SUBLANE
    xs = jnp.pad(x_sample, ((0, 0), (0, tsp - ts), (0, 0))).reshape(db * tsp, d)
    xp = x_prompt.reshape(bp * tp, d)
    n_p, n_s = bp * tp, db * tsp
    tm_p, tm_s = _row_tile(n_p, 256), n_s
    mem2d = mem_prompt.reshape(bp * mem_prompt.shape[1], d)
    mem_len = mem_prompt.shape[1]
    cache_view = cache_nsa_kv.reshape(cache_nsa_kv.shape[0], cache_nsa_kv.shape[1], page * PAGE_CHUNKS, HEAD_DIM)
    npg = min(16, n_pages)

    mem_kv_p, nsa_kv_p, nsa_kv_s, win_p, win_s, pool_p, pool_s = [], [], [], [], [], [], []
    mem_segs = ((2 * MEM_WIDTH, (1,) * MEM_HEADS + (None,) * MEM_HEADS),)
    pool_segs = ((tok_w, None), (MEM_WIDTH, None))
    nsa_segs = ((tok_w, (0,) * n_heads),
                (kvw, (None, None, None, None, 2, 2, None, None)),
                (winw, (3, 3, None, None)),
                (MEM_WIDTH, None),
                (LANE, None))

    wo_b = w_out.astype(BF16)
    wu_b = w_mlp_up.astype(BF16)
    wd_b = w_mlp_down.astype(BF16)
    wmem_b = w_mem_kv.astype(BF16)
    wpool_b = w_in_pool.astype(BF16)
    wg_b = w_pool_group.astype(BF16)
    kv_end = tok_w + kvw + winw
    wnsa_b = jnp.concatenate(
        [w_in_nsa[:, :, :kv_end], w_in_nsa[:, :, kv_end + n_gate:], w_in_nsa[:, :, kv_end:kv_end + n_gate],
         jnp.zeros(w_in_nsa.shape[:2] + (LANE - n_gate,), w_in_nsa.dtype)], axis=2).astype(BF16)
    mem_cache = cache_mem_kv.reshape(cache_mem_kv.shape[0], db, cache_mem_kv.shape[2] * 2 * MEM_HEADS, HEAD_DIM)

    for i in range(depth):
        li = i // 2
        (mkv,) = in_proj(mem2d, g_norm_mem[i], wmem_b, i, mem_qk_gain[i], mem_segs,
                         _row_tile(mem2d.shape[0], 256))
        mkv_p = mkv.reshape(bp, mem_len, 2 * MEM_WIDTH)
        mem_kv_p.append(mkv_p.reshape(bp, mem_len, 2, MEM_HEADS, HEAD_DIM))
        gated = None
        if i % 2 == 0:
            gains = jnp.ones((1, HEAD_DIM), F32)
            up, qmp = in_proj(xp, g_norm_mix[i], wpool_b, li, gains, pool_segs, tm_p)
            us, qms = in_proj(xs, g_norm_mix[i], wpool_b, li, gains, pool_segs, tm_s)
            up3 = up.reshape(bp, tp, tok_w)
            tok_p = pool_mix(up3, wg_b, li, pool_scale[li], 0, _row_tile(tp, 512)).reshape(n_p, tok_w)
            us3 = us.reshape(db, tsp, tok_w)[:, :ts]
            lead = _round_up(POOL_HIST + ts, SUBLANE) - (POOL_HIST + ts)
            ext = jnp.concatenate([state_pool[li], us3], axis=1)
            ext_pad = jnp.pad(ext, ((0, 0), (lead, 0), (0, 0)))
            l_ext = ext_pad.shape[1]
            tok_e = pool_mix(ext_pad, wg_b, li, pool_scale[li], past_len - POOL_HIST - lead, l_ext)
            tok_s = jnp.pad(tok_e[:, l_ext - ts:], ((0, 0), (0, tsp - ts), (0, 0))).reshape(n_s, tok_w)
            pool_p.append(up3[:, tp - POOL_HIST:])
            pool_s.append(ext[:, -POOL_HIST:])
        else:
            gains = nsa_qk_gain[li]
            bias = jnp.pad(nsa_gate_bias[li], (0, LANE - n_gate)).reshape(1, LANE)
            w1 = cmp_w1[li]
            half = CMP_STRIDE * HEAD_DIM
            w1k = jnp.concatenate([w1[0, :half], w1[0, half:]], axis=1).astype(BF16)
            w1v = jnp.concatenate([w1[1, :half], w1[1, half:]], axis=1).astype(BF16)
            w1b = w1.astype(BF16)
            w2b = cmp_w2[li].astype(BF16)

            q_p, kv4_p, wn_p, qmp, gt_p = in_proj(xp, g_norm_mix[i], wnsa_b, li, gains, nsa_segs, tm_p)
            q3 = q_p.reshape(bp, tp, tok_w)
            kv43 = kv4_p.reshape(bp, tp, kvw)
            wn3 = wn_p.reshape(bp, tp, winw)
            n_cmp = tp // CMP_STRIDE
            ncp = _round_up(n_cmp, LANE)
            ab = cmp_stage1_dense(kv43, w1k, w1v)
            ct = cmp_stage2([ab], cmp_pos[li], w1b, w2b, gains[1], ncp)
            oc_p, sel_p = cmp_attend_topk(q3, ct, 0, tp // SEL_BLOCK, _row_tile(tp, 256))
            os_p = slc_attend_prompt(q3, kv43, sel_p, _row_tile(tp, 512), _row_tile(tp, 512))
            kext = jnp.concatenate([jnp.zeros((bp, WINDOW, winw), F32), wn3], axis=1)
            ow_p = win_attend(q3, kext, QBLOCK, WINDOW + QBLOCK, 0, -WINDOW, nb=2 if tp % (2 * QBLOCK) == 0 else 1)
            tok_p = (oc_p.reshape(n_p, tok_w), os_p.reshape(n_p, tok_w), ow_p.reshape(n_p, tok_w))

            q_s, kv4_s, wn_s, qms, gt_s = in_proj(xs, g_norm_mix[i], wnsa_b, li, gains, nsa_segs, tm_s)
            qs3 = q_s.reshape(db, tsp, tok_w)
            kv4s3 = kv4_s.reshape(db, tsp, kvw)
            wns3 = wn_s.reshape(db, tsp, winw)
            new_rows = jnp.pad(kv4s3[:, :ts], ((0, 0), (0, page - ts), (0, 0)))
            n_rows = _round_up(past_len + ts, SEL_BLOCK)
            n_cmp_s = n_rows // CMP_STRIDE
            ncp_s = _round_up(n_cmp_s, LANE)
            ab_past = cmp_stage1_paged(cache_view, li, page_table, w1k, w1v, npg)
            ab_new = cmp_stage1_dense(new_rows, w1k, w1v)
            ct_s = cmp_stage2([ab_past, ab_new], cmp_pos[li], w1b, w2b, gains[1], ncp_s)
            oc_s, sel_s = cmp_attend_topk(qs3, ct_s, past_len, n_rows // SEL_BLOCK, tsp)
            os_s = slc_attend_paged(qs3, sel_s, kv4s3, cache_view, li, page_table, npg)
            wext = jnp.concatenate([cache_nsa_win[li].reshape(db, win_buf, winw), wns3[:, :ts]], axis=1)
            kext_s = jnp.pad(wext, ((0, 0), (0, tsp - ts), (0, 0)))
            ow_s = win_attend(qs3, kext_s, tsp, win_buf + tsp, past_len, past_len - win_buf)
            tok_s = (oc_s.reshape(n_s, tok_w), os_s.reshape(n_s, tok_w), ow_s.reshape(n_s, tok_w))
            gated = ((gt_p, bias), (gt_s, bias))

            nsa_kv_p.append(kv43.reshape(bp, tp, 4, NSA_KV_GROUPS, HEAD_DIM))
            nsa_kv_s.append(kv4s3[:, :ts].reshape(db, ts, 4, NSA_KV_GROUPS, HEAD_DIM))
            wlen = min(WINDOW, tp)
            win_p.append(wn3[:, tp - wlen:].reshape(bp, wlen, 2, NSA_KV_GROUPS, HEAD_DIM))
            win_s.append(wext[:, -win_buf:].reshape(db, win_buf, 2, NSA_KV_GROUPS, HEAD_DIM))

        hp = out_proj(tok_p, qmp, mkv_p, mem_qk_gain[i, 0], xp, wo_b, i, tm_p, tp, gates=gated and gated[0])
        hs = out_proj(tok_s, qms, mem_cache, mem_qk_gain[i, 0], xs, wo_b, i, tm_s, tsp, gates=gated and gated[1],
                      kv_layer=i)
        xp = mlp(hp, g_norm_mlp[i], wu_b, wd_b, i, _row_tile(n_p, 512), 1024)
        xs = mlp(hs, g_norm_mlp[i], wu_b, wd_b, i, tm_s, 1024)

    y_p = xp.reshape(bp, tp, d)
    y_s = xs.reshape(db, tsp, d)[:, :ts]
    return (y_p, y_s, jnp.stack(mem_kv_p), jnp.stack(nsa_kv_p), jnp.stack(nsa_kv_s),
            jnp.stack(win_p), jnp.stack(win_s), jnp.stack(pool_p), jnp.stack(pool_s))
```

```python
import functools

import jax
import jax.numpy as jnp
from jax import lax
from jax.experimental import pallas as pl
from jax.experimental.pallas import tpu as pltpu

F32 = jnp.float32
BF16 = jnp.bfloat16
I32 = jnp.int32

HEAD_DIM = 128
MEM_HEADS = 4
MEM_WIDTH = MEM_HEADS * HEAD_DIM
NSA_KV_GROUPS = 2
CMP_STRIDE = 16
CMP_BLOCK = 2 * CMP_STRIDE
SEL_BLOCK = 64
N_SEL = 16
WINDOW = 512
QBLOCK = 128
POOL_WINDOWS = (2, 4, 8, 16)
POOL_HIST = max(POOL_WINDOWS) - 1
POOL_HALO = 16
ATTN_SCALE = HEAD_DIM ** -0.5
SCALE_LOG2E = ATTN_SCALE * 1.4426950408889634
EPS = 1e-6
NEG = -1e30
BIG = 1e30
LOWEST = -3.0e38
MASK_OFF = 2.0 ** 20
SEL_SHIFT = SEL_BLOCK.bit_length() - 1
RATIO_SHIFT = (SEL_BLOCK // CMP_STRIDE).bit_length() - 1
assert 1 << SEL_SHIFT == SEL_BLOCK and 1 << RATIO_SHIFT == SEL_BLOCK // CMP_STRIDE

LANE = 128
SUBLANE = 8
VMEM_LIMIT_BYTES = 56 * 1024 * 1024


def _cparams(*sem):
    return pltpu.CompilerParams(dimension_semantics=sem, vmem_limit_bytes=VMEM_LIMIT_BYTES)


def _round_up(x, m):
    return (x + m - 1) // m * m


def _rms(x, gain):
    return x * lax.rsqrt(jnp.mean(x * x, axis=-1, keepdims=True) + EPS) * gain


def _dot(a, b):
    return jnp.dot(a, b, preferred_element_type=F32)


def _dot_nt(a, b):
    return lax.dot_general(a, b, (((1,), (1,)), ((), ())), preferred_element_type=F32)


def _in_proj_body(x_ref, g_ref, w_ref, gains_ref, *out_refs, segs):
    xb = _rms(x_ref[...], g_ref[...]).astype(BF16)
    col = 0
    for o_ref, (width, norms) in zip(out_refs, segs):
        y = _dot(xb, w_ref[:, col:col + width])
        if norms is None:
            o_ref[...] = y
        else:
            for c, gi in enumerate(norms):
                yc = y[:, c * LANE:(c + 1) * LANE]
                if gi is not None:
                    yc = _rms(yc, gains_ref[gi:gi + 1, :])
                o_ref[:, c * LANE:(c + 1) * LANE] = yc
        col += width


def in_proj(x2d, g, w, layer, gains, segs, tm):
    n, d = x2d.shape
    wtot = w.shape[2]
    assert n % tm == 0 and wtot == sum(s[0] for s in segs)
    return pl.pallas_call(
        functools.partial(_in_proj_body, segs=segs),
        grid=(n // tm,),
        in_specs=[
            pl.BlockSpec((tm, d), lambda i: (i, 0)),
            pl.BlockSpec((1, d), lambda i: (0, 0)),
            pl.BlockSpec((None, d, wtot), lambda i: (layer, 0, 0)),
            pl.BlockSpec(gains.shape, lambda i: (0, 0)),
        ],
        out_specs=[pl.BlockSpec((tm, s[0]), lambda i: (i, 0)) for s in segs],
        out_shape=[jax.ShapeDtypeStruct((n, s[0]), F32) for s in segs],
        compiler_params=_cparams("parallel"),
        name="in_proj",
    )(x2d, g.reshape(1, d), w, gains)


def _mem_attend(qm, kv_ref, gq_ref, row_major):
    chunks = 2 * MEM_HEADS
    m = kv_ref.shape[0] // chunks if row_major else kv_ref.shape[0]
    heads = []
    for h in range(MEM_HEADS):
        sl = slice(h * HEAD_DIM, (h + 1) * HEAD_DIM)
        q = _rms(qm[:, sl], gq_ref[...]).astype(BF16)
        if row_major:
            k = kv_ref[pl.ds(h, m, stride=chunks), :].astype(BF16)
            v = kv_ref[pl.ds(MEM_HEADS + h, m, stride=chunks), :].astype(BF16)
        else:
            k = kv_ref[:, sl].astype(BF16)
            v = kv_ref[:, MEM_WIDTH + h * HEAD_DIM:MEM_WIDTH + (h + 1) * HEAD_DIM].astype(BF16)
        s = _dot_nt(q, k) * ATTN_SCALE
        e = jnp.exp(s - jnp.max(s, axis=-1, keepdims=True))
        heads.append(_dot(e.astype(BF16), v) / jnp.sum(e, axis=-1, keepdims=True))
    return jnp.concatenate(heads, axis=1)


def _out_proj_body(*refs, gated, row_major, rows_per_batch):
    if gated:
        gt_ref, bias_ref, oc_ref, os_ref, ow_ref = refs[:5]
        refs = refs[5:]
        gs = jax.nn.sigmoid(gt_ref[...] + bias_ref[...])
        heads = []
        for h in range(oc_ref.shape[1] // HEAD_DIM):
            sl = slice(h * HEAD_DIM, (h + 1) * HEAD_DIM)
            heads.append(gs[:, 3 * h:3 * h + 1] * oc_ref[:, sl] + gs[:, 3 * h + 1:3 * h + 2] * os_ref[:, sl]
                         + gs[:, 3 * h + 2:3 * h + 3] * ow_ref[:, sl])
        tok = jnp.concatenate(heads, axis=1)
    else:
        tok = refs[0][...]
        refs = refs[1:]
    qm_ref, kv_ref, gq_ref, x_ref, w_ref, o_ref = refs
    tw = tok.shape[1]
    acc = _dot(tok.astype(BF16), w_ref[0:tw, :])
    if len(kv_ref.shape) == 2:
        mem = _mem_attend(qm_ref[...], kv_ref, gq_ref, row_major)
    else:
        mem = jnp.concatenate([_mem_attend(qm_ref[b * rows_per_batch:(b + 1) * rows_per_batch, :], kv_ref.at[b],
                                           gq_ref, row_major) for b in range(kv_ref.shape[0])], axis=0)
    acc = acc + _dot(mem.astype(BF16), w_ref[tw:, :])
    o_ref[...] = x_ref[...] + acc


def out_proj(tok, qm, mkv, gq, x, w, layer, tm, rows_per_batch, gates=None, kv_layer=None):
    n, d = x.shape
    toks = tok if gates is not None else (tok,)
    tw = toks[0].shape[1]
    assert rows_per_batch % tm == 0 or (tm == n and tm % rows_per_batch == 0 and rows_per_batch % SUBLANE == 0)
    tiles = max(rows_per_batch // tm, 1)
    one = rows_per_batch % tm == 0
    row = lambda width: pl.BlockSpec((tm, width), lambda i: (i, 0))
    lead_specs, lead_args = [], []
    if gates is not None:
        lead_specs = [row(LANE), pl.BlockSpec((1, LANE), lambda i: (0, 0))]
        lead_args = list(gates)
    if kv_layer is None:
        kv_spec = (pl.BlockSpec((None,) + mkv.shape[1:], lambda i: (i // tiles, 0, 0)) if one else
                   pl.BlockSpec(mkv.shape, lambda i: (0, 0, 0)))
    else:
        kv_spec = (pl.BlockSpec((None, None) + mkv.shape[2:], lambda i: (kv_layer, i // tiles, 0, 0)) if one else
                   pl.BlockSpec((None,) + mkv.shape[1:], lambda i: (kv_layer, 0, 0, 0)))
    return pl.pallas_call(
        functools.partial(_out_proj_body, gated=gates is not None, row_major=kv_layer is not None,
                          rows_per_batch=rows_per_batch),
        grid=(n // tm,),
        in_specs=lead_specs + [row(tw) for _ in toks] + [
            row(MEM_WIDTH), kv_spec, pl.BlockSpec((1, HEAD_DIM), lambda i: (0, 0)), row(d),
            pl.BlockSpec((None, tw + MEM_WIDTH, d), lambda i: (layer, 0, 0)),
        ],
        out_specs=row(d),
        out_shape=jax.ShapeDtypeStruct((n, d), F32),
        compiler_params=_cparams("parallel"),
        name="out_proj",
    )(*lead_args, *toks, qm, mkv, gq.reshape(1, HEAD_DIM), x, w)


def _mlp_body(h_ref, g_ref, wu_ref, wd_ref, o_ref, xn_ref, acc_ref):
    k = pl.program_id(1)

    @pl.when(k == 0)
    def _():
        xn_ref[...] = _rms(h_ref[...], g_ref[...]).astype(BF16)
        acc_ref[...] = jnp.zeros_like(acc_ref)

    a = jnp.maximum(_dot(xn_ref[...], wu_ref[...]), 0.0)
    acc_ref[...] += _dot((a * a).astype(BF16), wd_ref[...])

    @pl.when(k == pl.num_programs(1) - 1)
    def _():
        o_ref[...] = h_ref[...] + acc_ref[...]


def mlp(h, g, w_up, w_down, layer, tm, tf):
    n, d = h.shape
    ff = w_up.shape[2]
    return pl.pallas_call(
        _mlp_body,
        grid=(n // tm, ff // tf),
        in_specs=[
            pl.BlockSpec((tm, d), lambda i, k: (i, 0)),
            pl.BlockSpec((1, d), lambda i, k: (0, 0)),
            pl.BlockSpec((None, d, tf), lambda i, k: (layer, 0, k)),
            pl.BlockSpec((None, tf, d), lambda i, k: (layer, k, 0)),
        ],
        out_specs=pl.BlockSpec((tm, d), lambda i, k: (i, 0)),
        out_shape=jax.ShapeDtypeStruct((n, d), F32),
        scratch_shapes=[pltpu.VMEM((tm, d), BF16), pltpu.VMEM((tm, d), F32)],
        compiler_params=_cparams("parallel", "arbitrary"),
        name="mlp",
    )(h, g.reshape(1, d), w_up, w_down)


def _pool_body(cur_ref, halo_ref, wg_ref, sc_ref, o_ref, ext_ref, *, pos0, tt, gw):
    t = pl.program_id(1)
    ext_ref[0:POOL_HALO, :] = jnp.where(t == 0, 0.0, halo_ref[...])
    ext_ref[POOL_HALO:POOL_HALO + tt, :] = cur_ref[...]
    row = t * tt + lax.broadcasted_iota(I32, (tt, 1), 0)
    for gi, w in enumerate(POOL_WINDOWS):
        c0, c1 = gi * gw, (gi + 1) * gw
        x = ext_ref[POOL_HALO:POOL_HALO + tt, c0:c1]
        s = x
        for dd in range(1, w):
            s = s + ext_ref[POOL_HALO - dd:POOL_HALO - dd + tt, c0:c1]
        cnt = jnp.minimum(pos0 + row + 1, w).astype(F32)
        pooled = s / cnt - x
        o_ref[:, c0:c1] = _dot(pooled.astype(BF16), wg_ref[gi]) * sc_ref[:, c0:c1]


def pool_mix(u, w_group, layer, scale, pos0, tt):
    b, l, c = u.shape
    _, ng, gw, _ = w_group.shape
    assert l % tt == 0 and tt % POOL_HALO == 0 or l == tt
    halo_per_tile = tt // POOL_HALO if tt % POOL_HALO == 0 else 0
    return pl.pallas_call(
        functools.partial(_pool_body, pos0=pos0, tt=tt, gw=gw),
        grid=(b, l // tt),
        in_specs=[
            pl.BlockSpec((None, tt, c), lambda i, j: (i, j, 0)),
            pl.BlockSpec((None, POOL_HALO, c), lambda i, j: (i, jnp.maximum(j * halo_per_tile - 1, 0), 0)),
            pl.BlockSpec((None, ng, gw, gw), lambda i, j: (layer, 0, 0, 0)),
            pl.BlockSpec((1, c), lambda i, j: (0, 0)),
        ],
        out_specs=pl.BlockSpec((None, tt, c), lambda i, j: (i, j, 0)),
        out_shape=jax.ShapeDtypeStruct((b, l, c), F32),
        scratch_shapes=[pltpu.VMEM((POOL_HALO + tt, c), F32)],
        compiler_params=_cparams("parallel", "parallel"),
        name="pool_mix",
    )(u, u, w_group, scale.reshape(1, c))


def _cmp1_compute(page_refs, wk_ref, wv_ref, o_ref, stage_ref, rows):
    m = len(page_refs) * rows // CMP_STRIDE
    for slot, w_ref in ((0, wk_ref), (1, wv_ref)):
        xs = []
        for g in range(NSA_KV_GROUPS):
            j = slot * NSA_KV_GROUPS + g
            for p, pr in enumerate(page_refs):
                stage_ref[p * rows:(p + 1) * rows, :] = pr[:, j * HEAD_DIM:(j + 1) * HEAD_DIM]
            pieces = [stage_ref[pl.ds(r, m, stride=CMP_STRIDE), :] for r in range(CMP_STRIDE)]
            xs.append(jnp.concatenate(pieces, axis=1).astype(BF16))
        y = _dot(jnp.concatenate(xs, axis=0), w_ref[...])
        for g in range(NSA_KV_GROUPS):
            j = slot * NSA_KV_GROUPS + g
            o_ref[:, j * 2 * HEAD_DIM:(j + 1) * 2 * HEAD_DIM] = y[g * m:(g + 1) * m]


def _cmp1_dense_body(x_ref, wk_ref, wv_ref, o_ref, stage_ref, *, rows):
    _cmp1_compute([x_ref], wk_ref, wv_ref, o_ref, stage_ref, rows)


def cmp_stage1_dense(rows4, w1k, w1v, lane_block=0):
    b, l, _ = rows4.shape
    nh = l // CMP_STRIDE
    cw = 2 * NSA_KV_GROUPS * HEAD_DIM
    return pl.pallas_call(
        functools.partial(_cmp1_dense_body, rows=l),
        grid=(b,),
        in_specs=[
            pl.BlockSpec((None, l, cw), lambda i: (i, 0, lane_block)),
            pl.BlockSpec(w1k.shape, lambda i: (0, 0)),
            pl.BlockSpec(w1v.shape, lambda i: (0, 0)),
        ],
        out_specs=pl.BlockSpec((None, nh, 2 * cw), lambda i: (i, 0, 0)),
        out_shape=jax.ShapeDtypeStruct((b, nh, 2 * cw), F32),
        scratch_shapes=[pltpu.VMEM((l, HEAD_DIM), F32)],
        compiler_params=_cparams("parallel"),
        name="cmp_stage1_dense",
    )(rows4, w1k, w1v)


PAGE_CHUNKS = 4 * NSA_KV_GROUPS


def _cmp1_paged_body(pt_ref, *refs, npg, page):
    del pt_ref
    page_refs = refs[:npg]
    wk_ref, wv_ref, o_ref = refs[npg:npg + 3]
    nh = page // CMP_STRIDE
    m = npg * nh
    for slot, w_ref in ((0, wk_ref), (1, wv_ref)):
        xs = []
        for g in range(NSA_KV_GROUPS):
            j = slot * NSA_KV_GROUPS + g
            per_page = []
            for pr in page_refs:
                pieces = [pr[pl.ds(r * PAGE_CHUNKS + j, nh, stride=CMP_STRIDE * PAGE_CHUNKS), :]
                          for r in range(CMP_STRIDE)]
                per_page.append(jnp.concatenate(pieces, axis=1))
            xs.append(jnp.concatenate(per_page, axis=0).astype(BF16))
        y = _dot(jnp.concatenate(xs, axis=0), w_ref[...])
        for g in range(NSA_KV_GROUPS):
            j = slot * NSA_KV_GROUPS + g
            o_ref[:, j * 2 * HEAD_DIM:(j + 1) * 2 * HEAD_DIM] = y[g * m:(g + 1) * m]


def _page_spec(li, k, npg, page):
    return pl.BlockSpec((None, None, page * PAGE_CHUNKS, HEAD_DIM),
                        lambda i, s, pt: (li, pt[i, s * npg + k], 0, 0))


def cmp_stage1_paged(cache, li, page_table, w1k, w1v, npg):
    b, n_pages = page_table.shape
    page = cache.shape[2] // PAGE_CHUNKS
    nh = page // CMP_STRIDE
    cw = 2 * NSA_KV_GROUPS * HEAD_DIM
    assert n_pages % npg == 0
    grid_spec = pltpu.PrefetchScalarGridSpec(
        num_scalar_prefetch=1,
        grid=(b, n_pages // npg),
        in_specs=[_page_spec(li, k, npg, page) for k in range(npg)] + [
            pl.BlockSpec(w1k.shape, lambda i, s, pt: (0, 0)),
            pl.BlockSpec(w1v.shape, lambda i, s, pt: (0, 0)),
        ],
        out_specs=pl.BlockSpec((None, npg * nh, 2 * cw), lambda i, s, pt: (i, s, 0)),
    )
    return pl.pallas_call(
        functools.partial(_cmp1_paged_body, npg=npg, page=page),
        grid_spec=grid_spec,
        out_shape=jax.ShapeDtypeStruct((b, n_pages * nh, 2 * cw), F32),
        compiler_params=_cparams("parallel", "arbitrary"),
        name="cmp_stage1_paged",
    )(page_table, *([cache] * npg), w1k, w1v)


def _cmp2_body(*refs, n_ab, ncp):
    ab_refs = refs[:n_ab]
    pos_ref, w1_ref, w2_ref, gk_ref, o_ref = refs[n_ab:]
    n_have = sum(r.shape[0] for r in ab_refs)
    fill = [jnp.zeros((ncp + SUBLANE - n_have, HEAD_DIM), F32)]
    for slot in range(2):
        posflat = jnp.concatenate([pos_ref[slot, r:r + 1, :] for r in range(CMP_BLOCK)], axis=1)
        posb = jnp.broadcast_to(posflat, (SUBLANE, CMP_BLOCK * HEAD_DIM)).astype(BF16)
        posc = _dot(posb, w1_ref[slot])[0:1]
        for g in range(NSA_KV_GROUPS):
            j = slot * NSA_KV_GROUPS + g
            a_lanes = slice(j * 2 * HEAD_DIM, j * 2 * HEAD_DIM + HEAD_DIM)
            b_lanes = slice(j * 2 * HEAD_DIM + HEAD_DIM, (j + 1) * 2 * HEAD_DIM)
            a = jnp.concatenate([r[:, a_lanes] for r in ab_refs] + fill, axis=0)
            bm = jnp.concatenate([r[:, b_lanes] for r in ab_refs] + fill, axis=0)
            bm_next = pltpu.roll(bm, ncp + SUBLANE - 1, axis=0)
            hid = jax.nn.gelu(a[0:ncp] + bm_next[0:ncp] + posc)
            out = _dot(hid.astype(BF16), w2_ref[slot])
            if slot == 0:
                out = _rms(out, gk_ref[...])
            o_ref[:, j * HEAD_DIM:(j + 1) * HEAD_DIM] = out


def cmp_stage2(abs_, pos, w1, w2, gk, ncp):
    b, _, w = abs_[0].shape
    assert all(a.shape[1] % SUBLANE == 0 for a in abs_) and sum(a.shape[1] for a in abs_) <= ncp + SUBLANE
    cw = 2 * NSA_KV_GROUPS * HEAD_DIM
    return pl.pallas_call(
        functools.partial(_cmp2_body, n_ab=len(abs_), ncp=ncp),
        grid=(b,),
        in_specs=[pl.BlockSpec((None, a.shape[1], w), lambda i: (i, 0, 0)) for a in abs_] + [
            pl.BlockSpec(pos.shape, lambda i: (0, 0, 0)),
            pl.BlockSpec(w1.shape, lambda i: (0, 0, 0)),
            pl.BlockSpec(w2.shape, lambda i: (0, 0, 0)),
            pl.BlockSpec((1, HEAD_DIM), lambda i: (0, 0)),
        ],
        out_specs=pl.BlockSpec((None, ncp, cw), lambda i: (i, 0, 0)),
        out_shape=jax.ShapeDtypeStruct((b, ncp, cw), F32),
        compiler_params=_cparams("parallel"),
        name="cmp_stage2",
    )(*abs_, pos, w1, w2, gk.reshape(1, HEAD_DIM))


def _cmp_attn_body(q_ref, ct_ref, o_ref, sel_ref, st_ref, *, pos0, tq, ncp, n_blk, n_sel, hpg):
    t = pl.program_id(1)
    ng = NSA_KV_GROUPS
    gwid = hpg * HEAD_DIM
    ratio = SEL_BLOCK // CMP_STRIDE
    nbp = ncp // ratio
    q0 = pos0 + t * tq
    qpos = q0 + lax.broadcasted_iota(I32, (tq, 1), 0)
    cidx = lax.broadcasted_iota(I32, (1, ncp), 1)
    cmask = (cidx * CMP_STRIDE + (CMP_BLOCK - 1)) <= qpos
    bias = jnp.where(cmask, 0.0, NEG)
    rp = _round_up(tq, LANE)
    nbpp = sel_ref.shape[2]
    jb = lax.broadcasted_iota(I32, (nbp, 1), 0)
    jf = jb.astype(F32)
    cur = (q0 + lax.broadcasted_iota(I32, (1, rp), 1)) >> SEL_SHIFT
    forced = (jb == 0) | (jb == cur) | (jb == cur - 1)
    for g in range(ng):
        q6 = _stack_heads(q_ref, g * gwid, hpg, ATTN_SCALE)
        kc = ct_ref[:, g * HEAD_DIM:(g + 1) * HEAD_DIM].astype(BF16)
        vc = ct_ref[:, (ng + g) * HEAD_DIM:(ng + g + 1) * HEAD_DIM].astype(BF16)
        s_all = _dot_nt(q6, kc)
        imp = jnp.zeros((tq, ncp), F32)
        probs = []
        for i in range(hpg):
            s = s_all[i * tq:(i + 1) * tq] + bias
            e = jnp.exp(s - jnp.max(s, axis=-1, keepdims=True))
            p = jnp.where(cmask, e / jnp.sum(e, axis=-1, keepdims=True), 0.0)
            imp = imp + p
            probs.append(p)
        o = _dot(jnp.concatenate(probs, axis=0).astype(BF16), vc)
        for i in range(hpg):
            o_ref[:, g * gwid + i * HEAD_DIM:g * gwid + (i + 1) * HEAD_DIM] = o[i * tq:(i + 1) * tq]
        imp2 = imp + jnp.where(cidx == 0, 0.0, pltpu.roll(imp, 1, axis=1))
        t1 = imp2 + pltpu.roll(imp2, ncp - 1, axis=1)
        ps = t1 + pltpu.roll(t1, ncp - 2, axis=1)

        if rp > tq:
            ps = jnp.concatenate([ps, jnp.zeros((rp - tq, ncp), F32)], axis=0)
        ps_t = ps.T
        tiles = []
        for h in range(rp // LANE):
            st = st_ref.at[g * (rp // LANE) + h]
            st[...] = ps_t[:, h * LANE:(h + 1) * LANE]
            tiles.append(st[pl.ds(0, nbp, stride=ratio), :])
        ps_blk = jnp.concatenate(tiles, axis=1)
        score = jnp.where(forced, BIG, jnp.where(jb <= cur, ps_blk, NEG))
        score = jnp.where(jb < n_blk, score, LOWEST)
        sel_t = jnp.zeros((nbp, rp), F32)
        for _ in range(n_sel):
            m = jnp.max(score, axis=0, keepdims=True)
            first = jnp.min(jnp.where(score == m, jf, float(nbp)), axis=0, keepdims=True)
            pick = jf == first
            sel_t = jnp.where(pick & (m > 0.5 * NEG), 1.0, sel_t)
            score = jnp.where(pick, LOWEST, score)
        if nbpp > nbp:
            sel_t = jnp.concatenate([sel_t, jnp.zeros((nbpp - nbp, rp), F32)], axis=0)
        sel_ref[g] = sel_t.T[0:tq]


def cmp_attend_topk(q, ct, pos0, n_blk, tq):
    b, t, qw = q.shape
    ncp = ct.shape[1]
    gwid = qw // NSA_KV_GROUPS
    hpg = gwid // HEAD_DIM
    ratio = SEL_BLOCK // CMP_STRIDE
    assert ncp % LANE == 0 and t % tq == 0 and tq & (tq - 1) == 0 and tq % SUBLANE == 0
    nbpp = _round_up(ncp // ratio, LANE)
    n_sel = min(N_SEL, n_blk)
    return pl.pallas_call(
        functools.partial(_cmp_attn_body, pos0=pos0, tq=tq, ncp=ncp, n_blk=n_blk, n_sel=n_sel, hpg=hpg),
        grid=(b, t // tq),
        in_specs=[
            pl.BlockSpec((None, tq, qw), lambda i, j: (i, j, 0)),
            pl.BlockSpec((None, ncp, ct.shape[2]), lambda i, j: (i, 0, 0)),
        ],
        out_specs=[
            pl.BlockSpec((None, tq, qw), lambda i, j: (i, j, 0)),
            pl.BlockSpec((None, NSA_KV_GROUPS, tq, nbpp), lambda i, j: (i, 0, j, 0)),
        ],
        out_shape=[
            jax.ShapeDtypeStruct((b, t, qw), F32),
            jax.ShapeDtypeStruct((b, NSA_KV_GROUPS, t, nbpp), F32),
        ],
        scratch_shapes=[pltpu.VMEM((NSA_KV_GROUPS * (_round_up(tq, LANE) // LANE), ncp, LANE), F32)],
        compiler_params=_cparams("parallel", "parallel"),
        name="cmp_attend_topk",
    )(q, ct)


def _stack_heads(q_ref, lane0, hpg, scale=None):
    q = jnp.concatenate([q_ref[:, lane0 + i * HEAD_DIM:lane0 + (i + 1) * HEAD_DIM] for i in range(hpg)], axis=0)
    return (q if scale is None else q * scale).astype(BF16)


def _prob_dtype(rows_per_head):
    return BF16 if rows_per_head % (2 * SUBLANE) == 0 else F32


def _flash_init(m_ref, l_ref, acc_ref):
    m_ref[...] = jnp.full_like(m_ref, NEG)
    l_ref[...] = jnp.zeros_like(l_ref)
    acc_ref[...] = jnp.zeros_like(acc_ref)


def _flash_tile(q6, k, v, bias, m_ref, l_ref, acc_ref, e_ref, hpg):
    tk = k.shape[0]
    tq = q6.shape[0] // hpg
    assert tk % LANE == 0 and acc_ref.shape[1] == LANE
    s_all = _dot_nt(q6, k)
    for i in range(hpg):
        rows = slice(i * tq, (i + 1) * tq)
        s = s_all[rows] if bias is None else s_all[rows] + bias
        m_old = m_ref[rows, :]
        m_new = jnp.maximum(m_old, jnp.max(s, axis=-1, keepdims=True))
        e = jnp.exp2(s - jnp.concatenate([m_new] * (tk // LANE), axis=1))
        alpha = jnp.exp2(m_old - m_new)
        l_ref[rows, :] = alpha * l_ref[rows, :] + jnp.sum(e, axis=-1, keepdims=True)
        acc_ref[rows, :] = alpha * acc_ref[rows, :]
        m_ref[rows, :] = m_new
        e_ref[rows, :] = e.astype(e_ref.dtype)
    acc_ref[...] += _dot(e_ref[...].astype(BF16), v)


def _slc_prompt_body(q_ref, k_ref, v_ref, sel_ref, o_ref, kb_ref, vb_ref, m_ref, l_ref, acc_ref, e_ref,
                     *, tq, tk, hpg):
    t = pl.program_id(2)

    n_keys = k_ref.shape[0]
    nbp = sel_ref.shape[1]

    @pl.when(t == 0)
    def _():
        kb_ref[:, 0:HEAD_DIM] = k_ref[...].astype(BF16)
        key_blk = lax.broadcasted_iota(I32, (n_keys, nbp), 0) >> SEL_SHIFT
        kb_ref[:, HEAD_DIM:] = (key_blk == lax.broadcasted_iota(I32, (n_keys, nbp), 1)).astype(BF16)
        vb_ref[...] = v_ref[...].astype(BF16)

    off = ((sel_ref[...] - 1.0) * MASK_OFF).astype(BF16)
    q6 = jnp.concatenate([_stack_heads(q_ref, 0, hpg, SCALE_LOG2E), jnp.concatenate([off] * hpg, axis=0)], axis=1)
    qpos = t * tq + lax.broadcasted_iota(I32, (tq, 1), 0)
    _flash_init(m_ref, l_ref, acc_ref)

    def tile(kt, causal):
        k0 = pl.multiple_of(kt * tk, tk)
        bias = None
        if causal:
            kpos = k0 + lax.broadcasted_iota(I32, (1, tk), 1)
            bias = jnp.where(kpos <= qpos, 0.0, NEG)
        _flash_tile(q6, kb_ref[pl.ds(k0, tk), :], vb_ref[pl.ds(k0, tk), :], bias, m_ref, l_ref, acc_ref, e_ref, hpg)

    def below_diagonal(kt, carry):
        tile(kt, False)
        return carry

    def on_diagonal(kt, carry):
        tile(kt, True)
        return carry

    n_below = (t * tq + 1) // tk
    lax.fori_loop(0, n_below, below_diagonal, 0)
    lax.fori_loop(n_below, (t * tq + tq - 1) // tk + 1, on_diagonal, 0)
    o = acc_ref[...] / l_ref[...]
    for i in range(hpg):
        o_ref[:, i * HEAD_DIM:(i + 1) * HEAD_DIM] = o[i * tq:(i + 1) * tq]


def slc_attend_prompt(q, kv4, sel, tq, tk):
    b, t, qw = q.shape
    gwid = qw // NSA_KV_GROUPS
    hpg = gwid // HEAD_DIM
    nbp = sel.shape[3]
    assert t % tq == 0 and t % tk == 0
    return pl.pallas_call(
        functools.partial(_slc_prompt_body, tq=tq, tk=tk, hpg=hpg),
        grid=(b, NSA_KV_GROUPS, t // tq),
        in_specs=[
            pl.BlockSpec((None, tq, gwid), lambda i, g, j: (i, j, g)),
            pl.BlockSpec((None, t, HEAD_DIM), lambda i, g, j: (i, 0, 2 * NSA_KV_GROUPS + g)),
            pl.BlockSpec((None, t, HEAD_DIM), lambda i, g, j: (i, 0, 3 * NSA_KV_GROUPS + g)),
            pl.BlockSpec((None, None, tq, nbp), lambda i, g, j: (i, g, j, 0)),
        ],
        out_specs=pl.BlockSpec((None, tq, gwid), lambda i, g, j: (i, j, g)),
        out_shape=jax.ShapeDtypeStruct((b, t, qw), F32),
        scratch_shapes=[pltpu.VMEM((t, HEAD_DIM + nbp), BF16), pltpu.VMEM((t, HEAD_DIM), BF16),
                        pltpu.VMEM((hpg * tq, LANE), F32), pltpu.VMEM((hpg * tq, LANE), F32),
                        pltpu.VMEM((hpg * tq, HEAD_DIM), F32), pltpu.VMEM((hpg * tq, tk), BF16)],
        compiler_params=_cparams("parallel", "parallel", "arbitrary"),
        name="slc_attend_prompt",
    )(q, kv4, kv4, sel)


def _slc_paged_body(pt_ref, q_ref, sel_ref, new_ref, *refs, npg, tq, hpg, past_len, page):
    del pt_ref
    page_refs = refs[:npg]
    o_ref, m_ref, l_ref, acc_ref, e_ref, oh_ref = refs[npg:]
    s_id = pl.program_id(1)
    gwid = hpg * HEAD_DIM
    qpos = past_len + lax.broadcasted_iota(I32, (tq, 1), 0)
    k_chunk, v_chunk = 2 * NSA_KV_GROUPS, 3 * NSA_KV_GROUPS

    tk = npg * page
    nbp = sel_ref.shape[2]

    @pl.when(s_id == 0)
    def _():
        _flash_init(m_ref, l_ref, acc_ref)
        rel_blk = lax.broadcasted_iota(I32, (tk, LANE), 0) >> SEL_SHIFT
        oh_ref[...] = (rel_blk == lax.broadcasted_iota(I32, (tk, LANE), 1)).astype(BF16)

    blk0 = s_id * (tk // SEL_BLOCK)
    pick = (lax.broadcasted_iota(I32, (nbp, LANE), 0) == blk0 + lax.broadcasted_iota(I32, (nbp, LANE), 1)).astype(BF16)
    for g in range(NSA_KV_GROUPS):
        off = ((_dot(sel_ref[g].astype(BF16), pick) - 1.0) * MASK_OFF).astype(BF16)
        q6 = jnp.concatenate([_stack_heads(q_ref, g * gwid, hpg, SCALE_LOG2E), jnp.concatenate([off] * hpg, axis=0)],
                             axis=1)
        k = jnp.concatenate([pr[pl.ds(k_chunk + g, page, stride=PAGE_CHUNKS), :] for pr in page_refs],
                            axis=0).astype(BF16)
        v = jnp.concatenate([pr[pl.ds(v_chunk + g, page, stride=PAGE_CHUNKS), :] for pr in page_refs],
                            axis=0).astype(BF16)
        _flash_tile(q6, jnp.concatenate([k, oh_ref[...]], axis=1), v, None, m_ref.at[g], l_ref.at[g], acc_ref.at[g],
                    e_ref, hpg)

    @pl.when(s_id == pl.num_programs(1) - 1)
    def _():
        cur = past_len // SEL_BLOCK
        nn = new_ref.shape[0]
        npos = past_len + lax.broadcasted_iota(I32, (1, LANE), 1)
        fill = jnp.zeros((LANE - nn, HEAD_DIM), F32)
        for g in range(NSA_KV_GROUPS):
            q6 = _stack_heads(q_ref, g * gwid, hpg, SCALE_LOG2E)
            k = jnp.concatenate([new_ref[:, g * HEAD_DIM:(g + 1) * HEAD_DIM], fill], axis=0).astype(BF16)
            v = jnp.concatenate([new_ref[:, (NSA_KV_GROUPS + g) * HEAD_DIM:(NSA_KV_GROUPS + g + 1) * HEAD_DIM],
                                 fill], axis=0).astype(BF16)
            allowed = (sel_ref[g][:, cur:cur + 1] > 0.5) & (npos <= qpos) & (npos < past_len + nn)
            _flash_tile(q6, k, v, jnp.where(allowed, 0.0, NEG), m_ref.at[g], l_ref.at[g], acc_ref.at[g],
                        e_ref.at[:, 0:LANE], hpg)
            o = acc_ref[g] / l_ref[g]
            for i in range(hpg):
                o_ref[:, g * gwid + i * HEAD_DIM:g * gwid + (i + 1) * HEAD_DIM] = o[i * tq:(i + 1) * tq]


def slc_attend_paged(q, sel, kv4_new, cache, li, page_table, npg):
    b, tq, qw = q.shape
    n_pages = page_table.shape[1]
    page = cache.shape[2] // PAGE_CHUNKS
    past_len = n_pages * page
    gwid = qw // NSA_KV_GROUPS
    hpg = gwid // HEAD_DIM
    nbp = sel.shape[3]
    cw = 2 * NSA_KV_GROUPS * HEAD_DIM
    assert n_pages % npg == 0 and past_len % SEL_BLOCK == 0 and tq <= SEL_BLOCK
    assert (npg * page) % SEL_BLOCK == 0 and npg * page // SEL_BLOCK <= LANE
    grid_spec = pltpu.PrefetchScalarGridSpec(
        num_scalar_prefetch=1,
        grid=(b, n_pages // npg),
        in_specs=[
            pl.BlockSpec((None, tq, qw), lambda i, s, pt: (i, 0, 0)),
            pl.BlockSpec((None, NSA_KV_GROUPS, tq, nbp), lambda i, s, pt: (i, 0, 0, 0)),
            pl.BlockSpec((None, tq, cw), lambda i, s, pt: (i, 0, 1)),
        ] + [_page_spec(li, k, npg, page) for k in range(npg)],
        out_specs=pl.BlockSpec((None, tq, qw), lambda i, s, pt: (i, 0, 0)),
        scratch_shapes=[pltpu.VMEM((NSA_KV_GROUPS, hpg * tq, LANE), F32),
                        pltpu.VMEM((NSA_KV_GROUPS, hpg * tq, LANE), F32),
                        pltpu.VMEM((NSA_KV_GROUPS, hpg * tq, HEAD_DIM), F32),
                        pltpu.VMEM((hpg * tq, npg * page), _prob_dtype(tq)),
                        pltpu.VMEM((npg * page, LANE), BF16)],
    )
    return pl.pallas_call(
        functools.partial(_slc_paged_body, npg=npg, tq=tq, hpg=hpg, past_len=past_len, page=page),
        grid_spec=grid_spec,
        out_shape=jax.ShapeDtypeStruct((b, tq, qw), F32),
        compiler_params=_cparams("parallel", "arbitrary"),
        name="slc_attend_paged",
    )(page_table, q, sel, kv4_new, *([cache] * npg))


def _win_body(q_ref, k_ref, v_ref, o_ref, l_ref, e_ref, *, qb, nb, span, pos0, kpos0, hpg):
    for u in range(nb):
        blk = pl.program_id(2) * nb + u
        q6 = jnp.concatenate([q_ref[u * qb:(u + 1) * qb, i * HEAD_DIM:(i + 1) * HEAD_DIM] for i in range(hpg)],
                             axis=0)
        q6 = (q6 * SCALE_LOG2E).astype(BF16)
        start = pl.multiple_of(blk * qb, qb)
        k = k_ref[pl.ds(start, span), :].astype(BF16)
        v = v_ref[pl.ds(start, span), :].astype(BF16)
        kpos = kpos0 + blk * qb + lax.broadcasted_iota(I32, (1, span), 1)
        qpos = pos0 + blk * qb + lax.broadcasted_iota(I32, (qb, 1), 0)
        dist = qpos - kpos
        allowed = (kpos >= 0) & (dist >= 0) & (dist < WINDOW)
        bias = jnp.where(allowed, 0.0, NEG)
        s_all = _dot_nt(q6, k)
        for i in range(hpg):
            rows = slice(i * qb, (i + 1) * qb)
            s = s_all[rows] + bias
            e = jnp.exp2(s - jnp.max(s, axis=-1, keepdims=True))
            l_ref[u, rows, :] = jnp.broadcast_to(jnp.sum(e, axis=-1, keepdims=True), (qb, LANE))
            e_ref[u, rows, :] = e.astype(e_ref.dtype)
        o = _dot(e_ref[u].astype(BF16), v) / l_ref[u]
        for i in range(hpg):
            o_ref[u * qb:(u + 1) * qb, i * HEAD_DIM:(i + 1) * HEAD_DIM] = o[i * qb:(i + 1) * qb]


def win_attend(q, kext, qb, span, pos0, kpos0, nb=1):
    b, t, qw = q.shape
    lk = kext.shape[1]
    gwid = qw // NSA_KV_GROUPS
    hpg = gwid // HEAD_DIM
    assert t % (qb * nb) == 0 and (t // qb - 1) * qb + span <= lk and qb % SUBLANE == 0 and span % SUBLANE == 0
    return pl.pallas_call(
        functools.partial(_win_body, qb=qb, nb=nb, span=span, pos0=pos0, kpos0=kpos0, hpg=hpg),
        grid=(b, NSA_KV_GROUPS, t // (qb * nb)),
        in_specs=[
            pl.BlockSpec((None, qb * nb, gwid), lambda i, g, j: (i, j, g)),
            pl.BlockSpec((None, lk, HEAD_DIM), lambda i, g, j: (i, 0, g)),
            pl.BlockSpec((None, lk, HEAD_DIM), lambda i, g, j: (i, 0, NSA_KV_GROUPS + g)),
        ],
        out_specs=pl.BlockSpec((None, qb * nb, gwid), lambda i, g, j: (i, j, g)),
        out_shape=jax.ShapeDtypeStruct((b, t, qw), F32),
        scratch_shapes=[pltpu.VMEM((nb, hpg * qb, LANE), F32),
                        pltpu.VMEM((nb, hpg * qb, span), _prob_dtype(qb))],
        compiler_params=_cparams("parallel", "parallel", "parallel"),
        name="win_attend",
    )(q, kext, kext)


def _row_tile(n, pref):
    t = min(n, pref)
    assert n % t == 0
    return t


def kernel(x_prompt, x_sample, mem_prompt, cache_mem_kv, cache_nsa_kv, cache_nsa_win, state_pool, page_table, g_norm_mix, g_norm_mlp, g_norm_mem, w_mem_kv, mem_qk_gain, w_out, w_mlp_up, w_mlp_down, w_in_pool, w_pool_group, pool_scale, w_in_nsa, nsa_gate_bias, nsa_qk_gain, cmp_pos, cmp_w1, cmp_w2):
    depth = g_norm_mix.shape[0]
    bp, tp, d = x_prompt.shape
    db, ts, _ = x_sample.shape
    n_pages = page_table.shape[1]
    page = cache_nsa_kv.shape[2]
    past_len = n_pages * page
    win_buf = cache_nsa_win.shape[2]
    tok_w = w_pool_group.shape[1] * w_pool_group.shape[2]
    n_heads = tok_w // HEAD_DIM
    n_gate = 3 * n_heads
    kvw = 4 * NSA_KV_GROUPS * HEAD_DIM
    winw = 2 * NSA_KV_GROUPS * HEAD_DIM
    assert win_buf == WINDOW and past_len % SEL_BLOCK == 0 and ts <= SUBLANE and tp % QBLOCK == 0

    tsp = SUBLANE
    xs = jnp.pad(x_sample, ((0, 0), (0, tsp - ts), (0, 0))).reshape(db * tsp, d)
    xp = x_prompt.reshape(bp * tp, d)
    n_p, n_s = bp * tp, db * tsp
    tm_p, tm_s = _row_tile(n_p, 256), n_s
    mem2d = mem_prompt.reshape(bp * mem_prompt.shape[1], d)
    mem_len = mem_prompt.shape[1]
    cache_view = cache_nsa_kv.reshape(cache_nsa_kv.shape[0], cache_nsa_kv.shape[1], page * PAGE_CHUNKS, HEAD_DIM)
    npg = min(16, n_pages)

    mem_kv_p, nsa_kv_p, nsa_kv_s, win_p, win_s, pool_p, pool_s = [], [], [], [], [], [], []
    mem_segs = ((2 * MEM_WIDTH, (1,) * MEM_HEADS + (None,) * MEM_HEADS),)
    pool_segs = ((tok_w, None), (MEM_WIDTH, None))
    nsa_segs = ((tok_w, (0,) * n_heads),
                (kvw, (None, None, None, None, 2, 2, None, None)),
                (winw, (3, 3, None, None)),
                (MEM_WIDTH, None),
                (LANE, None))

    wo_b = w_out.astype(BF16)
    wu_b = w_mlp_up.astype(BF16)
    wd_b = w_mlp_down.astype(BF16)
    wmem_b = w_mem_kv.astype(BF16)
    wpool_b = w_in_pool.astype(BF16)
    wg_b = w_pool_group.astype(BF16)
    kv_end = tok_w + kvw + winw
    wnsa_b = jnp.concatenate(
        [w_in_nsa[:, :, :kv_end], w_in_nsa[:, :, kv_end + n_gate:], w_in_nsa[:, :, kv_end:kv_end + n_gate],
         jnp.zeros(w_in_nsa.shape[:2] + (LANE - n_gate,), w_in_nsa.dtype)], axis=2).astype(BF16)
    mem_cache = cache_mem_kv.reshape(cache_mem_kv.shape[0], db, cache_mem_kv.shape[2] * 2 * MEM_HEADS, HEAD_DIM)

    for i in range(depth):
        li = i // 2
        (mkv,) = in_proj(mem2d, g_norm_mem[i], wmem_b, i, mem_qk_gain[i], mem_segs,
                         _row_tile(mem2d.shape[0], 256))
        mkv_p = mkv.reshape(bp, mem_len, 2 * MEM_WIDTH)
        mem_kv_p.append(mkv_p.reshape(bp, mem_len, 2, MEM_HEADS, HEAD_DIM))
        gated = None
        if i % 2 == 0:
            gains = jnp.ones((1, HEAD_DIM), F32)
            up, qmp = in_proj(xp, g_norm_mix[i], wpool_b, li, gains, pool_segs, tm_p)
            us, qms = in_proj(xs, g_norm_mix[i], wpool_b, li, gains, pool_segs, tm_s)
            up3 = up.reshape(bp, tp, tok_w)
            tok_p = pool_mix(up3, wg_b, li, pool_scale[li], 0, _row_tile(tp, 512)).reshape(n_p, tok_w)
            us3 = us.reshape(db, tsp, tok_w)[:, :ts]
            lead = _round_up(POOL_HIST + ts, SUBLANE) - (POOL_HIST + ts)
            ext = jnp.concatenate([state_pool[li], us3], axis=1)
            ext_pad = jnp.pad(ext, ((0, 0), (lead, 0), (0, 0)))
            l_ext = ext_pad.shape[1]
            tok_e = pool_mix(ext_pad, wg_b, li, pool_scale[li], past_len - POOL_HIST - lead, l_ext)
            tok_s = jnp.pad(tok_e[:, l_ext - ts:], ((0, 0), (0, tsp - ts), (0, 0))).reshape(n_s, tok_w)
            pool_p.append(up3[:, tp - POOL_HIST:])
            pool_s.append(ext[:, -POOL_HIST:])
        else:
            gains = nsa_qk_gain[li]
            bias = jnp.pad(nsa_gate_bias[li], (0, LANE - n_gate)).reshape(1, LANE)
            w1 = cmp_w1[li]
            half = CMP_STRIDE * HEAD_DIM
            w1k = jnp.concatenate([w1[0, :half], w1[0, half:]], axis=1).astype(BF16)
            w1v = jnp.concatenate([w1[1, :half], w1[1, half:]], axis=1).astype(BF16)
            w1b = w1.astype(BF16)
            w2b = cmp_w2[li].astype(BF16)

            q_p, kv4_p, wn_p, qmp, gt_p = in_proj(xp, g_norm_mix[i], wnsa_b, li, gains, nsa_segs, tm_p)
            q3 = q_p.reshape(bp, tp, tok_w)
            kv43 = kv4_p.reshape(bp, tp, kvw)
            wn3 = wn_p.reshape(bp, tp, winw)
            n_cmp = tp // CMP_STRIDE
            ncp = _round_up(n_cmp, LANE)
            ab = cmp_stage1_dense(kv43, w1k, w1v)
            ct = cmp_stage2([ab], cmp_pos[li], w1b, w2b, gains[1], ncp)
            oc_p, sel_p = cmp_attend_topk(q3, ct, 0, tp // SEL_BLOCK, _row_tile(tp, 512))
            os_p = slc_attend_prompt(q3, kv43, sel_p, _row_tile(tp, 512), _row_tile(tp, 512))
            kext = jnp.concatenate([jnp.zeros((bp, WINDOW, winw), F32), wn3], axis=1)
            ow_p = win_attend(q3, kext, QBLOCK, WINDOW + QBLOCK, 0, -WINDOW, nb=4 if tp % (4 * QBLOCK) == 0 else 1)
            tok_p = (oc_p.reshape(n_p, tok_w), os_p.reshape(n_p, tok_w), ow_p.reshape(n_p, tok_w))

            q_s, kv4_s, wn_s, qms, gt_s = in_proj(xs, g_norm_mix[i], wnsa_b, li, gains, nsa_segs, tm_s)
            qs3 = q_s.reshape(db, tsp, tok_w)
            kv4s3 = kv4_s.reshape(db, tsp, kvw)
            wns3 = wn_s.reshape(db, tsp, winw)
            new_rows = jnp.pad(kv4s3[:, :ts], ((0, 0), (0, page - ts), (0, 0)))
            n_rows = _round_up(past_len + ts, SEL_BLOCK)
            n_cmp_s = n_rows // CMP_STRIDE
            ncp_s = _round_up(n_cmp_s, LANE)
            ab_past = cmp_stage1_paged(cache_view, li, page_table, w1k, w1v, npg)
            ab_new = cmp_stage1_dense(new_rows, w1k, w1v)
            ct_s = cmp_stage2([ab_past, ab_new], cmp_pos[li], w1b, w2b, gains[1], ncp_s)
            oc_s, sel_s = cmp_attend_topk(qs3, ct_s, past_len, n_rows // SEL_BLOCK, tsp)
            os_s = slc_attend_paged(qs3, sel_s, kv4s3, cache_view, li, page_table, npg)
            wext = jnp.concatenate([cache_nsa_win[li].reshape(db, win_buf, winw), wns3[:, :ts]], axis=1)
            kext_s = jnp.pad(wext, ((0, 0), (0, tsp - ts), (0, 0)))
            ow_s = win_attend(qs3, kext_s, tsp, win_buf + tsp, past_len, past_len - win_buf)
            tok_s = (oc_s.reshape(n_s, tok_w), os_s.reshape(n_s, tok_w), ow_s.reshape(n_s, tok_w))
            gated = ((gt_p, bias), (gt_s, bias))

            nsa_kv_p.append(kv43.reshape(bp, tp, 4, NSA_KV_GROUPS, HEAD_DIM))
            nsa_kv_s.append(kv4s3[:, :ts].reshape(db, ts, 4, NSA_KV_GROUPS, HEAD_DIM))
            wlen = min(WINDOW, tp)
            win_p.append(wn3[:, tp - wlen:].reshape(bp, wlen, 2, NSA_KV_GROUPS, HEAD_DIM))
            win_s.append(wext[:, -win_buf:].reshape(db, win_buf, 2, NSA_KV_GROUPS, HEAD_DIM))

        hp = out_proj(tok_p, qmp, mkv_p, mem_qk_gain[i, 0], xp, wo_b, i, tm_p, tp, gates=gated and gated[0])
        hs = out_proj(tok_s, qms, mem_cache, mem_qk_gain[i, 0], xs, wo_b, i, tm_s, tsp, gates=gated and gated[1],
                      kv_layer=i)
        xp = mlp(hp, g_norm_mlp[i], wu_b, wd_b, i, _row_tile(n_p, 512), 1024)
        xs = mlp(hs, g_norm_mlp[i], wu_b, wd_b, i, tm_s, 1024)

    y_p = xp.reshape(bp, tp, d)
    y_s = xs.reshape(db, tsp, d)[:, :ts]
    return (y_p, y_s, jnp.stack(mem_kv_p), jnp.stack(nsa_kv_p), jnp.stack(nsa_kv_s),
            jnp.stack(win_p), jnp.stack(win_s), jnp.stack(pool_p), jnp.stack(pool_s))
```

```python
import functools

import jax
import jax.numpy as jnp
from jax import lax
from jax.experimental import pallas as pl
from jax.experimental.pallas import tpu as pltpu

F32 = jnp.float32
BF16 = jnp.bfloat16
I32 = jnp.int32

HEAD_DIM = 128
MEM_HEADS = 4
MEM_WIDTH = MEM_HEADS * HEAD_DIM
NSA_KV_GROUPS = 2
CMP_STRIDE = 16
CMP_BLOCK = 2 * CMP_STRIDE
SEL_BLOCK = 64
N_SEL = 16
WINDOW = 512
QBLOCK = 128
POOL_WINDOWS = (2, 4, 8, 16)
POOL_HIST = max(POOL_WINDOWS) - 1
POOL_HALO = 16
ATTN_SCALE = HEAD_DIM ** -0.5
SCALE_LOG2E = ATTN_SCALE * 1.4426950408889634
EPS = 1e-6
NEG = -1e30
BIG = 1e30
LOWEST = -3.0e38
MASK_OFF = 2.0 ** 20
SEL_SHIFT = SEL_BLOCK.bit_length() - 1
RATIO_SHIFT = (SEL_BLOCK // CMP_STRIDE).bit_length() - 1
assert 1 << SEL_SHIFT == SEL_BLOCK and 1 << RATIO_SHIFT == SEL_BLOCK // CMP_STRIDE

LANE = 128
SUBLANE = 8
VMEM_LIMIT_BYTES = 56 * 1024 * 1024


def _cparams(*sem):
    return pltpu.CompilerParams(dimension_semantics=sem, vmem_limit_bytes=VMEM_LIMIT_BYTES)


def _round_up(x, m):
    return (x + m - 1) // m * m


def _rms(x, gain):
    return x * lax.rsqrt(jnp.mean(x * x, axis=-1, keepdims=True) + EPS) * gain


def _dot(a, b):
    return jnp.dot(a, b, preferred_element_type=F32)


def _dot_nt(a, b):
    return lax.dot_general(a, b, (((1,), (1,)), ((), ())), preferred_element_type=F32)


def _in_proj_body(x_ref, g_ref, w_ref, gains_ref, *out_refs, segs):
    xb = _rms(x_ref[...], g_ref[...]).astype(BF16)
    col = 0
    for o_ref, (width, norms) in zip(out_refs, segs):
        y = _dot(xb, w_ref[:, col:col + width])
        if norms is None:
            o_ref[...] = y
        else:
            for c, gi in enumerate(norms):
                yc = y[:, c * LANE:(c + 1) * LANE]
                if gi is not None:
                    yc = _rms(yc, gains_ref[gi:gi + 1, :])
                o_ref[:, c * LANE:(c + 1) * LANE] = yc
        col += width


def in_proj(x2d, g, w, layer, gains, segs, tm):
    n, d = x2d.shape
    wtot = w.shape[2]
    assert n % tm == 0 and wtot == sum(s[0] for s in segs)
    return pl.pallas_call(
        functools.partial(_in_proj_body, segs=segs),
        grid=(n // tm,),
        in_specs=[
            pl.BlockSpec((tm, d), lambda i: (i, 0)),
            pl.BlockSpec((1, d), lambda i: (0, 0)),
            pl.BlockSpec((None, d, wtot), lambda i: (layer, 0, 0)),
            pl.BlockSpec(gains.shape, lambda i: (0, 0)),
        ],
        out_specs=[pl.BlockSpec((tm, s[0]), lambda i: (i, 0)) for s in segs],
        out_shape=[jax.ShapeDtypeStruct((n, s[0]), F32) for s in segs],
        compiler_params=_cparams("parallel"),
        name="in_proj",
    )(x2d, g.reshape(1, d), w, gains)


def _mem_attend(qm, kv_ref, gq_ref, row_major):
    chunks = 2 * MEM_HEADS
    m = kv_ref.shape[0] // chunks if row_major else kv_ref.shape[0]
    heads = []
    for h in range(MEM_HEADS):
        sl = slice(h * HEAD_DIM, (h + 1) * HEAD_DIM)
        q = _rms(qm[:, sl], gq_ref[...]).astype(BF16)
        if row_major:
            k = kv_ref[pl.ds(h, m, stride=chunks), :].astype(BF16)
            v = kv_ref[pl.ds(MEM_HEADS + h, m, stride=chunks), :].astype(BF16)
        else:
            k = kv_ref[:, sl].astype(BF16)
            v = kv_ref[:, MEM_WIDTH + h * HEAD_DIM:MEM_WIDTH + (h + 1) * HEAD_DIM].astype(BF16)
        s = _dot_nt(q, k) * ATTN_SCALE
        e = jnp.exp(s - jnp.max(s, axis=-1, keepdims=True))
        heads.append(_dot(e.astype(BF16), v) / jnp.sum(e, axis=-1, keepdims=True))
    return jnp.concatenate(heads, axis=1)


def _out_proj_body(*refs, gated, row_major, rows_per_batch):
    if gated:
        gt_ref, bias_ref, oc_ref, os_ref, ow_ref = refs[:5]
        refs = refs[5:]
        gs = jax.nn.sigmoid(gt_ref[...] + bias_ref[...])
        heads = []
        for h in range(oc_ref.shape[1] // HEAD_DIM):
            sl = slice(h * HEAD_DIM, (h + 1) * HEAD_DIM)
            heads.append(gs[:, 3 * h:3 * h + 1] * oc_ref[:, sl] + gs[:, 3 * h + 1:3 * h + 2] * os_ref[:, sl]
                         + gs[:, 3 * h + 2:3 * h + 3] * ow_ref[:, sl])
        tok = jnp.concatenate(heads, axis=1)
    else:
        tok = refs[0][...]
        refs = refs[1:]
    qm_ref, kv_ref, gq_ref, x_ref, w_ref, o_ref = refs
    tw = tok.shape[1]
    acc = _dot(tok.astype(BF16), w_ref[0:tw, :])
    if len(kv_ref.shape) == 2:
        mem = _mem_attend(qm_ref[...], kv_ref, gq_ref, row_major)
    else:
        mem = jnp.concatenate([_mem_attend(qm_ref[b * rows_per_batch:(b + 1) * rows_per_batch, :], kv_ref.at[b],
                                           gq_ref, row_major) for b in range(kv_ref.shape[0])], axis=0)
    acc = acc + _dot(mem.astype(BF16), w_ref[tw:, :])
    o_ref[...] = x_ref[...] + acc


def out_proj(tok, qm, mkv, gq, x, w, layer, tm, rows_per_batch, gates=None, kv_layer=None):
    n, d = x.shape
    toks = tok if gates is not None else (tok,)
    tw = toks[0].shape[1]
    assert rows_per_batch % tm == 0 or (tm == n and tm % rows_per_batch == 0 and rows_per_batch % SUBLANE == 0)
    tiles = max(rows_per_batch // tm, 1)
    one = rows_per_batch % tm == 0
    row = lambda width: pl.BlockSpec((tm, width), lambda i: (i, 0))
    lead_specs, lead_args = [], []
    if gates is not None:
        lead_specs = [row(LANE), pl.BlockSpec((1, LANE), lambda i: (0, 0))]
        lead_args = list(gates)
    if kv_layer is None:
        kv_spec = (pl.BlockSpec((None,) + mkv.shape[1:], lambda i: (i // tiles, 0, 0)) if one else
                   pl.BlockSpec(mkv.shape, lambda i: (0, 0, 0)))
    else:
        kv_spec = (pl.BlockSpec((None, None) + mkv.shape[2:], lambda i: (kv_layer, i // tiles, 0, 0)) if one else
                   pl.BlockSpec((None,) + mkv.shape[1:], lambda i: (kv_layer, 0, 0, 0)))
    return pl.pallas_call(
        functools.partial(_out_proj_body, gated=gates is not None, row_major=kv_layer is not None,
                          rows_per_batch=rows_per_batch),
        grid=(n // tm,),
        in_specs=lead_specs + [row(tw) for _ in toks] + [
            row(MEM_WIDTH), kv_spec, pl.BlockSpec((1, HEAD_DIM), lambda i: (0, 0)), row(d),
            pl.BlockSpec((None, tw + MEM_WIDTH, d), lambda i: (layer, 0, 0)),
        ],
        out_specs=row(d),
        out_shape=jax.ShapeDtypeStruct((n, d), F32),
        compiler_params=_cparams("parallel"),
        name="out_proj",
    )(*lead_args, *toks, qm, mkv, gq.reshape(1, HEAD_DIM), x, w)


def _mlp_body(h_ref, g_ref, wu_ref, wd_ref, o_ref, xn_ref, acc_ref):
    k = pl.program_id(1)

    @pl.when(k == 0)
    def _():
        xn_ref[...] = _rms(h_ref[...], g_ref[...]).astype(BF16)
        acc_ref[...] = jnp.zeros_like(acc_ref)

    a = jnp.maximum(_dot(xn_ref[...], wu_ref[...]), 0.0)
    acc_ref[...] += _dot((a * a).astype(BF16), wd_ref[...])

    @pl.when(k == pl.num_programs(1) - 1)
    def _():
        o_ref[...] = h_ref[...] + acc_ref[...]


def mlp(h, g, w_up, w_down, layer, tm, tf):
    n, d = h.shape
    ff = w_up.shape[2]
    return pl.pallas_call(
        _mlp_body,
        grid=(n // tm, ff // tf),
        in_specs=[
            pl.BlockSpec((tm, d), lambda i, k: (i, 0)),
            pl.BlockSpec((1, d), lambda i, k: (0, 0)),
            pl.BlockSpec((None, d, tf), lambda i, k: (layer, 0, k)),
            pl.BlockSpec((None, tf, d), lambda i, k: (layer, k, 0)),
        ],
        out_specs=pl.BlockSpec((tm, d), lambda i, k: (i, 0)),
        out_shape=jax.ShapeDtypeStruct((n, d), F32),
        scratch_shapes=[pltpu.VMEM((tm, d), BF16), pltpu.VMEM((tm, d), F32)],
        compiler_params=_cparams("parallel", "arbitrary"),
        name="mlp",
    )(h, g.reshape(1, d), w_up, w_down)


def _pool_body(cur_ref, halo_ref, wg_ref, sc_ref, o_ref, ext_ref, *, pos0, tt, gw):
    t = pl.program_id(1)
    ext_ref[0:POOL_HALO, :] = jnp.where(t == 0, 0.0, halo_ref[...])
    ext_ref[POOL_HALO:POOL_HALO + tt, :] = cur_ref[...]
    row = t * tt + lax.broadcasted_iota(I32, (tt, 1), 0)
    for gi, w in enumerate(POOL_WINDOWS):
        c0, c1 = gi * gw, (gi + 1) * gw
        x = ext_ref[POOL_HALO:POOL_HALO + tt, c0:c1]
        s = x
        for dd in range(1, w):
            s = s + ext_ref[POOL_HALO - dd:POOL_HALO - dd + tt, c0:c1]
        cnt = jnp.minimum(pos0 + row + 1, w).astype(F32)
        pooled = s / cnt - x
        o_ref[:, c0:c1] = _dot(pooled.astype(BF16), wg_ref[gi]) * sc_ref[:, c0:c1]


def pool_mix(u, w_group, layer, scale, pos0, tt):
    b, l, c = u.shape
    _, ng, gw, _ = w_group.shape
    assert l % tt == 0 and tt % POOL_HALO == 0 or l == tt
    halo_per_tile = tt // POOL_HALO if tt % POOL_HALO == 0 else 0
    return pl.pallas_call(
        functools.partial(_pool_body, pos0=pos0, tt=tt, gw=gw),
        grid=(b, l // tt),
        in_specs=[
            pl.BlockSpec((None, tt, c), lambda i, j: (i, j, 0)),
            pl.BlockSpec((None, POOL_HALO, c), lambda i, j: (i, jnp.maximum(j * halo_per_tile - 1, 0), 0)),
            pl.BlockSpec((None, ng, gw, gw), lambda i, j: (layer, 0, 0, 0)),
            pl.BlockSpec((1, c), lambda i, j: (0, 0)),
        ],
        out_specs=pl.BlockSpec((None, tt, c), lambda i, j: (i, j, 0)),
        out_shape=jax.ShapeDtypeStruct((b, l, c), F32),
        scratch_shapes=[pltpu.VMEM((POOL_HALO + tt, c), F32)],
        compiler_params=_cparams("parallel", "parallel"),
        name="pool_mix",
    )(u, u, w_group, scale.reshape(1, c))


def _cmp1_compute(page_refs, wk_ref, wv_ref, o_ref, stage_ref, rows):
    m = len(page_refs) * rows // CMP_STRIDE
    for slot, w_ref in ((0, wk_ref), (1, wv_ref)):
        xs = []
        for g in range(NSA_KV_GROUPS):
            j = slot * NSA_KV_GROUPS + g
            for p, pr in enumerate(page_refs):
                stage_ref[p * rows:(p + 1) * rows, :] = pr[:, j * HEAD_DIM:(j + 1) * HEAD_DIM]
            pieces = [stage_ref[pl.ds(r, m, stride=CMP_STRIDE), :] for r in range(CMP_STRIDE)]
            xs.append(jnp.concatenate(pieces, axis=1).astype(BF16))
        y = _dot(jnp.concatenate(xs, axis=0), w_ref[...])
        for g in range(NSA_KV_GROUPS):
            j = slot * NSA_KV_GROUPS + g
            o_ref[:, j * 2 * HEAD_DIM:(j + 1) * 2 * HEAD_DIM] = y[g * m:(g + 1) * m]


def _cmp1_dense_body(x_ref, wk_ref, wv_ref, o_ref, stage_ref, *, rows):
    _cmp1_compute([x_ref], wk_ref, wv_ref, o_ref, stage_ref, rows)


def cmp_stage1_dense(rows4, w1k, w1v, lane_block=0):
    b, l, _ = rows4.shape
    nh = l // CMP_STRIDE
    cw = 2 * NSA_KV_GROUPS * HEAD_DIM
    return pl.pallas_call(
        functools.partial(_cmp1_dense_body, rows=l),
        grid=(b,),
        in_specs=[
            pl.BlockSpec((None, l, cw), lambda i: (i, 0, lane_block)),
            pl.BlockSpec(w1k.shape, lambda i: (0, 0)),
            pl.BlockSpec(w1v.shape, lambda i: (0, 0)),
        ],
        out_specs=pl.BlockSpec((None, nh, 2 * cw), lambda i: (i, 0, 0)),
        out_shape=jax.ShapeDtypeStruct((b, nh, 2 * cw), F32),
        scratch_shapes=[pltpu.VMEM((l, HEAD_DIM), F32)],
        compiler_params=_cparams("parallel"),
        name="cmp_stage1_dense",
    )(rows4, w1k, w1v)


PAGE_CHUNKS = 4 * NSA_KV_GROUPS


def _cmp1_paged_body(pt_ref, *refs, npg, page):
    del pt_ref
    page_refs = refs[:npg]
    wk_ref, wv_ref, o_ref = refs[npg:npg + 3]
    nh = page // CMP_STRIDE
    m = npg * nh
    for slot, w_ref in ((0, wk_ref), (1, wv_ref)):
        xs = []
        for g in range(NSA_KV_GROUPS):
            j = slot * NSA_KV_GROUPS + g
            per_page = []
            for pr in page_refs:
                pieces = [pr[pl.ds(r * PAGE_CHUNKS + j, nh, stride=CMP_STRIDE * PAGE_CHUNKS), :]
                          for r in range(CMP_STRIDE)]
                per_page.append(jnp.concatenate(pieces, axis=1))
            xs.append(jnp.concatenate(per_page, axis=0).astype(BF16))
        y = _dot(jnp.concatenate(xs, axis=0), w_ref[...])
        for g in range(NSA_KV_GROUPS):
            j = slot * NSA_KV_GROUPS + g
            o_ref[:, j * 2 * HEAD_DIM:(j + 1) * 2 * HEAD_DIM] = y[g * m:(g + 1) * m]


def _page_spec(li, k, npg, page):
    return pl.BlockSpec((None, None, page * PAGE_CHUNKS, HEAD_DIM),
                        lambda i, s, pt: (li, pt[i, s * npg + k], 0, 0))


def cmp_stage1_paged(cache, li, page_table, w1k, w1v, npg):
    b, n_pages = page_table.shape
    page = cache.shape[2] // PAGE_CHUNKS
    nh = page // CMP_STRIDE
    cw = 2 * NSA_KV_GROUPS * HEAD_DIM
    assert n_pages % npg == 0
    grid_spec = pltpu.PrefetchScalarGridSpec(
        num_scalar_prefetch=1,
        grid=(b, n_pages // npg),
        in_specs=[_page_spec(li, k, npg, page) for k in range(npg)] + [
            pl.BlockSpec(w1k.shape, lambda i, s, pt: (0, 0)),
            pl.BlockSpec(w1v.shape, lambda i, s, pt: (0, 0)),
        ],
        out_specs=pl.BlockSpec((None, npg * nh, 2 * cw), lambda i, s, pt: (i, s, 0)),
    )
    return pl.pallas_call(
        functools.partial(_cmp1_paged_body, npg=npg, page=page),
        grid_spec=grid_spec,
        out_shape=jax.ShapeDtypeStruct((b, n_pages * nh, 2 * cw), F32),
        compiler_params=_cparams("parallel", "arbitrary"),
        name="cmp_stage1_paged",
    )(page_table, *([cache] * npg), w1k, w1v)


def _cmp2_body(*refs, n_ab, ncp):
    ab_refs = refs[:n_ab]
    pos_ref, w1_ref, w2_ref, gk_ref, o_ref = refs[n_ab:]
    n_have = sum(r.shape[0] for r in ab_refs)
    fill = [jnp.zeros((ncp + SUBLANE - n_have, HEAD_DIM), F32)]
    for slot in range(2):
        posflat = jnp.concatenate([pos_ref[slot, r:r + 1, :] for r in range(CMP_BLOCK)], axis=1)
        posb = jnp.broadcast_to(posflat, (SUBLANE, CMP_BLOCK * HEAD_DIM)).astype(BF16)
        posc = _dot(posb, w1_ref[slot])[0:1]
        for g in range(NSA_KV_GROUPS):
            j = slot * NSA_KV_GROUPS + g
            a_lanes = slice(j * 2 * HEAD_DIM, j * 2 * HEAD_DIM + HEAD_DIM)
            b_lanes = slice(j * 2 * HEAD_DIM + HEAD_DIM, (j + 1) * 2 * HEAD_DIM)
            a = jnp.concatenate([r[:, a_lanes] for r in ab_refs] + fill, axis=0)
            bm = jnp.concatenate([r[:, b_lanes] for r in ab_refs] + fill, axis=0)
            bm_next = pltpu.roll(bm, ncp + SUBLANE - 1, axis=0)
            hid = jax.nn.gelu(a[0:ncp] + bm_next[0:ncp] + posc)
            out = _dot(hid.astype(BF16), w2_ref[slot])
            if slot == 0:
                out = _rms(out, gk_ref[...])
            o_ref[:, j * HEAD_DIM:(j + 1) * HEAD_DIM] = out


def cmp_stage2(abs_, pos, w1, w2, gk, ncp):
    b, _, w = abs_[0].shape
    assert all(a.shape[1] % SUBLANE == 0 for a in abs_) and sum(a.shape[1] for a in abs_) <= ncp + SUBLANE
    cw = 2 * NSA_KV_GROUPS * HEAD_DIM
    return pl.pallas_call(
        functools.partial(_cmp2_body, n_ab=len(abs_), ncp=ncp),
        grid=(b,),
        in_specs=[pl.BlockSpec((None, a.shape[1], w), lambda i: (i, 0, 0)) for a in abs_] + [
            pl.BlockSpec(pos.shape, lambda i: (0, 0, 0)),
            pl.BlockSpec(w1.shape, lambda i: (0, 0, 0)),
            pl.BlockSpec(w2.shape, lambda i: (0, 0, 0)),
            pl.BlockSpec((1, HEAD_DIM), lambda i: (0, 0)),
        ],
        out_specs=pl.BlockSpec((None, ncp, cw), lambda i: (i, 0, 0)),
        out_shape=jax.ShapeDtypeStruct((b, ncp, cw), F32),
        compiler_params=_cparams("parallel"),
        name="cmp_stage2",
    )(*abs_, pos, w1, w2, gk.reshape(1, HEAD_DIM))


def _cmp_attn_body(q_ref, ct_ref, o_ref, sel_ref, st_ref, *, pos0, tq, ncp, n_blk, n_sel, hpg):
    t = pl.program_id(1)
    ng = NSA_KV_GROUPS
    gwid = hpg * HEAD_DIM
    ratio = SEL_BLOCK // CMP_STRIDE
    nbp = ncp // ratio
    q0 = pos0 + t * tq
    qpos = q0 + lax.broadcasted_iota(I32, (tq, 1), 0)
    cidx = lax.broadcasted_iota(I32, (1, ncp), 1)
    cmask = (cidx * CMP_STRIDE + (CMP_BLOCK - 1)) <= qpos
    bias = jnp.where(cmask, 0.0, NEG)
    rp = _round_up(tq, LANE)
    nbpp = sel_ref.shape[2]
    jb = lax.broadcasted_iota(I32, (nbp, 1), 0)
    jf = jb.astype(F32)
    cur = (q0 + lax.broadcasted_iota(I32, (1, rp), 1)) >> SEL_SHIFT
    forced = (jb == 0) | (jb == cur) | (jb == cur - 1)
    for g in range(ng):
        q6 = _stack_heads(q_ref, g * gwid, hpg, ATTN_SCALE)
        kc = ct_ref[:, g * HEAD_DIM:(g + 1) * HEAD_DIM].astype(BF16)
        vc = ct_ref[:, (ng + g) * HEAD_DIM:(ng + g + 1) * HEAD_DIM].astype(BF16)
        s_all = _dot_nt(q6, kc)
        imp = jnp.zeros((tq, ncp), F32)
        probs = []
        for i in range(hpg):
            s = s_all[i * tq:(i + 1) * tq] + bias
            e = jnp.exp(s - jnp.max(s, axis=-1, keepdims=True))
            p = jnp.where(cmask, e / jnp.sum(e, axis=-1, keepdims=True), 0.0)
            imp = imp + p
            probs.append(p)
        o = _dot(jnp.concatenate(probs, axis=0).astype(BF16), vc)
        for i in range(hpg):
            o_ref[:, g * gwid + i * HEAD_DIM:g * gwid + (i + 1) * HEAD_DIM] = o[i * tq:(i + 1) * tq]
        imp2 = imp + jnp.where(cidx == 0, 0.0, pltpu.roll(imp, 1, axis=1))
        t1 = imp2 + pltpu.roll(imp2, ncp - 1, axis=1)
        ps = t1 + pltpu.roll(t1, ncp - 2, axis=1)

        if rp > tq:
            ps = jnp.concatenate([ps, jnp.zeros((rp - tq, ncp), F32)], axis=0)
        ps_t = ps.T
        tiles = []
        for h in range(rp // LANE):
            st = st_ref.at[g * (rp // LANE) + h]
            st[...] = ps_t[:, h * LANE:(h + 1) * LANE]
            tiles.append(st[pl.ds(0, nbp, stride=ratio), :])
        ps_blk = jnp.concatenate(tiles, axis=1)
        score = jnp.where(forced, BIG, jnp.where(jb <= cur, ps_blk, NEG))
        score = jnp.where(jb < n_blk, score, LOWEST)
        sel_t = jnp.zeros((nbp, rp), F32)
        for _ in range(n_sel):
            m = jnp.max(score, axis=0, keepdims=True)
            first = jnp.min(jnp.where(score == m, jf, float(nbp)), axis=0, keepdims=True)
            pick = jf == first
            sel_t = jnp.where(pick & (m > 0.5 * NEG), 1.0, sel_t)
            score = jnp.where(pick, LOWEST, score)
        if nbpp > nbp:
            sel_t = jnp.concatenate([sel_t, jnp.zeros((nbpp - nbp, rp), F32)], axis=0)
        sel_ref[g] = sel_t.T[0:tq]


def cmp_attend_topk(q, ct, pos0, n_blk, tq):
    b, t, qw = q.shape
    ncp = ct.shape[1]
    gwid = qw // NSA_KV_GROUPS
    hpg = gwid // HEAD_DIM
    ratio = SEL_BLOCK // CMP_STRIDE
    assert ncp % LANE == 0 and t % tq == 0 and tq & (tq - 1) == 0 and tq % SUBLANE == 0
    nbpp = _round_up(ncp // ratio, LANE)
    n_sel = min(N_SEL, n_blk)
    return pl.pallas_call(
        functools.partial(_cmp_attn_body, pos0=pos0, tq=tq, ncp=ncp, n_blk=n_blk, n_sel=n_sel, hpg=hpg),
        grid=(b, t // tq),
        in_specs=[
            pl.BlockSpec((None, tq, qw), lambda i, j: (i, j, 0)),
            pl.BlockSpec((None, ncp, ct.shape[2]), lambda i, j: (i, 0, 0)),
        ],
        out_specs=[
            pl.BlockSpec((None, tq, qw), lambda i, j: (i, j, 0)),
            pl.BlockSpec((None, NSA_KV_GROUPS, tq, nbpp), lambda i, j: (i, 0, j, 0)),
        ],
        out_shape=[
            jax.ShapeDtypeStruct((b, t, qw), F32),
            jax.ShapeDtypeStruct((b, NSA_KV_GROUPS, t, nbpp), F32),
        ],
        scratch_shapes=[pltpu.VMEM((NSA_KV_GROUPS * (_round_up(tq, LANE) // LANE), ncp, LANE), F32)],
        compiler_params=_cparams("parallel", "parallel"),
        name="cmp_attend_topk",
    )(q, ct)


def _stack_heads(q_ref, lane0, hpg, scale=None):
    q = jnp.concatenate([q_ref[:, lane0 + i * HEAD_DIM:lane0 + (i + 1) * HEAD_DIM] for i in range(hpg)], axis=0)
    return (q if scale is None else q * scale).astype(BF16)


def _prob_dtype(rows_per_head):
    return BF16 if rows_per_head % (2 * SUBLANE) == 0 else F32


def _flash_init(m_ref, l_ref, acc_ref):
    m_ref[...] = jnp.full_like(m_ref, NEG)
    l_ref[...] = jnp.zeros_like(l_ref)
    acc_ref[...] = jnp.zeros_like(acc_ref)


def _flash_tile(q6, k, v, bias, m_ref, l_ref, acc_ref, e_ref, hpg):
    tk = k.shape[0]
    tq = q6.shape[0] // hpg
    assert tk % LANE == 0 and acc_ref.shape[1] == LANE
    s_all = _dot_nt(q6, k)
    for i in range(hpg):
        rows = slice(i * tq, (i + 1) * tq)
        s = s_all[rows] if bias is None else s_all[rows] + bias
        m_old = m_ref[rows, :]
        m_new = jnp.maximum(m_old, jnp.max(s, axis=-1, keepdims=True))
        e = jnp.exp2(s - jnp.concatenate([m_new] * (tk // LANE), axis=1))
        alpha = jnp.exp2(m_old - m_new)
        l_ref[rows, :] = alpha * l_ref[rows, :] + jnp.sum(e, axis=-1, keepdims=True)
        acc_ref[rows, :] = alpha * acc_ref[rows, :]
        m_ref[rows, :] = m_new
        e_ref[rows, :] = e.astype(e_ref.dtype)
    acc_ref[...] += _dot(e_ref[...].astype(BF16), v)


def _slc_prompt_body(q_ref, k_ref, v_ref, sel_ref, o_ref, kb_ref, vb_ref, m_ref, l_ref, acc_ref, e_ref,
                     *, tq, tk, hpg):
    t = pl.program_id(2)

    n_keys = k_ref.shape[0]
    nbp = sel_ref.shape[1]

    @pl.when(t == 0)
    def _():
        kb_ref[:, 0:HEAD_DIM] = k_ref[...].astype(BF16)
        key_blk = lax.broadcasted_iota(I32, (n_keys, nbp), 0) >> SEL_SHIFT
        kb_ref[:, HEAD_DIM:] = (key_blk == lax.broadcasted_iota(I32, (n_keys, nbp), 1)).astype(BF16)
        vb_ref[...] = v_ref[...].astype(BF16)

    off = ((sel_ref[...] - 1.0) * MASK_OFF).astype(BF16)
    q6 = jnp.concatenate([_stack_heads(q_ref, 0, hpg, SCALE_LOG2E), jnp.concatenate([off] * hpg, axis=0)], axis=1)
    qpos = t * tq + lax.broadcasted_iota(I32, (tq, 1), 0)
    _flash_init(m_ref, l_ref, acc_ref)

    def tile(kt, causal):
        k0 = pl.multiple_of(kt * tk, tk)
        bias = None
        if causal:
            kpos = k0 + lax.broadcasted_iota(I32, (1, tk), 1)
            bias = jnp.where(kpos <= qpos, 0.0, NEG)
        _flash_tile(q6, kb_ref[pl.ds(k0, tk), :], vb_ref[pl.ds(k0, tk), :], bias, m_ref, l_ref, acc_ref, e_ref, hpg)

    def below_diagonal(kt, carry):
        tile(kt, False)
        return carry

    def on_diagonal(kt, carry):
        tile(kt, True)
        return carry

    n_below = (t * tq + 1) // tk
    lax.fori_loop(0, n_below, below_diagonal, 0)
    lax.fori_loop(n_below, (t * tq + tq - 1) // tk + 1, on_diagonal, 0)
    o = acc_ref[...] / l_ref[...]
    for i in range(hpg):
        o_ref[:, i * HEAD_DIM:(i + 1) * HEAD_DIM] = o[i * tq:(i + 1) * tq]


def slc_attend_prompt(q, kv4, sel, tq, tk):
    b, t, qw = q.shape
    gwid = qw // NSA_KV_GROUPS
    hpg = gwid // HEAD_DIM
    nbp = sel.shape[3]
    assert t % tq == 0 and t % tk == 0
    return pl.pallas_call(
        functools.partial(_slc_prompt_body, tq=tq, tk=tk, hpg=hpg),
        grid=(b, NSA_KV_GROUPS, t // tq),
        in_specs=[
            pl.BlockSpec((None, tq, gwid), lambda i, g, j: (i, j, g)),
            pl.BlockSpec((None, t, HEAD_DIM), lambda i, g, j: (i, 0, 2 * NSA_KV_GROUPS + g)),
            pl.BlockSpec((None, t, HEAD_DIM), lambda i, g, j: (i, 0, 3 * NSA_KV_GROUPS + g)),
            pl.BlockSpec((None, None, tq, nbp), lambda i, g, j: (i, g, j, 0)),
        ],
        out_specs=pl.BlockSpec((None, tq, gwid), lambda i, g, j: (i, j, g)),
        out_shape=jax.ShapeDtypeStruct((b, t, qw), F32),
        scratch_shapes=[pltpu.VMEM((t, HEAD_DIM + nbp), BF16), pltpu.VMEM((t, HEAD_DIM), BF16),
                        pltpu.VMEM((hpg * tq, LANE), F32), pltpu.VMEM((hpg * tq, LANE), F32),
                        pltpu.VMEM((hpg * tq, HEAD_DIM), F32), pltpu.VMEM((hpg * tq, tk), BF16)],
        compiler_params=_cparams("parallel", "parallel", "arbitrary"),
        name="slc_attend_prompt",
    )(q, kv4, kv4, sel)


def _slc_paged_body(pt_ref, q_ref, sel_ref, new_ref, *refs, npg, tq, hpg, past_len, page):
    del pt_ref
    page_refs = refs[:npg]
    o_ref, m_ref, l_ref, acc_ref, e_ref, oh_ref = refs[npg:]
    s_id = pl.program_id(1)
    gwid = hpg * HEAD_DIM
    qpos = past_len + lax.broadcasted_iota(I32, (tq, 1), 0)
    k_chunk, v_chunk = 2 * NSA_KV_GROUPS, 3 * NSA_KV_GROUPS

    tk = npg * page
    nbp = sel_ref.shape[2]

    @pl.when(s_id == 0)
    def _():
        _flash_init(m_ref, l_ref, acc_ref)
        rel_blk = lax.broadcasted_iota(I32, (tk, LANE), 0) >> SEL_SHIFT
        oh_ref[...] = (rel_blk == lax.broadcasted_iota(I32, (tk, LANE), 1)).astype(BF16)

    blk0 = s_id * (tk // SEL_BLOCK)
    pick = (lax.broadcasted_iota(I32, (nbp, LANE), 0) == blk0 + lax.broadcasted_iota(I32, (nbp, LANE), 1)).astype(BF16)
    for g in range(NSA_KV_GROUPS):
        off = ((_dot(sel_ref[g].astype(BF16), pick) - 1.0) * MASK_OFF).astype(BF16)
        q6 = jnp.concatenate([_stack_heads(q_ref, g * gwid, hpg, SCALE_LOG2E), jnp.concatenate([off] * hpg, axis=0)],
                             axis=1)
        k = jnp.concatenate([pr[pl.ds(k_chunk + g, page, stride=PAGE_CHUNKS), :] for pr in page_refs],
                            axis=0).astype(BF16)
        v = jnp.concatenate([pr[pl.ds(v_chunk + g, page, stride=PAGE_CHUNKS), :] for pr in page_refs],
                            axis=0).astype(BF16)
        _flash_tile(q6, jnp.concatenate([k, oh_ref[...]], axis=1), v, None, m_ref.at[g], l_ref.at[g], acc_ref.at[g],
                    e_ref, hpg)

    @pl.when(s_id == pl.num_programs(1) - 1)
    def _():
        cur = past_len // SEL_BLOCK
        nn = new_ref.shape[0]
        npos = past_len + lax.broadcasted_iota(I32, (1, LANE), 1)
        fill = jnp.zeros((LANE - nn, HEAD_DIM), F32)
        for g in range(NSA_KV_GROUPS):
            q6 = _stack_heads(q_ref, g * gwid, hpg, SCALE_LOG2E)
            k = jnp.concatenate([new_ref[:, g * HEAD_DIM:(g + 1) * HEAD_DIM], fill], axis=0).astype(BF16)
            v = jnp.concatenate([new_ref[:, (NSA_KV_GROUPS + g) * HEAD_DIM:(NSA_KV_GROUPS + g + 1) * HEAD_DIM],
                                 fill], axis=0).astype(BF16)
            allowed = (sel_ref[g][:, cur:cur + 1] > 0.5) & (npos <= qpos) & (npos < past_len + nn)
            _flash_tile(q6, k, v, jnp.where(allowed, 0.0, NEG), m_ref.at[g], l_ref.at[g], acc_ref.at[g],
                        e_ref.at[:, 0:LANE], hpg)
            o = acc_ref[g] / l_ref[g]
            for i in range(hpg):
                o_ref[:, g * gwid + i * HEAD_DIM:g * gwid + (i + 1) * HEAD_DIM] = o[i * tq:(i + 1) * tq]


def slc_attend_paged(q, sel, kv4_new, cache, li, page_table, npg):
    b, tq, qw = q.shape
    n_pages = page_table.shape[1]
    page = cache.shape[2] // PAGE_CHUNKS
    past_len = n_pages * page
    gwid = qw // NSA_KV_GROUPS
    hpg = gwid // HEAD_DIM
    nbp = sel.shape[3]
    cw = 2 * NSA_KV_GROUPS * HEAD_DIM
    assert n_pages % npg == 0 and past_len % SEL_BLOCK == 0 and tq <= SEL_BLOCK
    assert (npg * page) % SEL_BLOCK == 0 and npg * page // SEL_BLOCK <= LANE
    grid_spec = pltpu.PrefetchScalarGridSpec(
        num_scalar_prefetch=1,
        grid=(b, n_pages // npg),
        in_specs=[
            pl.BlockSpec((None, tq, qw), lambda i, s, pt: (i, 0, 0)),
            pl.BlockSpec((None, NSA_KV_GROUPS, tq, nbp), lambda i, s, pt: (i, 0, 0, 0)),
            pl.BlockSpec((None, tq, cw), lambda i, s, pt: (i, 0, 1)),
        ] + [_page_spec(li, k, npg, page) for k in range(npg)],
        out_specs=pl.BlockSpec((None, tq, qw), lambda i, s, pt: (i, 0, 0)),
        scratch_shapes=[pltpu.VMEM((NSA_KV_GROUPS, hpg * tq, LANE), F32),
                        pltpu.VMEM((NSA_KV_GROUPS, hpg * tq, LANE), F32),
                        pltpu.VMEM((NSA_KV_GROUPS, hpg * tq, HEAD_DIM), F32),
                        pltpu.VMEM((hpg * tq, npg * page), _prob_dtype(tq)),
                        pltpu.VMEM((npg * page, LANE), BF16)],
    )
    return pl.pallas_call(
        functools.partial(_slc_paged_body, npg=npg, tq=tq, hpg=hpg, past_len=past_len, page=page),
        grid_spec=grid_spec,
        out_shape=jax.ShapeDtypeStruct((b, tq, qw), F32),
        compiler_params=_cparams("parallel", "arbitrary"),
        name="slc_attend_paged",
    )(page_table, q, sel, kv4_new, *([cache] * npg))


def _win_body(q_ref, k_ref, v_ref, o_ref, l_ref, e_ref, *, qb, nb, span, pos0, kpos0, hpg):
    for u in range(nb):
        blk = pl.program_id(2) * nb + u
        q6 = jnp.concatenate([q_ref[u * qb:(u + 1) * qb, i * HEAD_DIM:(i + 1) * HEAD_DIM] for i in range(hpg)],
                             axis=0)
        q6 = (q6 * SCALE_LOG2E).astype(BF16)
        start = pl.multiple_of(blk * qb, qb)
        k = k_ref[pl.ds(start, span), :].astype(BF16)
        v = v_ref[pl.ds(start, span), :].astype(BF16)
        kpos = kpos0 + blk * qb + lax.broadcasted_iota(I32, (1, span), 1)
        qpos = pos0 + blk * qb + lax.broadcasted_iota(I32, (qb, 1), 0)
        dist = qpos - kpos
        allowed = (kpos >= 0) & (dist >= 0) & (dist < WINDOW)
        bias = jnp.where(allowed, 0.0, NEG)
        s_all = _dot_nt(q6, k)
        for i in range(hpg):
            rows = slice(i * qb, (i + 1) * qb)
            s = s_all[rows] + bias
            e = jnp.exp2(s - jnp.max(s, axis=-1, keepdims=True))
            l_ref[u, rows, :] = jnp.broadcast_to(jnp.sum(e, axis=-1, keepdims=True), (qb, LANE))
            e_ref[u, rows, :] = e.astype(e_ref.dtype)
        o = _dot(e_ref[u].astype(BF16), v) / l_ref[u]
        for i in range(hpg):
            o_ref[u * qb:(u + 1) * qb, i * HEAD_DIM:(i + 1) * HEAD_DIM] = o[i * qb:(i + 1) * qb]


def win_attend(q, kext, qb, span, pos0, kpos0, nb=1):
    b, t, qw = q.shape
    lk = kext.shape[1]
    gwid = qw // NSA_KV_GROUPS
    hpg = gwid // HEAD_DIM
    assert t % (qb * nb) == 0 and (t // qb - 1) * qb + span <= lk and qb % SUBLANE == 0 and span % SUBLANE == 0
    return pl.pallas_call(
        functools.partial(_win_body, qb=qb, nb=nb, span=span, pos0=pos0, kpos0=kpos0, hpg=hpg),
        grid=(b, NSA_KV_GROUPS, t // (qb * nb)),
        in_specs=[
            pl.BlockSpec((None, qb * nb, gwid), lambda i, g, j: (i, j, g)),
            pl.BlockSpec((None, lk, HEAD_DIM), lambda i, g, j: (i, 0, g)),
            pl.BlockSpec((None, lk, HEAD_DIM), lambda i, g, j: (i, 0, NSA_KV_GROUPS + g)),
        ],
        out_specs=pl.BlockSpec((None, qb * nb, gwid), lambda i, g, j: (i, j, g)),
        out_shape=jax.ShapeDtypeStruct((b, t, qw), F32),
        scratch_shapes=[pltpu.VMEM((nb, hpg * qb, LANE), F32),
                        pltpu.VMEM((nb, hpg * qb, span), _prob_dtype(qb))],
        compiler_params=_cparams("parallel", "parallel", "parallel"),
        name="win_attend",
    )(q, kext, kext)


def _row_tile(n, pref):
    t = min(n, pref)
    assert n % t == 0
    return t


def kernel(x_prompt, x_sample, mem_prompt, cache_mem_kv, cache_nsa_kv, cache_nsa_win, state_pool, page_table, g_norm_mix, g_norm_mlp, g_norm_mem, w_mem_kv, mem_qk_gain, w_out, w_mlp_up, w_mlp_down, w_in_pool, w_pool_group, pool_scale, w_in_nsa, nsa_gate_bias, nsa_qk_gain, cmp_pos, cmp_w1, cmp_w2):
    depth = g_norm_mix.shape[0]
    bp, tp, d = x_prompt.shape
    db, ts, _ = x_sample.shape
    n_pages = page_table.shape[1]
    page = cache_nsa_kv.shape[2]
    past_len = n_pages * page
    win_buf = cache_nsa_win.shape[2]
    tok_w = w_pool_group.shape[1] * w_pool_group.shape[2]
    n_heads = tok_w // HEAD_DIM
    n_gate = 3 * n_heads
    kvw = 4 * NSA_KV_GROUPS * HEAD_DIM
    winw = 2 * NSA_KV_GROUPS * HEAD_DIM
    assert win_buf == WINDOW and past_len % SEL_BLOCK == 0 and ts <= SUBLANE and tp % QBLOCK == 0

    tsp = SUBLANE
    xs = jnp.pad(x_sample, ((0, 0), (0, tsp - ts), (0, 0))).reshape(db * tsp, d)
    xp = x_prompt.reshape(bp * tp, d)
    n_p, n_s = bp * tp, db * tsp
    tm_p, tm_s = _row_tile(n_p, 256), n_s
    mem2d = mem_prompt.reshape(bp * mem_prompt.shape[1], d)
    mem_len = mem_prompt.shape[1]
    cache_view = cache_nsa_kv.reshape(cache_nsa_kv.shape[0], cache_nsa_kv.shape[1], page * PAGE_CHUNKS, HEAD_DIM)
    npg = min(16, n_pages)

    mem_kv_p, nsa_kv_p, nsa_kv_s, win_p, win_s, pool_p, pool_s = [], [], [], [], [], [], []
    mem_segs = ((2 * MEM_WIDTH, (1,) * MEM_HEADS + (None,) * MEM_HEADS),)
    pool_segs = ((tok_w, None), (MEM_WIDTH, None))
    nsa_segs = ((tok_w, (0,) * n_heads),
                (kvw, (None, None, None, None, 2, 2, None, None)),
                (winw, (3, 3, None, None)),
                (MEM_WIDTH, None),
                (LANE, None))

    wo_b = w_out.astype(BF16)
    wu_b = w_mlp_up.astype(BF16)
    wd_b = w_mlp_down.astype(BF16)
    wmem_b = w_mem_kv.astype(BF16)
    wpool_b = w_in_pool.astype(BF16)
    wg_b = w_pool_group.astype(BF16)
    kv_end = tok_w + kvw + winw
    wnsa_b = jnp.concatenate(
        [w_in_nsa[:, :, :kv_end], w_in_nsa[:, :, kv_end + n_gate:], w_in_nsa[:, :, kv_end:kv_end + n_gate],
         jnp.zeros(w_in_nsa.shape[:2] + (LANE - n_gate,), w_in_nsa.dtype)], axis=2).astype(BF16)
    mem_cache = cache_mem_kv.reshape(cache_mem_kv.shape[0], db, cache_mem_kv.shape[2] * 2 * MEM_HEADS, HEAD_DIM)

    for i in range(depth):
        li = i // 2
        (mkv,) = in_proj(mem2d, g_norm_mem[i], wmem_b, i, mem_qk_gain[i], mem_segs,
                         _row_tile(mem2d.shape[0], 256))
        mkv_p = mkv.reshape(bp, mem_len, 2 * MEM_WIDTH)
        mem_kv_p.append(mkv_p.reshape(bp, mem_len, 2, MEM_HEADS, HEAD_DIM))
        gated = None
        if i % 2 == 0:
            gains = jnp.ones((1, HEAD_DIM), F32)
            up, qmp = in_proj(xp, g_norm_mix[i], wpool_b, li, gains, pool_segs, tm_p)
            us, qms = in_proj(xs, g_norm_mix[i], wpool_b, li, gains, pool_segs, tm_s)
            up3 = up.reshape(bp, tp, tok_w)
            tok_p = pool_mix(up3, wg_b, li, pool_scale[li], 0, _row_tile(tp, 512)).reshape(n_p, tok_w)
            us3 = us.reshape(db, tsp, tok_w)[:, :ts]
            lead = _round_up(POOL_HIST + ts, SUBLANE) - (POOL_HIST + ts)
            ext = jnp.concatenate([state_pool[li], us3], axis=1)
            ext_pad = jnp.pad(ext, ((0, 0), (lead, 0), (0, 0)))
            l_ext = ext_pad.shape[1]
            tok_e = pool_mix(ext_pad, wg_b, li, pool_scale[li], past_len - POOL_HIST - lead, l_ext)
            tok_s = jnp.pad(tok_e[:, l_ext - ts:], ((0, 0), (0, tsp - ts), (0, 0))).reshape(n_s, tok_w)
            pool_p.append(up3[:, tp - POOL_HIST:])
            pool_s.append(ext[:, -POOL_HIST:])
        else:
            gains = nsa_qk_gain[li]
            bias = jnp.pad(nsa_gate_bias[li], (0, LANE - n_gate)).reshape(1, LANE)
            w1 = cmp_w1[li]
            half = CMP_STRIDE * HEAD_DIM
            w1k = jnp.concatenate([w1[0, :half], w1[0, half:]], axis=1).astype(BF16)
            w1v = jnp.concatenate([w1[1, :half], w1[1, half:]], axis=1).astype(BF16)
            w1b = w1.astype(BF16)
            w2b = cmp_w2[li].astype(BF16)

            q_p, kv4_p, wn_p, qmp, gt_p = in_proj(xp, g_norm_mix[i], wnsa_b, li, gains, nsa_segs, tm_p)
            q3 = q_p.reshape(bp, tp, tok_w)
            kv43 = kv4_p.reshape(bp, tp, kvw)
            wn3 = wn_p.reshape(bp, tp, winw)
            n_cmp = tp // CMP_STRIDE
            ncp = _round_up(n_cmp, LANE)
            ab = cmp_stage1_dense(kv43, w1k, w1v)
            ct = cmp_stage2([ab], cmp_pos[li], w1b, w2b, gains[1], ncp)
            oc_p, sel_p = cmp_attend_topk(q3, ct, 0, tp // SEL_BLOCK, _row_tile(tp, 512))
            os_p = slc_attend_prompt(q3, kv43, sel_p, _row_tile(tp, 512), _row_tile(tp, 512))
            kext = jnp.concatenate([jnp.zeros((bp, WINDOW, winw), F32), wn3], axis=1)
            ow_p = win_attend(q3, kext, QBLOCK, WINDOW + QBLOCK, 0, -WINDOW, nb=8 if tp % (8 * QBLOCK) == 0 else 1)
            tok_p = (oc_p.reshape(n_p, tok_w), os_p.reshape(n_p, tok_w), ow_p.reshape(n_p, tok_w))

            q_s, kv4_s, wn_s, qms, gt_s = in_proj(xs, g_norm_mix[i], wnsa_b, li, gains, nsa_segs, tm_s)
            qs3 = q_s.reshape(db, tsp, tok_w)
            kv4s3 = kv4_s.reshape(db, tsp, kvw)
            wns3 = wn_s.reshape(db, tsp, winw)
            new_rows = jnp.pad(kv4s3[:, :ts], ((0, 0), (0, page - ts), (0, 0)))
            n_rows = _round_up(past_len + ts, SEL_BLOCK)
            n_cmp_s = n_rows // CMP_STRIDE
            ncp_s = _round_up(n_cmp_s, LANE)
            ab_past = cmp_stage1_paged(cache_view, li, page_table, w1k, w1v, npg)
            ab_new = cmp_stage1_dense(new_rows, w1k, w1v)
            ct_s = cmp_stage2([ab_past, ab_new], cmp_pos[li], w1b, w2b, gains[1], ncp_s)
            oc_s, sel_s = cmp_attend_topk(qs3, ct_s, past_len, n_rows // SEL_BLOCK, tsp)
            os_s = slc_attend_paged(qs3, sel_s, kv4s3, cache_view, li, page_table, npg)
            wext = jnp.concatenate([cache_nsa_win[li].reshape(db, win_buf, winw), wns3[:, :ts]], axis=1)
            kext_s = jnp.pad(wext, ((0, 0), (0, tsp - ts), (0, 0)))
            ow_s = win_attend(qs3, kext_s, tsp, win_buf + tsp, past_len, past_len - win_buf)
            tok_s = (oc_s.reshape(n_s, tok_w), os_s.reshape(n_s, tok_w), ow_s.reshape(n_s, tok_w))
            gated = ((gt_p, bias), (gt_s, bias))

            nsa_kv_p.append(kv43.reshape(bp, tp, 4, NSA_KV_GROUPS, HEAD_DIM))
            nsa_kv_s.append(kv4s3[:, :ts].reshape(db, ts, 4, NSA_KV_GROUPS, HEAD_DIM))
            wlen = min(WINDOW, tp)
            win_p.append(wn3[:, tp - wlen:].reshape(bp, wlen, 2, NSA_KV_GROUPS, HEAD_DIM))
            win_s.append(wext[:, -win_buf:].reshape(db, win_buf, 2, NSA_KV_GROUPS, HEAD_DIM))

        hp = out_proj(tok_p, qmp, mkv_p, mem_qk_gain[i, 0], xp, wo_b, i, tm_p, tp, gates=gated and gated[0])
        hs = out_proj(tok_s, qms, mem_cache, mem_qk_gain[i, 0], xs, wo_b, i, tm_s, tsp, gates=gated and gated[1],
                      kv_layer=i)
        xp = mlp(hp, g_norm_mlp[i], wu_b, wd_b, i, _row_tile(n_p, 512), 1024)
        xs = mlp(hs, g_norm_mlp[i], wu_b, wd_b, i, tm_s, 1024)

    y_p = xp.reshape(bp, tp, d)
    y_s = xs.reshape(db, tsp, d)[:, :ts]
    return (y_p, y_s, jnp.stack(mem_kv_p), jnp.stack(nsa_kv_p), jnp.stack(nsa_kv_s),
            jnp.stack(win_p), jnp.stack(win_s), jnp.stack(pool_p), jnp.stack(pool_s))
```
